```python
import math
import jax, jax.numpy as jnp
from jax import lax
import numpy as np

D_MODEL = 1024
BATCH = 8
SEQ = 2048
DEPTH = 2
DEC_BATCH = 128
DEC_SEQ = 8
PAST_LEN = 16384
PAGE_SIZE = 128

N_EVEN = (DEPTH + 1) // 2
N_ODD = DEPTH // 2
D_A = D_MODEL // 2
CONV_W = 3
B_HEAD_DIM = 128
B_HEADS = D_MODEL // (2 * B_HEAD_DIM)
D_B = B_HEADS * B_HEAD_DIM
D_IN = 3 * D_A + 4 * D_B
CHUNK = 64
POOL_WINDOWS = (2, 4, 8, 16)
N_POOL_GROUPS = len(POOL_WINDOWS)
POOL_GROUP = D_MODEL // N_POOL_GROUPS
POOL_BUF = max(POOL_WINDOWS) - 1
D_FF = 2816
FFN_CONV_W = 3
EPS = 1e-6

kernel_name = "hybrid_shortconv_hgrn2_pool_convffn_step"


def rmsnorm(x, g):
    xf = x.astype(jnp.float32)
    y = xf * lax.rsqrt(jnp.mean(xf * xf, axis=-1, keepdims=True) + EPS)
    return (y * g.astype(jnp.float32)).astype(x.dtype)


def causal_dwconv(u, buf, w):
    K = w.shape[0]
    L = u.shape[1]
    ext = jnp.concatenate([buf.astype(u.dtype), u], axis=1)
    y = ext[:, 0:L] * w[0]
    for k in range(1, K):
        y = y + ext[:, k:k + L] * w[k]
    new_buf = ext[:, ext.shape[1] - (K - 1):]
    return y, new_buf


def hgrn2_recurrence(q, k, v, logf, s0):
    N, L, H, DK = q.shape
    c = math.gcd(L, CHUNK)
    nc = L // c

    def to_chunks(t):
        return t.astype(jnp.float32).reshape(N, nc, c, H, t.shape[-1]).transpose(1, 0, 3, 2, 4)

    qc, kc, vc, gc = to_chunks(q), to_chunks(k), to_chunks(v), to_chunks(logf)
    mask = jnp.tril(jnp.ones((c, c), dtype=bool))[:, :, None]

    def step(S, inp):
        qb, kb, vb, gb = inp
        G = jnp.cumsum(gb, axis=2)
        diff = G[:, :, :, None, :] - G[:, :, None, :, :]
        decay = jnp.exp(jnp.where(mask, diff, -jnp.inf))
        A = jnp.einsum('nhtd,nhsd,nhtsd->nhts', qb, kb, decay)
        o = (jnp.einsum('nhts,nhsv->nhtv', A, vb)
             + jnp.einsum('nhtd,nhdv->nhtv', qb * jnp.exp(G), S))
        G_last = G[:, :, -1:, :]
        S_new = (jnp.exp(G_last[:, :, 0, :])[..., None] * S
                 + jnp.einsum('nhsd,nhsv->nhdv', kb * jnp.exp(G_last - G), vb))
        return S_new, o

    S, o = lax.scan(step, s0.astype(jnp.float32), (qc, kc, vc, gc))
    o = o.transpose(1, 0, 3, 2, 4).reshape(N, L, H, v.shape[-1])
    return o, S


def even_mixer(h, conv_buf, s0, w_in, conv_w, lb, g_norm, w_out):
    N, L, _ = h.shape
    z = h @ w_in
    offs = [D_A, 2 * D_A, 3 * D_A, 3 * D_A + D_B, 3 * D_A + 2 * D_B, 3 * D_A + 3 * D_B]
    a_c, a_b, a_v, b_q, b_f, b_i, b_g = jnp.split(z, offs, axis=-1)
    conv_out, new_conv = causal_dwconv(a_c * a_v, conv_buf, conv_w)
    y_a = a_b * conv_out
    f = lb + (1.0 - lb) * jax.nn.sigmoid(b_f.astype(jnp.float32))
    logf = jnp.log(f)
    key = 1.0 - f
    hs = (N, L, B_HEADS, B_HEAD_DIM)
    o, S = hgrn2_recurrence(b_q.reshape(hs), key.reshape(hs), b_i.reshape(hs), logf.reshape(hs), s0)
    o = rmsnorm(o, g_norm.reshape(B_HEADS, B_HEAD_DIM)).reshape(N, L, D_B).astype(h.dtype)
    y_b = o * jax.nn.silu(b_g)
    y = jnp.concatenate([y_a, y_b], axis=-1) @ w_out
    return y, new_conv, S


def pool_mixer(h, buf, start, w_pool, scale):
    N, L, D = h.shape
    ext = jnp.concatenate([buf.astype(h.dtype), h], axis=1).astype(jnp.float32)
    cz = jnp.concatenate([jnp.zeros((N, 1, D), jnp.float32), jnp.cumsum(ext, axis=1)], axis=1)
    pos = start + jnp.arange(L)
    end = cz[:, POOL_BUF + 1:POOL_BUF + 1 + L]
    means = []
    for gi, w in enumerate(POOL_WINDOWS):
        sl = slice(gi * POOL_GROUP, (gi + 1) * POOL_GROUP)
        begin = cz[:, POOL_BUF + 1 - w:POOL_BUF + 1 - w + L, sl]
        cnt = jnp.minimum(w, pos + 1).astype(jnp.float32)[None, :, None]
        means.append((end[..., sl] - begin) / cnt)
    d = (jnp.concatenate(means, axis=-1) - ext[:, POOL_BUF:]).reshape(N, L, N_POOL_GROUPS, POOL_GROUP)
    y = jnp.einsum('nlgc,gce->nlge', d.astype(h.dtype), w_pool).reshape(N, L, D) * scale
    new_buf = ext[:, ext.shape[1] - POOL_BUF:].astype(h.dtype)
    return y, new_buf


def conv_ffn(h, buf, w_gu, conv_w, conv_b, w_down):
    gu = h @ w_gu
    g, u = jnp.split(gu, [D_FF], axis=-1)
    gc, new_buf = causal_dwconv(g, buf, conv_w)
    y = (jax.nn.silu(gc + conv_b) * u) @ w_down
    return y, new_buf


def trunk(x, start, s_conv, s_hgrn, s_pool, s_ffn, norm_mix, norm_ffn, norm_final, w_in, conv_a_w,
          hgrn_lower_bounds, hgrn_norm, w_out, pool_w, pool_scale, ffn_w_gu, ffn_conv_w, ffn_conv_b,
          ffn_w_down):
    lb_all = jnp.cumsum(jax.nn.softmax(hgrn_lower_bounds.astype(jnp.float32), axis=0), axis=0)
    new_conv, new_hgrn, new_pool, new_ffn = [], [], [], []
    for l in range(DEPTH):
        h = rmsnorm(x, norm_mix[l])
        if l % 2 == 0:
            e = l // 2
            y, cb, S = even_mixer(h, s_conv[e], s_hgrn[e], w_in[e], conv_a_w[e], lb_all[l],
                                  hgrn_norm[e], w_out[e])
            new_conv.append(cb)
            new_hgrn.append(S.astype(s_hgrn.dtype))
        else:
            o = l // 2
            y, pb = pool_mixer(h, s_pool[o], start, pool_w[o], pool_scale[o])
            new_pool.append(pb)
        x = x + y
        h = rmsnorm(x, norm_ffn[l])
        y, fb = conv_ffn(h, s_ffn[l], ffn_w_gu[l], ffn_conv_w[l], ffn_conv_b[l], ffn_w_down[l])
        new_ffn.append(fb)
        x = x + y
    out = rmsnorm(x, norm_final)
    return out, jnp.stack(new_conv), jnp.stack(new_hgrn), jnp.stack(new_pool), jnp.stack(new_ffn)


def setup_inputs(seed: int = 0) -> dict:
    key = jax.random.key(seed)
    ks = jax.random.split(key, 24)
    f32 = jnp.float32
    nrm = lambda k, s, sc: (jax.random.normal(k, s, f32) * sc)
    return {
        "x_prompt": nrm(ks[0], (BATCH, SEQ, D_MODEL), 1.0),
        "x_sample": nrm(ks[1], (DEC_BATCH, DEC_SEQ, D_MODEL), 1.0),
        "state_conv_a": nrm(ks[2], (N_EVEN, DEC_BATCH, CONV_W - 1, D_A), 1.0),
        "state_hgrn": nrm(ks[3], (N_EVEN, DEC_BATCH, B_HEADS, B_HEAD_DIM, B_HEAD_DIM), 0.5),
        "state_pool": nrm(ks[4], (N_ODD, DEC_BATCH, POOL_BUF, D_MODEL), 1.0),
        "state_ffn": nrm(ks[5], (DEPTH, DEC_BATCH, FFN_CONV_W - 1, D_FF), 1.0),
        "norm_mix": 1.0 + nrm(ks[6], (DEPTH, D_MODEL), 0.05),
        "norm_ffn": 1.0 + nrm(ks[7], (DEPTH, D_MODEL), 0.05),
        "norm_final": 1.0 + nrm(ks[8], (D_MODEL,), 0.05),
        "w_in": nrm(ks[9], (N_EVEN, D_MODEL, D_IN), D_MODEL ** -0.5),
        "conv_a_w": nrm(ks[10], (N_EVEN, CONV_W, D_A), 0.5),
        "hgrn_lower_bounds": nrm(ks[11], (DEPTH + 1, D_B), 0.5),
        "hgrn_norm": 1.0 + nrm(ks[12], (N_EVEN, D_B), 0.05),
        "w_out": nrm(ks[13], (N_EVEN, D_A + D_B, D_MODEL), (D_A + D_B) ** -0.5),
        "pool_w": nrm(ks[14], (N_ODD, N_POOL_GROUPS, POOL_GROUP, POOL_GROUP), POOL_GROUP ** -0.5),
        "pool_scale": 1.0 + nrm(ks[15], (N_ODD, D_MODEL), 0.1),
        "ffn_w_gu": nrm(ks[16], (DEPTH, D_MODEL, 2 * D_FF), D_MODEL ** -0.5),
        "ffn_conv_w": nrm(ks[17], (DEPTH, FFN_CONV_W, D_FF), 0.5),
        "ffn_conv_b": nrm(ks[18], (DEPTH, D_FF), 0.02),
        "ffn_w_down": nrm(ks[19], (DEPTH, D_FF, D_MODEL), D_FF ** -0.5),
    }


def reference(x_prompt, x_sample, state_conv_a, state_hgrn, state_pool, state_ffn, norm_mix, norm_ffn,
              norm_final, w_in, conv_a_w, hgrn_lower_bounds, hgrn_norm, w_out, pool_w, pool_scale,
              ffn_w_gu, ffn_conv_w, ffn_conv_b, ffn_w_down):
    B = x_prompt.shape[0]
    dt = x_prompt.dtype
    z_conv = jnp.zeros((N_EVEN, B, CONV_W - 1, D_A), dt)
    z_hgrn = jnp.zeros((N_EVEN, B, B_HEADS, B_HEAD_DIM, B_HEAD_DIM), dt)
    z_pool = jnp.zeros((N_ODD, B, POOL_BUF, D_MODEL), dt)
    z_ffn = jnp.zeros((DEPTH, B, FFN_CONV_W - 1, D_FF), dt)
    y_prompt, conv_p, hgrn_p, pool_p, ffn_p = trunk(
        x_prompt, 0, z_conv, z_hgrn, z_pool, z_ffn, norm_mix, norm_ffn, norm_final, w_in, conv_a_w,
        hgrn_lower_bounds, hgrn_norm, w_out, pool_w, pool_scale, ffn_w_gu, ffn_conv_w, ffn_conv_b,
        ffn_w_down)
    y_sample, conv_s, hgrn_s, pool_s, ffn_s = trunk(
        x_sample, PAST_LEN, state_conv_a, state_hgrn, state_pool, state_ffn, norm_mix, norm_ffn,
        norm_final, w_in, conv_a_w, hgrn_lower_bounds, hgrn_norm, w_out, pool_w, pool_scale, ffn_w_gu,
        ffn_conv_w, ffn_conv_b, ffn_w_down)
    return (y_prompt, y_sample, conv_p, conv_s, hgrn_p, hgrn_s, pool_p, pool_s, ffn_p, ffn_s)
```

```python
import functools

import jax
import jax.numpy as jnp
from jax import lax
from jax.experimental import pallas as pl
from jax.experimental.pallas import tpu as pltpu

D_MODEL = 1024
D_A = 512
N_HEADS = 4
HEAD = 128
D_B = N_HEADS * HEAD
D_IN = 3 * D_A + 4 * D_B
D_FF = 2816
POOL_WINDOWS = (2, 4, 8, 16)
POOL_GROUP = D_MODEL // len(POOL_WINDOWS)
POOL_BUF = max(POOL_WINDOWS) - 1
CONV_HIST = 2
PAST_LEN = 16384
EPS = 1e-6

SUBLANES = 8
HGRN_CHUNK = 64
DIAG = SUBLANES
VMEM_LIMIT = 56 * 1024 * 1024

F32 = jnp.float32
BF16 = jnp.bfloat16

_OFF_AC, _OFF_AB, _OFF_AV = 0, D_A, 2 * D_A
_OFF_Q, _OFF_F, _OFF_I, _OFF_G = (3 * D_A, 3 * D_A + D_B, 3 * D_A + 2 * D_B, 3 * D_A + 3 * D_B)


def _round_up(x, m):
    return (x + m - 1) // m * m


def _log2(x):
    assert x & (x - 1) == 0
    return x.bit_length() - 1


def _rms(x, g):
    return x * lax.rsqrt(jnp.mean(x * x, axis=-1, keepdims=True) + EPS) * g


def _dot(a, b):
    return jnp.dot(a, b, preferred_element_type=F32)


def _dot_nt(a, b):
    return lax.dot_general(a, b, (((1,), (1,)), ((), ())), preferred_element_type=F32)


def _dot_tn(a, b):
    return lax.dot_general(a, b, (((0,), (0,)), ((), ())), preferred_element_type=F32)


def _shifted(ext_ref, padr, tl, k_rows, cols=slice(None)):
    off = padr - k_rows
    assert off >= 0
    base = off // SUBLANES * SUBLANES
    rem = off - base
    if rem == 0:
        return ext_ref[pl.ds(off, tl), cols]
    win = ext_ref[pl.ds(base, tl + SUBLANES), cols]
    return pltpu.roll(win, SUBLANES - rem, 0)[SUBLANES:SUBLANES + tl, :]


def _hgrn_chunk(q, kk, v, lf, st, g_ref):
    c = q.shape[0]
    row = lax.broadcasted_iota(jnp.int32, (c, 1), 0)
    heads = [slice(h * HEAD, (h + 1) * HEAD) for h in range(N_HEADS)]

    gcum = lf
    s = 1
    while s < c:
        gcum = gcum + jnp.where(row >= s, pltpu.roll(gcum, s, 0), 0.0)
        s *= 2
    g_ref[...] = gcum

    sub = row & (DIAG - 1)
    o = [None] * N_HEADS
    for r in range(DIAG):
        if r == 0:
            p = q * kk
            vr = v
        else:
            decay = jnp.exp(jnp.minimum(gcum - pltpu.roll(gcum, r, 0), 0.0))
            p = q * pltpu.roll(kk, r, 0) * decay
            vr = pltpu.roll(v, r, 0)
        for h, sl in enumerate(heads):
            a = jnp.sum(p[:, sl], axis=-1, keepdims=True)
            if r > 0:
                a = jnp.where(sub >= r, a, 0.0)
            term = a * vr[:, sl]
            o[h] = term if o[h] is None else o[h] + term

    amat = [None] * N_HEADS
    b = DIAG
    if c > DIAG:
        rowi = lax.broadcasted_iota(jnp.int32, (c, c), 0)
        coli = lax.broadcasted_iota(jnp.int32, (c, c), 1)
    while b < c:
        segs = []
        for pair in range(c // (2 * b)):
            anchor = g_ref[pl.ds(pair * 2 * b + b - 1, 1), :]
            segs.append(gcum[pair * 2 * b:(pair + 1) * 2 * b, :] - anchor)
        e = segs[0] if len(segs) == 1 else jnp.concatenate(segs, axis=0)
        w = jnp.exp(-jnp.abs(e))
        right = ((row >> _log2(b)) & 1) == 1
        qb = jnp.where(right, q * w, 0.0).astype(BF16)
        kb = jnp.where(right, 0.0, kk * w).astype(BF16)
        same_pair = (rowi >> _log2(2 * b)) == (coli >> _log2(2 * b))
        for h, sl in enumerate(heads):
            ab = _dot_nt(qb[:, sl], kb[:, sl])
            if 2 * b < c:
                ab = jnp.where(same_pair, ab, 0.0)
            amat[h] = ab if amat[h] is None else amat[h] + ab
        b *= 2

    glast = g_ref[pl.ds(c - 1, 1), :]
    qg = (q * jnp.exp(gcum)).astype(BF16)
    kg = (kk * jnp.exp(glast - gcum)).astype(BF16)
    dec = jnp.exp(glast)
    vb = v.astype(BF16)
    outs, new_st = [], []
    for h, sl in enumerate(heads):
        oh = o[h] + _dot_nt(qg[:, sl], st[h].astype(BF16))
        if amat[h] is not None:
            oh = oh + _dot(amat[h].astype(BF16), vb[:, sl])
        outs.append(oh)
        new_st.append(dec[:, sl] * st[h] + _dot_tn(vb[:, sl], kg[:, sl]))
    return jnp.concatenate(outs, axis=1), new_st


def _mixer0_kernel(*refs, tl, chunk, prompt):
    if prompt:
        (x_ref, gn_ref, win_ref, cw_ref, lbp_ref, hn_ref, wout_ref,
         xo_ref, convo_ref, so_ref,
         z_ref, ext_ref, o_ref, y_ref, g_ref, st_ref) = refs
    else:
        (x_ref, e1_ref, e2_ref, s0_ref, gn_ref, win_ref, cw_ref, lbp_ref, hn_ref, wout_ref,
         xo_ref, uo_ref, so_ref,
         z_ref, o_ref, y_ref, g_ref) = refs

    if prompt:
        t = pl.program_id(1)

        @pl.when(t == 0)
        def _():
            ext_ref[0:SUBLANES, :] = jnp.zeros((SUBLANES, D_A), F32)
            st_ref[...] = jnp.zeros((N_HEADS, HEAD, HEAD), F32)

    x = x_ref[...]
    hb = _rms(x, gn_ref[...]).astype(BF16)
    z_ref[...] = _dot(hb, win_ref[...])

    u = z_ref[:, _OFF_AC:_OFF_AC + D_A] * z_ref[:, _OFF_AV:_OFF_AV + D_A]
    cw = cw_ref[...]
    if prompt:
        ext_ref[SUBLANES:SUBLANES + tl, :] = u
        u1 = _shifted(ext_ref, SUBLANES, tl, 1)
        u2 = _shifted(ext_ref, SUBLANES, tl, 2)
        carry = ext_ref[tl:tl + SUBLANES, :]
        ext_ref[0:SUBLANES, :] = carry

        @pl.when(t == pl.num_programs(1) - 1)
        def _():
            convo_ref[0] = carry
    else:
        tpos = lax.broadcasted_iota(jnp.int32, (tl, 1), 0) & (SUBLANES - 1)
        u1 = jnp.where(tpos >= 1, pltpu.roll(u, 1, 0), e1_ref[...])
        u2 = jnp.where(tpos >= 2, pltpu.roll(u, 2, 0), e2_ref[...])
        uo_ref[...] = u
    conv = u2 * cw[0:1, :] + u1 * cw[1:2, :] + u * cw[2:3, :]
    y_ref[:, 0:D_A] = (z_ref[:, _OFF_AB:_OFF_AB + D_A] * conv).astype(BF16)

    lbp = lbp_ref[...]
    lbe = jnp.exp(lbp - jnp.max(lbp, axis=0, keepdims=True))
    lb = lbe[0:1, :] / jnp.sum(lbe, axis=0, keepdims=True)

    def chunk_body(i, carry_unused):
        r0 = pl.multiple_of(i * chunk, chunk)
        rows = pl.ds(r0, chunk)
        f = lb + (1.0 - lb) * jax.nn.sigmoid(z_ref[rows, _OFF_F:_OFF_F + D_B])
        q = z_ref[rows, _OFF_Q:_OFF_Q + D_B]
        v = z_ref[rows, _OFF_I:_OFF_I + D_B]
        if prompt:
            st = [st_ref[h] for h in range(N_HEADS)]
        else:
            st = [s0_ref[i, h].T for h in range(N_HEADS)]
        o, new_st = _hgrn_chunk(q, 1.0 - f, v, jnp.log(f), st, g_ref)
        o_ref[rows, :] = o
        for h in range(N_HEADS):
            if prompt:
                st_ref[h] = new_st[h]
            else:
                so_ref[i, h] = new_st[h].T
        return carry_unused

    lax.fori_loop(0, tl // chunk, chunk_body, 0)

    if prompt:
        @pl.when(t == pl.num_programs(1) - 1)
        def _():
            for h in range(N_HEADS):
                so_ref[0, h] = st_ref[h].T

    hn = hn_ref[...]
    for h in range(N_HEADS):
        sl = slice(h * HEAD, (h + 1) * HEAD)
        on = _rms(o_ref[:, sl], hn[:, sl])
        gate = z_ref[:, _OFF_G + h * HEAD:_OFF_G + (h + 1) * HEAD]
        y_ref[:, D_A + h * HEAD:D_A + (h + 1) * HEAD] = (on * (gate * jax.nn.sigmoid(gate))).astype(BF16)

    xo_ref[...] = x + _dot(y_ref[...], wout_ref[...])


def _const_spec(shape):
    nd = len(shape)
    return pl.BlockSpec(shape, lambda *_: (0,) * nd, pipeline_mode=pl.Buffered(1))


def _mixer0_prompt(x2d, n_seq, seq_len, gn, win, cw, lbp, hn, wout, *, tl):
    nt = seq_len // tl
    kern = functools.partial(_mixer0_kernel, tl=tl, chunk=HGRN_CHUNK, prompt=True)
    row_spec = pl.BlockSpec((tl, D_MODEL), lambda n, t: (n * nt + t, 0))
    return pl.pallas_call(
        kern,
        grid=(n_seq, nt),
        in_specs=[row_spec, _const_spec(gn.shape), _const_spec(win.shape), _const_spec(cw.shape),
                  _const_spec(lbp.shape), _const_spec(hn.shape), _const_spec(wout.shape)],
        out_specs=[row_spec,
                   pl.BlockSpec((1, SUBLANES, D_A), lambda n, t: (n, 0, 0)),
                   pl.BlockSpec((1, N_HEADS, HEAD, HEAD), lambda n, t: (n, 0, 0, 0))],
        out_shape=[jax.ShapeDtypeStruct(x2d.shape, F32),
                   jax.ShapeDtypeStruct((n_seq, SUBLANES, D_A), F32),
                   jax.ShapeDtypeStruct((n_seq, N_HEADS, HEAD, HEAD), F32)],
        scratch_shapes=[pltpu.VMEM((tl, D_IN), F32),
                        pltpu.VMEM((SUBLANES + tl, D_A), F32),
                        pltpu.VMEM((tl, D_B), F32),
                        pltpu.VMEM((tl, D_A + D_B), BF16),
                        pltpu.VMEM((HGRN_CHUNK, D_B), F32),
                        pltpu.VMEM((N_HEADS, HEAD, HEAD), F32)],
        compiler_params=pltpu.CompilerParams(
            dimension_semantics=("arbitrary", "arbitrary"), vmem_limit_bytes=VMEM_LIMIT),
        name="mixer0_prompt",
    )(x2d, gn, win, cw, lbp, hn, wout)


def _mixer0_sample(x2d, e1, e2, s0, gn, win, cw, lbp, hn, wout, *, seq_len, bn):
    n_seq = s0.shape[0]
    tl = bn * seq_len
    assert seq_len == SUBLANES
    kern = functools.partial(_mixer0_kernel, tl=tl, chunk=seq_len, prompt=False)
    row_spec = pl.BlockSpec((tl, D_MODEL), lambda i: (i, 0))
    conv_spec = pl.BlockSpec((tl, D_A), lambda i: (i, 0))
    st_spec = pl.BlockSpec((bn, N_HEADS, HEAD, HEAD), lambda i: (i, 0, 0, 0))
    return pl.pallas_call(
        kern,
        grid=(n_seq // bn,),
        in_specs=[row_spec, conv_spec, conv_spec, st_spec,
                  _const_spec(gn.shape), _const_spec(win.shape), _const_spec(cw.shape),
                  _const_spec(lbp.shape), _const_spec(hn.shape), _const_spec(wout.shape)],
        out_specs=[row_spec, conv_spec, st_spec],
        out_shape=[jax.ShapeDtypeStruct(x2d.shape, F32),
                   jax.ShapeDtypeStruct((x2d.shape[0], D_A), F32),
                   jax.ShapeDtypeStruct(s0.shape, F32)],
        scratch_shapes=[pltpu.VMEM((tl, D_IN), F32),
                        pltpu.VMEM((tl, D_B), F32),
                        pltpu.VMEM((tl, D_A + D_B), BF16),
                        pltpu.VMEM((seq_len, D_B), F32)],
        compiler_params=pltpu.CompilerParams(
            dimension_semantics=("arbitrary",), vmem_limit_bytes=VMEM_LIMIT),
        name="mixer0_sample",
    )(x2d, e1, e2, s0, gn, win, cw, lbp, hn, wout)


def _ffn_kernel(x_ref, st_ref, gn_ref, wgu_ref, cw_ref, cb_ref, wd_ref, gf_ref,
                xo_ref, sto_ref, ext_ref, *, tl, ns, padr, final):
    t = pl.program_id(1)

    @pl.when(t == 0)
    def _():
        ext_ref[0:padr, :] = st_ref[0]

    x = x_ref[...]
    hb = _rms(x, gn_ref[...]).astype(BF16)
    g = _dot(hb, wgu_ref[:, 0:D_FF])
    ext_ref[padr:padr + tl, :] = g
    cw = cw_ref[...]
    gc = (_shifted(ext_ref, padr, tl, 2 * ns) * cw[0:1, :]
          + _shifted(ext_ref, padr, tl, ns) * cw[1:2, :]
          + g * cw[2:3, :] + cb_ref[...])
    up = _dot(hb, wgu_ref[:, D_FF:2 * D_FF])
    act = (gc * jax.nn.sigmoid(gc) * up).astype(BF16)
    xo = x + _dot(act, wd_ref[...])
    if final:
        xo = _rms(xo, gf_ref[...])
    xo_ref[...] = xo

    carry = ext_ref[tl:tl + padr, :]
    ext_ref[0:padr, :] = carry

    @pl.when(t == pl.num_programs(1) - 1)
    def _():
        sto_ref[0] = carry


def _ffn(x2d, state, gn, wgu, cw, cb, wd, gf, *, tl, ns, final):
    n_groups, padr, _ = state.shape
    rows_per_group = x2d.shape[0] // n_groups
    nt = rows_per_group // tl
    assert padr % SUBLANES == 0 and padr >= CONV_HIST * ns and (nt == 1 or tl >= padr)
    kern = functools.partial(_ffn_kernel, tl=tl, ns=ns, padr=padr, final=final)
    row_spec = pl.BlockSpec((tl, D_MODEL), lambda n, t: (n * nt + t, 0))
    st_spec = pl.BlockSpec((1, padr, D_FF), lambda n, t: (n, 0, 0))
    return pl.pallas_call(
        kern,
        grid=(n_groups, nt),
        in_specs=[row_spec, st_spec, _const_spec(gn.shape), _const_spec(wgu.shape),
                  _const_spec(cw.shape), _const_spec(cb.shape), _const_spec(wd.shape),
                  _const_spec(gf.shape)],
        out_specs=[row_spec, st_spec],
        out_shape=[jax.ShapeDtypeStruct(x2d.shape, F32),
                   jax.ShapeDtypeStruct(state.shape, F32)],
        scratch_shapes=[pltpu.VMEM((padr + tl, D_FF), F32)],
        compiler_params=pltpu.CompilerParams(
            dimension_semantics=("arbitrary", "arbitrary"), vmem_limit_bytes=VMEM_LIMIT),
        name="ffn_final" if final else "ffn",
    )(x2d, state, gn, wgu, cw, cb, wd, gf)


def _pool_kernel(x_ref, st_ref, gn_ref, wp_ref, sc_ref, xo_ref, sto_ref, ext_ref,
                 *, tl, ns, padr, start):
    t = pl.program_id(1)

    @pl.when(t == 0)
    def _():
        ext_ref[0:padr, :] = st_ref[0]

    x = x_ref[...]
    h = _rms(x, gn_ref[...])
    ext_ref[padr:padr + tl, :] = h
    row = lax.broadcasted_iota(jnp.int32, (tl, 1), 0)
    pos = start + ((t * tl + row) >> _log2(ns))
    ys = []
    for gi, w in enumerate(POOL_WINDOWS):
        cols = slice(gi * POOL_GROUP, (gi + 1) * POOL_GROUP)
        hg = h[:, cols]
        acc = hg
        for k in range(1, w):
            acc = acc + _shifted(ext_ref, padr, tl, k * ns, cols)
        cnt = jnp.minimum(w, pos + 1).astype(F32)
        d = acc / cnt - hg
        ys.append(_dot(d.astype(BF16), wp_ref[gi]))
    xo_ref[...] = x + jnp.concatenate(ys, axis=1) * sc_ref[...]

    carry = ext_ref[tl:tl + padr, :]
    ext_ref[0:padr, :] = carry

    @pl.when(t == pl.num_programs(1) - 1)
    def _():
        sto_ref[0] = carry


def _pool(x2d, state, gn, wp, sc, *, tl, ns, start):
    n_groups, padr, _ = state.shape
    rows_per_group = x2d.shape[0] // n_groups
    nt = rows_per_group // tl
    assert padr % SUBLANES == 0 and padr >= POOL_BUF * ns and (nt == 1 or tl >= padr)
    kern = functools.partial(_pool_kernel, tl=tl, ns=ns, padr=padr, start=start)
    row_spec = pl.BlockSpec((tl, D_MODEL), lambda n, t: (n * nt + t, 0))
    st_spec = pl.BlockSpec((1, padr, D_MODEL), lambda n, t: (n, 0, 0))
    return pl.pallas_call(
        kern,
        grid=(n_groups, nt),
        in_specs=[row_spec, st_spec, _const_spec(gn.shape), _const_spec(wp.shape),
                  _const_spec(sc.shape)],
        out_specs=[row_spec, st_spec],
        out_shape=[jax.ShapeDtypeStruct(x2d.shape, F32),
                   jax.ShapeDtypeStruct(state.shape, F32)],
        scratch_shapes=[pltpu.VMEM((padr + tl, D_MODEL), F32)],
        compiler_params=pltpu.CompilerParams(
            dimension_semantics=("arbitrary", "arbitrary"), vmem_limit_bytes=VMEM_LIMIT),
        name="pool",
    )(x2d, state, gn, wp, sc)


def _front_pad(state, padr):
    return jnp.pad(state, ((0, 0), (padr - state.shape[1], 0), (0, 0)))


def _trunk_tail(x1, n_groups, ns, start, st_ffn0, st_pool, st_ffn1, p, *, tl_ffn, tl_pool):
    x2, ffn0 = _ffn(x1, st_ffn0, p["norm_ffn"][0], p["wgu"][0], p["ffn_cw"][0], p["ffn_cb"][0],
                    p["wd"][0], p["norm_final"], tl=tl_ffn, ns=ns, final=False)
    x3, pool = _pool(x2, st_pool, p["norm_mix"][1], p["pool_w"], p["pool_scale"],
                     tl=tl_pool, ns=ns, start=start)
    y, ffn1 = _ffn(x3, st_ffn1, p["norm_ffn"][1], p["wgu"][1], p["ffn_cw"][1], p["ffn_cb"][1],
                   p["wd"][1], p["norm_final"], tl=tl_ffn, ns=ns, final=True)
    return y, ffn0, pool, ffn1


def kernel(x_prompt, x_sample, state_conv_a, state_hgrn, state_pool, state_ffn, norm_mix, norm_ffn,
           norm_final, w_in, conv_a_w, hgrn_lower_bounds, hgrn_norm, w_out, pool_w, pool_scale,
           ffn_w_gu, ffn_conv_w, ffn_conv_b, ffn_w_down):
    batch, seq, _ = x_prompt.shape
    dec_batch, dec_seq, _ = x_sample.shape

    p = dict(
        norm_mix=norm_mix[:, None, :], norm_ffn=norm_ffn[:, None, :], norm_final=norm_final[None, :],
        wgu=ffn_w_gu.astype(BF16), wd=ffn_w_down.astype(BF16), ffn_cw=ffn_conv_w,
        ffn_cb=ffn_conv_b[:, None, :], pool_w=pool_w[0].astype(BF16), pool_scale=pool_scale[0][None, :])
    win = w_in[0].astype(BF16)
    wout = w_out[0].astype(BF16)
    mix_args = (p["norm_mix"][0], win, conv_a_w[0], hgrn_lower_bounds, hgrn_norm[0][None, :], wout)

    xp = x_prompt.reshape(batch * seq, D_MODEL)
    x1, conv_p, hgrn_p = _mixer0_prompt(xp, batch, seq, *mix_args, tl=512)
    pad_ffn = _round_up(CONV_HIST, SUBLANES)
    pad_pool = _round_up(POOL_BUF, SUBLANES)
    zf = jnp.zeros((batch, pad_ffn, D_FF), F32)
    zp = jnp.zeros((batch, pad_pool, D_MODEL), F32)
    yp, ffn0_p, pool_p, ffn1_p = _trunk_tail(x1, batch, 1, 0, zf, zp, zf, p, tl_ffn=256, tl_pool=512)
    y_prompt = yp.reshape(batch, seq, D_MODEL)
    conv_prompt = conv_p[None, :, SUBLANES - CONV_HIST:, :]
    hgrn_prompt = hgrn_p[None]
    pool_prompt = pool_p[None, :, pad_pool - POOL_BUF:, :]
    ffn_prompt = jnp.stack([ffn0_p, ffn1_p])[:, :, pad_ffn - CONV_HIST:, :]

    xs = x_sample.reshape(dec_batch * dec_seq, D_MODEL)
    cs = state_conv_a[0]
    zeros_tail = jnp.zeros((dec_batch, dec_seq - 1, D_A), F32)
    e1 = jnp.concatenate([cs[:, 1:2], zeros_tail], axis=1).reshape(dec_batch * dec_seq, D_A)
    e2 = jnp.concatenate([cs, zeros_tail[:, 1:]], axis=1).reshape(dec_batch * dec_seq, D_A)
    x1s, us, hgrn_s = _mixer0_sample(xs, e1, e2, state_hgrn[0], *mix_args, seq_len=dec_seq, bn=16)
    conv_sample = us.reshape(dec_batch, dec_seq, D_A)[None, :, dec_seq - CONV_HIST:, :]

    def to_tm(a):
        return a.transpose(1, 0, 2).reshape(1, a.shape[1] * dec_batch, a.shape[2])

    def from_tm(a, rows):
        return a.reshape(rows, dec_batch, a.shape[-1]).transpose(1, 0, 2)

    x1t = to_tm(x1s.reshape(dec_batch, dec_seq, D_MODEL))[0]
    ys, ffn0_s, pool_s, ffn1_s = _trunk_tail(
        x1t, 1, dec_batch, PAST_LEN, to_tm(state_ffn[0]), to_tm(state_pool[0]), to_tm(state_ffn[1]), p,
        tl_ffn=256, tl_pool=dec_batch * dec_seq)
    y_sample = from_tm(ys, dec_seq)
    pool_sample = from_tm(pool_s[0], POOL_BUF)[None]
    ffn_sample = jnp.stack([from_tm(ffn0_s[0], CONV_HIST), from_tm(ffn1_s[0], CONV_HIST)])

    return (y_prompt, y_sample, conv_prompt, conv_sample, hgrn_prompt, hgrn_s[None],
            pool_prompt, pool_sample, ffn_prompt, ffn_sample)
```

```python
import functools

import jax
import jax.numpy as jnp
import numpy as np
from jax import lax
from jax.experimental import pallas as pl
from jax.experimental.pallas import tpu as pltpu

D_MODEL = 1024
D_A = 512
N_HEADS = 4
HEAD = 128
D_B = N_HEADS * HEAD
D_IN = 3 * D_A + 4 * D_B
D_FF = 2816
POOL_WINDOWS = (2, 4, 8, 16)
POOL_GROUP = D_MODEL // len(POOL_WINDOWS)
POOL_BUF = max(POOL_WINDOWS) - 1
CONV_HIST = 2
PAST_LEN = 16384
EPS = 1e-6
LOG2E = 1.4426950408889634

SUBLANES = 8
HGRN_CHUNK = 64
VMEM_LIMIT = 56 * 1024 * 1024

F32 = jnp.float32
BF16 = jnp.bfloat16

_OFF_AC, _OFF_AB, _OFF_AV = 0, D_A, 2 * D_A
_OFF_Q, _OFF_F, _OFF_I, _OFF_G = (3 * D_A, 3 * D_A + D_B, 3 * D_A + 2 * D_B, 3 * D_A + 3 * D_B)
_HEADS = [slice(h * HEAD, (h + 1) * HEAD) for h in range(N_HEADS)]


def _round_up(x, m):
    return (x + m - 1) // m * m


def _log2(x):
    assert x & (x - 1) == 0
    return x.bit_length() - 1


def _rms(x, g):
    return x * lax.rsqrt(jnp.mean(x * x, axis=-1, keepdims=True) + EPS) * g


def _dot(a, b):
    return jnp.dot(a, b, preferred_element_type=F32)


def _dot_nt(a, b):
    return lax.dot_general(a, b, (((1,), (1,)), ((), ())), preferred_element_type=F32)


def _dot_tn(a, b):
    return lax.dot_general(a, b, (((0,), (0,)), ((), ())), preferred_element_type=F32)


def _shifted(ext_ref, padr, tl, k_rows, cols=slice(None)):
    off = padr - k_rows
    assert off >= 0
    base = off // SUBLANES * SUBLANES
    rem = off - base
    if rem == 0:
        return ext_ref[pl.ds(off, tl), cols]
    win = ext_ref[pl.ds(base, tl + SUBLANES), cols]
    return pltpu.roll(win, SUBLANES - rem, 0)[SUBLANES:SUBLANES + tl, :]


def _levels(seg):
    return [1 << i for i in range(_log2(seg))]


def _selector(c, seg):
    t = np.arange(c)[:, None]
    u = np.arange(c)[None, :]
    same_seg = (t // seg) == (u // seg)
    blocks = [same_seg & (u <= t), same_seg & (u > t)]
    for b in _levels(seg):
        pair = (t // (2 * b)) == (u // (2 * b))
        anchor = (t // (2 * b)) * 2 * b + b - 1
        right = (t // b) % 2 == 1
        blocks.append(pair & np.where(right, (u > anchor) & (u <= t), (u > t) & (u <= anchor)))
    sel = np.concatenate(blocks, axis=0).astype(np.float32)
    return np.concatenate([sel, sel], axis=1)


def _hgrn_intra(q, kk, v, lf2, sel_ref, seg):
    c = q.shape[0]
    hi = lf2.astype(BF16)
    mid = (lf2 - hi.astype(F32)).astype(BF16)
    x = _dot(sel_ref[...], jnp.concatenate([hi, mid], axis=0))

    row = lax.broadcasted_iota(jnp.int32, (c, 1), 0)
    rowi = lax.broadcasted_iota(jnp.int32, (c, c), 0)
    coli = lax.broadcasted_iota(jnp.int32, (c, c), 1)
    diff = rowi ^ coli
    lower = rowi > coli

    qb, kb, vb = q.astype(BF16), kk.astype(BF16), v.astype(BF16)
    amat = [jnp.where(rowi == coli, _dot_nt(qb[:, sl], kb[:, sl]), 0.0) for sl in _HEADS]
    for li, b in enumerate(_levels(seg)):
        w = jnp.exp2(x[(2 + li) * c:(3 + li) * c])
        right = ((row >> _log2(b)) & 1) == 1
        ql = jnp.where(right, q * w, 0.0).astype(BF16)
        kl = jnp.where(right, 0.0, kk * w).astype(BF16)
        mask = lower & ((diff >> _log2(b)) == 1)
        for h, sl in enumerate(_HEADS):
            amat[h] = jnp.where(mask, _dot_nt(ql[:, sl], kl[:, sl]), amat[h])
    o = [_dot(amat[h].astype(BF16), vb[:, sl]) for h, sl in enumerate(_HEADS)]
    g2 = x[0:c]
    return o, q * jnp.exp2(g2), kk * jnp.exp2(x[c:2 * c]), g2


def _gates(z_ref, rows, lb):
    f = lb + (1.0 - lb) * jax.nn.sigmoid(z_ref[rows, _OFF_F:_OFF_F + D_B])
    return 1.0 - f, jnp.log(f) * LOG2E


def _mixer0_kernel(*refs, tl, chunk, prompt):
    if prompt:
        (x_ref, gn_ref, win_ref, cw_ref, lbp_ref, hn_ref, wout_ref, sel_ref,
         xo_ref, convo_ref, so_ref,
         z_ref, o_ref, y_ref, g_ref, ext_ref, st_ref) = refs
    else:
        (x_ref, e1_ref, e2_ref, s0_ref, gn_ref, win_ref, cw_ref, lbp_ref, hn_ref, wout_ref, sel_ref,
         xo_ref, uo_ref, so_ref,
         z_ref, o_ref, y_ref, g_ref, qg_ref, kg_ref) = refs

    if prompt:
        t = pl.program_id(1)

        @pl.when(t == 0)
        def _():
            ext_ref[0:SUBLANES, :] = jnp.zeros((SUBLANES, D_A), F32)
            st_ref[...] = jnp.zeros((N_HEADS, HEAD, HEAD), F32)

    x = x_ref[...]
    hb = _rms(x, gn_ref[...]).astype(BF16)
    z_ref[...] = _dot(hb, win_ref[...])

    u = z_ref[:, _OFF_AC:_OFF_AC + D_A] * z_ref[:, _OFF_AV:_OFF_AV + D_A]
    cw = cw_ref[...]
    if prompt:
        ext_ref[SUBLANES:SUBLANES + tl, :] = u
        u1 = _shifted(ext_ref, SUBLANES, tl, 1)
        u2 = _shifted(ext_ref, SUBLANES, tl, 2)
        carry = ext_ref[tl:tl + SUBLANES, :]
        ext_ref[0:SUBLANES, :] = carry

        @pl.when(t == pl.num_programs(1) - 1)
        def _():
            convo_ref[0] = carry
    else:
        tpos = lax.broadcasted_iota(jnp.int32, (tl, 1), 0) & (SUBLANES - 1)
        u1 = jnp.where(tpos >= 1, pltpu.roll(u, 1, 0), e1_ref[...])
        u2 = jnp.where(tpos >= 2, pltpu.roll(u, 2, 0), e2_ref[...])
        uo_ref[...] = u
    conv = u2 * cw[0:1, :] + u1 * cw[1:2, :] + u * cw[2:3, :]
    y_ref[:, 0:D_A] = (z_ref[:, _OFF_AB:_OFF_AB + D_A] * conv).astype(BF16)

    lbp = lbp_ref[...]
    lbe = jnp.exp(lbp - jnp.max(lbp, axis=0, keepdims=True))
    lb = lbe[0:1, :] / jnp.sum(lbe, axis=0, keepdims=True)

    if prompt:
        def chunk_body(i, carry_unused):
            rows = pl.ds(pl.multiple_of(i * chunk, chunk), chunk)
            kk, lf2 = _gates(z_ref, rows, lb)
            v = z_ref[rows, _OFF_I:_OFF_I + D_B]
            o, qg, kg, g2 = _hgrn_intra(z_ref[rows, _OFF_Q:_OFF_Q + D_B], kk, v, lf2, sel_ref, chunk)
            g_ref[...] = g2
            dec = jnp.exp2(g_ref[pl.ds(chunk - 1, 1), :])
            qg, kg, vb = qg.astype(BF16), kg.astype(BF16), v.astype(BF16)
            for h, sl in enumerate(_HEADS):
                st = st_ref[h]
                o_ref[rows, sl] = o[h] + _dot_nt(qg[:, sl], st.astype(BF16))
                st_ref[h] = dec[:, sl] * st + _dot_tn(vb[:, sl], kg[:, sl])
            return carry_unused

        lax.fori_loop(0, tl // chunk, chunk_body, 0, unroll=True)

        @pl.when(t == pl.num_programs(1) - 1)
        def _():
            for h in range(N_HEADS):
                so_ref[0, h] = st_ref[h].T
    else:
        kk, lf2 = _gates(z_ref, pl.ds(0, tl), lb)
        o, qg, kg, g2 = _hgrn_intra(z_ref[:, _OFF_Q:_OFF_Q + D_B], kk,
                                    z_ref[:, _OFF_I:_OFF_I + D_B], lf2, sel_ref, chunk)
        o_ref[...] = jnp.concatenate(o, axis=1)
        qg_ref[...] = qg
        kg_ref[...] = kg
        g_ref[...] = g2

        def seq_body(i, carry_unused):
            r0 = pl.multiple_of(i * chunk, chunk)
            rows = pl.ds(r0, chunk)
            dec = jnp.exp2(g_ref[pl.ds(r0 + chunk - 1, 1), :])
            qg_i = qg_ref[rows, :].astype(BF16)
            kg_i = kg_ref[rows, :].astype(BF16)
            vb = z_ref[rows, _OFF_I:_OFF_I + D_B].astype(BF16)
            for h, sl in enumerate(_HEADS):
                st = s0_ref[i, h].T
                o_ref[rows, sl] = o_ref[rows, sl] + _dot_nt(qg_i[:, sl], st.astype(BF16))
                so_ref[i, h] = (dec[:, sl] * st + _dot_tn(vb[:, sl], kg_i[:, sl])).T
            return carry_unused

        lax.fori_loop(0, tl // chunk, seq_body, 0)

    hn = hn_ref[...]
    for h, sl in enumerate(_HEADS):
        on = _rms(o_ref[:, sl], hn[:, sl])
        gate = z_ref[:, _OFF_G + h * HEAD:_OFF_G + (h + 1) * HEAD]
        y_ref[:, D_A + h * HEAD:D_A + (h + 1) * HEAD] = (on * (gate * jax.nn.sigmoid(gate))).astype(BF16)

    xo_ref[...] = x + _dot(y_ref[...], wout_ref[...])


def _const_spec(shape):
    nd = len(shape)
    return pl.BlockSpec(shape, lambda *_: (0,) * nd, pipeline_mode=pl.Buffered(1))


def _mixer0_prompt(x2d, n_seq, seq_len, gn, win, cw, lbp, hn, wout, *, tl):
    nt = seq_len // tl
    sel = jnp.asarray(_selector(HGRN_CHUNK, HGRN_CHUNK), BF16)
    kern = functools.partial(_mixer0_kernel, tl=tl, chunk=HGRN_CHUNK, prompt=True)
    row_spec = pl.BlockSpec((tl, D_MODEL), lambda n, t: (n * nt + t, 0))
    consts = (gn, win, cw, lbp, hn, wout, sel)
    return pl.pallas_call(
        kern,
        grid=(n_seq, nt),
        in_specs=[row_spec] + [_const_spec(a.shape) for a in consts],
        out_specs=[row_spec,
                   pl.BlockSpec((1, SUBLANES, D_A), lambda n, t: (n, 0, 0)),
                   pl.BlockSpec((1, N_HEADS, HEAD, HEAD), lambda n, t: (n, 0, 0, 0))],
        out_shape=[jax.ShapeDtypeStruct(x2d.shape, F32),
                   jax.ShapeDtypeStruct((n_seq, SUBLANES, D_A), F32),
                   jax.ShapeDtypeStruct((n_seq, N_HEADS, HEAD, HEAD), F32)],
        scratch_shapes=[pltpu.VMEM((tl, D_IN), F32),
                        pltpu.VMEM((tl, D_B), F32),
                        pltpu.VMEM((tl, D_A + D_B), BF16),
                        pltpu.VMEM((HGRN_CHUNK, D_B), F32),
                        pltpu.VMEM((SUBLANES + tl, D_A), F32),
                        pltpu.VMEM((N_HEADS, HEAD, HEAD), F32)],
        compiler_params=pltpu.CompilerParams(
            dimension_semantics=("arbitrary", "arbitrary"), vmem_limit_bytes=VMEM_LIMIT),
        name="mixer0_prompt",
    )(x2d, *consts)


def _mixer0_sample(x2d, e1, e2, s0, gn, win, cw, lbp, hn, wout, *, seq_len, bn):
    n_seq = s0.shape[0]
    tl = bn * seq_len
    sel = jnp.asarray(_selector(tl, seq_len), BF16)
    kern = functools.partial(_mixer0_kernel, tl=tl, chunk=seq_len, prompt=False)
    row_spec = pl.BlockSpec((tl, D_MODEL), lambda i: (i, 0))
    conv_spec = pl.BlockSpec((tl, D_A), lambda i: (i, 0))
    st_spec = pl.BlockSpec((bn, N_HEADS, HEAD, HEAD), lambda i: (i, 0, 0, 0))
    consts = (gn, win, cw, lbp, hn, wout, sel)
    return pl.pallas_call(
        kern,
        grid=(n_seq // bn,),
        in_specs=[row_spec, conv_spec, conv_spec, st_spec] + [_const_spec(a.shape) for a in consts],
        out_specs=[row_spec, conv_spec, st_spec],
        out_shape=[jax.ShapeDtypeStruct(x2d.shape, F32),
                   jax.ShapeDtypeStruct((x2d.shape[0], D_A), F32),
                   jax.ShapeDtypeStruct(s0.shape, F32)],
        scratch_shapes=[pltpu.VMEM((tl, D_IN), F32),
                        pltpu.VMEM((tl, D_B), F32),
                        pltpu.VMEM((tl, D_A + D_B), BF16),
                        pltpu.VMEM((tl, D_B), F32),
                        pltpu.VMEM((tl, D_B), F32),
                        pltpu.VMEM((tl, D_B), F32)],
        compiler_params=pltpu.CompilerParams(
            dimension_semantics=("arbitrary",), vmem_limit_bytes=VMEM_LIMIT),
        name="mixer0_sample",
    )(x2d, e1, e2, s0, *consts)


def _ffn_kernel(x_ref, st_ref, gn_ref, wgu_ref, cw_ref, cb_ref, wd_ref, gf_ref,
                xo_ref, sto_ref, ext_ref, *, tl, ns, padr, final):
    t = pl.program_id(1)

    @pl.when(t == 0)
    def _():
        ext_ref[0:padr, :] = st_ref[0]

    x = x_ref[...]
    hb = _rms(x, gn_ref[...]).astype(BF16)
    g = _dot(hb, wgu_ref[:, 0:D_FF])
    ext_ref[padr:padr + tl, :] = g
    cw = cw_ref[...]
    gc = (_shifted(ext_ref, padr, tl, 2 * ns) * cw[0:1, :]
          + _shifted(ext_ref, padr, tl, ns) * cw[1:2, :]
          + g * cw[2:3, :] + cb_ref[...])
    up = _dot(hb, wgu_ref[:, D_FF:2 * D_FF])
    act = (gc * jax.nn.sigmoid(gc) * up).astype(BF16)
    xo = x + _dot(act, wd_ref[...])
    if final:
        xo = _rms(xo, gf_ref[...])
    xo_ref[...] = xo

    carry = ext_ref[tl:tl + padr, :]
    ext_ref[0:padr, :] = carry

    @pl.when(t == pl.num_programs(1) - 1)
    def _():
        sto_ref[0] = carry


def _ffn(x2d, state, gn, wgu, cw, cb, wd, gf, *, tl, ns, final):
    n_groups, padr, _ = state.shape
    rows_per_group = x2d.shape[0] // n_groups
    nt = rows_per_group // tl
    assert padr % SUBLANES == 0 and padr >= CONV_HIST * ns and (nt == 1 or tl >= padr)
    kern = functools.partial(_ffn_kernel, tl=tl, ns=ns, padr=padr, final=final)
    row_spec = pl.BlockSpec((tl, D_MODEL), lambda n, t: (n * nt + t, 0))
    st_spec = pl.BlockSpec((1, padr, D_FF), lambda n, t: (n, 0, 0))
    return pl.pallas_call(
        kern,
        grid=(n_groups, nt),
        in_specs=[row_spec, st_spec, _const_spec(gn.shape), _const_spec(wgu.shape),
                  _const_spec(cw.shape), _const_spec(cb.shape), _const_spec(wd.shape),
                  _const_spec(gf.shape)],
        out_specs=[row_spec, st_spec],
        out_shape=[jax.ShapeDtypeStruct(x2d.shape, F32),
                   jax.ShapeDtypeStruct(state.shape, F32)],
        scratch_shapes=[pltpu.VMEM((padr + tl, D_FF), F32)],
        compiler_params=pltpu.CompilerParams(
            dimension_semantics=("arbitrary", "arbitrary"), vmem_limit_bytes=VMEM_LIMIT),
        name="ffn_final" if final else "ffn",
    )(x2d, state, gn, wgu, cw, cb, wd, gf)


def _pool_kernel(x_ref, st_ref, gn_ref, wp_ref, sc_ref, xo_ref, sto_ref, ext_ref,
                 *, tl, ns, padr, start):
    t = pl.program_id(1)

    @pl.when(t == 0)
    def _():
        ext_ref[0:padr, :] = st_ref[0]

    x = x_ref[...]
    h = _rms(x, gn_ref[...])
    ext_ref[padr:padr + tl, :] = h
    row = lax.broadcasted_iota(jnp.int32, (tl, 1), 0)
    pos = start + ((t * tl + row) >> _log2(ns))
    ys = []
    for gi, w in enumerate(POOL_WINDOWS):
        cols = slice(gi * POOL_GROUP, (gi + 1) * POOL_GROUP)
        hg = h[:, cols]
        acc = hg
        for k in range(1, w):
            acc = acc + _shifted(ext_ref, padr, tl, k * ns, cols)
        cnt = jnp.minimum(w, pos + 1).astype(F32)
        d = acc / cnt - hg
        ys.append(_dot(d.astype(BF16), wp_ref[gi]))
    xo_ref[...] = x + jnp.concatenate(ys, axis=1) * sc_ref[...]

    carry = ext_ref[tl:tl + padr, :]
    ext_ref[0:padr, :] = carry

    @pl.when(t == pl.num_programs(1) - 1)
    def _():
        sto_ref[0] = carry


def _pool(x2d, state, gn, wp, sc, *, tl, ns, start):
    n_groups, padr, _ = state.shape
    rows_per_group = x2d.shape[0] // n_groups
    nt = rows_per_group // tl
    assert padr % SUBLANES == 0 and padr >= POOL_BUF * ns and (nt == 1 or tl >= padr)
    kern = functools.partial(_pool_kernel, tl=tl, ns=ns, padr=padr, start=start)
    row_spec = pl.BlockSpec((tl, D_MODEL), lambda n, t: (n * nt + t, 0))
    st_spec = pl.BlockSpec((1, padr, D_MODEL), lambda n, t: (n, 0, 0))
    return pl.pallas_call(
        kern,
        grid=(n_groups, nt),
        in_specs=[row_spec, st_spec, _const_spec(gn.shape), _const_spec(wp.shape),
                  _const_spec(sc.shape)],
        out_specs=[row_spec, st_spec],
        out_shape=[jax.ShapeDtypeStruct(x2d.shape, F32),
                   jax.ShapeDtypeStruct(state.shape, F32)],
        scratch_shapes=[pltpu.VMEM((padr + tl, D_MODEL), F32)],
        compiler_params=pltpu.CompilerParams(
            dimension_semantics=("arbitrary", "arbitrary"), vmem_limit_bytes=VMEM_LIMIT),
        name="pool",
    )(x2d, state, gn, wp, sc)


def _trunk_tail(x1, ns, start, st_ffn0, st_pool, st_ffn1, p, *, tl_ffn, tl_pool):
    x2, ffn0 = _ffn(x1, st_ffn0, p["norm_ffn"][0], p["wgu"][0], p["ffn_cw"][0], p["ffn_cb"][0],
                    p["wd"][0], p["norm_final"], tl=tl_ffn, ns=ns, final=False)
    x3, pool = _pool(x2, st_pool, p["norm_mix"][1], p["pool_w"], p["pool_scale"],
                     tl=tl_pool, ns=ns, start=start)
    y, ffn1 = _ffn(x3, st_ffn1, p["norm_ffn"][1], p["wgu"][1], p["ffn_cw"][1], p["ffn_cb"][1],
                   p["wd"][1], p["norm_final"], tl=tl_ffn, ns=ns, final=True)
    return y, ffn0, pool, ffn1


def kernel(x_prompt, x_sample, state_conv_a, state_hgrn, state_pool, state_ffn, norm_mix, norm_ffn,
           norm_final, w_in, conv_a_w, hgrn_lower_bounds, hgrn_norm, w_out, pool_w, pool_scale,
           ffn_w_gu, ffn_conv_w, ffn_conv_b, ffn_w_down):
    batch, seq, _ = x_prompt.shape
    dec_batch, dec_seq, _ = x_sample.shape

    p = dict(
        norm_mix=norm_mix[:, None, :], norm_ffn=norm_ffn[:, None, :], norm_final=norm_final[None, :],
        wgu=ffn_w_gu.astype(BF16), wd=ffn_w_down.astype(BF16), ffn_cw=ffn_conv_w,
        ffn_cb=ffn_conv_b[:, None, :], pool_w=pool_w[0].astype(BF16), pool_scale=pool_scale[0][None, :])
    win = w_in[0].astype(BF16)
    wout = w_out[0].astype(BF16)
    mix_args = (p["norm_mix"][0], win, conv_a_w[0], hgrn_lower_bounds, hgrn_norm[0][None, :], wout)

    xp = x_prompt.reshape(batch * seq, D_MODEL)
    x1, conv_p, hgrn_p = _mixer0_prompt(xp, batch, seq, *mix_args, tl=512)
    pad_ffn = _round_up(CONV_HIST, SUBLANES)
    pad_pool = _round_up(POOL_BUF, SUBLANES)
    zf = jnp.zeros((batch, pad_ffn, D_FF), F32)
    zp = jnp.zeros((batch, pad_pool, D_MODEL), F32)
    yp, ffn0_p, pool_p, ffn1_p = _trunk_tail(x1, 1, 0, zf, zp, zf, p, tl_ffn=256, tl_pool=512)
    y_prompt = yp.reshape(batch, seq, D_MODEL)
    conv_prompt = conv_p[None, :, SUBLANES - CONV_HIST:, :]
    hgrn_prompt = hgrn_p[None]
    pool_prompt = pool_p[None, :, pad_pool - POOL_BUF:, :]
    ffn_prompt = jnp.stack([ffn0_p, ffn1_p])[:, :, pad_ffn - CONV_HIST:, :]

    xs = x_sample.reshape(dec_batch * dec_seq, D_MODEL)
    cs = state_conv_a[0]
    zeros_tail = jnp.zeros((dec_batch, dec_seq - 1, D_A), F32)
    e1 = jnp.concatenate([cs[:, 1:2], zeros_tail], axis=1).reshape(dec_batch * dec_seq, D_A)
    e2 = jnp.concatenate([cs, zeros_tail[:, 1:]], axis=1).reshape(dec_batch * dec_seq, D_A)
    x1s, us, hgrn_s = _mixer0_sample(xs, e1, e2, state_hgrn[0], *mix_args, seq_len=dec_seq, bn=16)
    conv_sample = us.reshape(dec_batch, dec_seq, D_A)[None, :, dec_seq - CONV_HIST:, :]

    def to_tm(a):
        return a.transpose(1, 0, 2).reshape(1, a.shape[1] * dec_batch, a.shape[2])

    def from_tm(a, rows):
        return a.reshape(rows, dec_batch, a.shape[-1]).transpose(1, 0, 2)

    x1t = to_tm(x1s.reshape(dec_batch, dec_seq, D_MODEL))[0]
    ys, ffn0_s, pool_s, ffn1_s = _trunk_tail(
        x1t, dec_batch, PAST_LEN, to_tm(state_ffn[0]), to_tm(state_pool[0]), to_tm(state_ffn[1]), p,
        tl_ffn=256, tl_pool=dec_batch * dec_seq)
    y_sample = from_tm(ys, dec_seq)
    pool_sample = from_tm(pool_s[0], POOL_BUF)[None]
    ffn_sample = jnp.stack([from_tm(ffn0_s[0], CONV_HIST), from_tm(ffn1_s[0], CONV_HIST)])

    return (y_prompt, y_sample, conv_prompt, conv_sample, hgrn_prompt, hgrn_s[None],
            pool_prompt, pool_sample, ffn_prompt, ffn_sample)
```

```python
import functools

import jax
import jax.numpy as jnp
import numpy as np
from jax import lax
from jax.experimental import pallas as pl
from jax.experimental.pallas import tpu as pltpu

D_MODEL = 1024
D_A = 512
N_HEADS = 4
HEAD = 128
D_B = N_HEADS * HEAD
D_IN = 3 * D_A + 4 * D_B
D_FF = 2816
POOL_WINDOWS = (2, 4, 8, 16)
POOL_GROUP = D_MODEL // len(POOL_WINDOWS)
POOL_BUF = max(POOL_WINDOWS) - 1
CONV_HIST = 2
PAST_LEN = 16384
EPS = 1e-6
LOG2E = 1.4426950408889634

SUBLANES = 8
HGRN_CHUNK = 64
VMEM_LIMIT = 56 * 1024 * 1024

F32 = jnp.float32
BF16 = jnp.bfloat16

_OFF_AC, _OFF_AB, _OFF_AV = 0, D_A, 2 * D_A
_OFF_Q, _OFF_F, _OFF_I, _OFF_G = (3 * D_A, 3 * D_A + D_B, 3 * D_A + 2 * D_B, 3 * D_A + 3 * D_B)
_HEADS = [slice(h * HEAD, (h + 1) * HEAD) for h in range(N_HEADS)]


def _round_up(x, m):
    return (x + m - 1) // m * m


def _log2(x):
    assert x & (x - 1) == 0
    return x.bit_length() - 1


def _rms(x, g):
    return x * lax.rsqrt(jnp.mean(x * x, axis=-1, keepdims=True) + EPS) * g


def _silu(x):
    return x * jax.nn.sigmoid(x)


def _dot(a, b):
    return jnp.dot(a, b, preferred_element_type=F32)


def _dot_nt(a, b):
    return lax.dot_general(a, b, (((1,), (1,)), ((), ())), preferred_element_type=F32)


def _dot_tn(a, b):
    return lax.dot_general(a, b, (((0,), (0,)), ((), ())), preferred_element_type=F32)


def _shifted(ext_ref, padr, n, k_rows, cols=slice(None), row0=0):
    off = padr + row0 - k_rows
    assert off >= 0
    base = off // SUBLANES * SUBLANES
    rem = off - base
    if rem == 0:
        return ext_ref[pl.ds(off, n), cols]
    win = ext_ref[pl.ds(base, n + SUBLANES), cols]
    return pltpu.roll(win, SUBLANES - rem, 0)[SUBLANES:SUBLANES + n, :]


def _levels(seg):
    return [1 << i for i in range(_log2(seg))]


def _selector(c, seg, with_total):
    t = np.arange(c)[:, None]
    u = np.arange(c)[None, :]
    same_seg = (t // seg) == (u // seg)
    blocks = [same_seg & (u <= t), same_seg & (u > t)]
    for b in _levels(seg):
        pair = (t // (2 * b)) == (u // (2 * b))
        anchor = (t // (2 * b)) * 2 * b + b - 1
        right = (t // b) % 2 == 1
        blocks.append(pair & np.where(right, (u > anchor) & (u <= t), (u > t) & (u <= anchor)))
    if with_total:
        assert seg == c
        blocks.append(np.ones((SUBLANES, c), bool))
    sel = np.concatenate(blocks, axis=0).astype(np.float32)
    return np.concatenate([sel, sel], axis=1)


def _split_bf16(a):
    hi = a.astype(BF16)
    return hi, (a - hi.astype(F32)).astype(BF16)


def _hgrn_intra(q, kk, v, lf2, sel_ref, seg):
    c = q.shape[0]
    x = _dot(sel_ref[...], jnp.concatenate(_split_bf16(lf2), axis=0))

    row = lax.broadcasted_iota(jnp.int32, (c, 1), 0)
    rowi = lax.broadcasted_iota(jnp.int32, (c, c), 0)
    coli = lax.broadcasted_iota(jnp.int32, (c, c), 1)
    diff = rowi ^ coli
    lower = rowi > coli

    qb, kb, vb = q.astype(BF16), kk.astype(BF16), v.astype(BF16)
    amat = [jnp.where(rowi == coli, _dot_nt(qb[:, sl], kb[:, sl]), 0.0) for sl in _HEADS]
    for li, b in enumerate(_levels(seg)):
        w = jnp.exp2(x[(2 + li) * c:(3 + li) * c])
        right = ((row >> _log2(b)) & 1) == 1
        ql = jnp.where(right, q * w, 0.0).astype(BF16)
        kl = jnp.where(right, 0.0, kk * w).astype(BF16)
        mask = lower & ((diff >> _log2(b)) == 1)
        for h, sl in enumerate(_HEADS):
            amat[h] = jnp.where(mask, _dot_nt(ql[:, sl], kl[:, sl]), amat[h])
    o = [_dot(amat[h].astype(BF16), vb[:, sl]) for h, sl in enumerate(_HEADS)]
    return o, q * jnp.exp2(x[0:c]), kk * jnp.exp2(x[c:2 * c]), x


def _gates(z_ref, rows, lb):
    f = lb + (1.0 - lb) * jax.nn.sigmoid(z_ref[rows, _OFF_F:_OFF_F + D_B])
    return 1.0 - f, jnp.log(f) * LOG2E


def _lower_bound(lbp_ref):
    lbp = lbp_ref[...]
    lbe = jnp.exp(lbp - jnp.max(lbp, axis=0, keepdims=True))
    return lbe[0:1, :] / jnp.sum(lbe, axis=0, keepdims=True)


def _mixer_b_out(z_ref, o_ref, y_ref, hn, rows):
    for h, sl in enumerate(_HEADS):
        on = _rms(o_ref[rows, sl], hn[:, sl])
        gate = z_ref[rows, _OFF_G + h * HEAD:_OFF_G + (h + 1) * HEAD]
        y_ref[rows, D_A + h * HEAD:D_A + (h + 1) * HEAD] = (on * _silu(gate)).astype(BF16)


def _mixer0_prompt_kernel(x_ref, gn_ref, win_ref, cw_ref, lbp_ref, hn_ref, wout_ref, sel_ref,
                          xo_ref, convo_ref, so_ref,
                          hb_ref, z_ref, o_ref, y_ref, ext_ref, st_ref, *, tl, chunk, nsplit):
    t = pl.program_id(1)

    @pl.when(t == 0)
    def _():
        ext_ref[0:SUBLANES, :] = jnp.zeros((SUBLANES, D_A), F32)
        st_ref[...] = jnp.zeros((N_HEADS, HEAD, HEAD), F32)

    hb_ref[...] = _rms(x_ref[...], gn_ref[...]).astype(BF16)
    lb = _lower_bound(lbp_ref)
    cw = cw_ref[...]
    hn = hn_ref[...]
    part = tl // nsplit
    n_lev = _log2(chunk)

    for s in range(nsplit):
        r0 = s * part
        rows = slice(r0, r0 + part)
        hb = hb_ref[rows, :]
        z_ref[rows, _OFF_Q:_OFF_G] = _dot(hb, win_ref[:, _OFF_Q:_OFF_G])

        for c0 in range(r0, r0 + part, chunk):
            crows = slice(c0, c0 + chunk)
            kk, lf2 = _gates(z_ref, crows, lb)
            v = z_ref[crows, _OFF_I:_OFF_I + D_B]
            o, qg, kg, x = _hgrn_intra(z_ref[crows, _OFF_Q:_OFF_Q + D_B], kk, v, lf2, sel_ref, chunk)
            dec = jnp.exp2(x[(2 + n_lev) * chunk:(2 + n_lev) * chunk + 1])
            qg, kg, vb = qg.astype(BF16), kg.astype(BF16), v.astype(BF16)
            for h, sl in enumerate(_HEADS):
                st = st_ref[h]
                o_ref[crows, sl] = o[h] + _dot_nt(qg[:, sl], st.astype(BF16))
                st_ref[h] = dec[:, sl] * st + _dot_tn(vb[:, sl], kg[:, sl])

        z_ref[rows, 0:_OFF_Q] = _dot(hb, win_ref[:, 0:_OFF_Q])
        z_ref[rows, _OFF_G:D_IN] = _dot(hb, win_ref[:, _OFF_G:D_IN])

        u = z_ref[rows, _OFF_AC:_OFF_AC + D_A] * z_ref[rows, _OFF_AV:_OFF_AV + D_A]
        ext_ref[SUBLANES + r0:SUBLANES + r0 + part, :] = u
        conv = (_shifted(ext_ref, SUBLANES, part, 2, row0=r0) * cw[0:1, :]
                + _shifted(ext_ref, SUBLANES, part, 1, row0=r0) * cw[1:2, :] + u * cw[2:3, :])
        y_ref[rows, 0:D_A] = (z_ref[rows, _OFF_AB:_OFF_AB + D_A] * conv).astype(BF16)
        _mixer_b_out(z_ref, o_ref, y_ref, hn, rows)
        xo_ref[rows, :] = x_ref[rows, :] + _dot(y_ref[rows, :], wout_ref[...])

    carry = ext_ref[tl:tl + SUBLANES, :]
    ext_ref[0:SUBLANES, :] = carry

    @pl.when(t == pl.num_programs(1) - 1)
    def _():
        convo_ref[0] = carry
        for h in range(N_HEADS):
            so_ref[0, h] = st_ref[h].T


def _mixer0_sample_kernel(x_ref, e1_ref, e2_ref, s0_ref, gn_ref, win_ref, cw_ref, lbp_ref, hn_ref,
                          wout_ref, sel_ref,
                          xo_ref, uo_ref, so_ref,
                          z_ref, o_ref, y_ref, lf_ref, qg_ref, kg_ref, *, tl, seq):
    x = x_ref[...]
    z_ref[...] = _dot(_rms(x, gn_ref[...]).astype(BF16), win_ref[...])

    u = z_ref[:, _OFF_AC:_OFF_AC + D_A] * z_ref[:, _OFF_AV:_OFF_AV + D_A]
    cw = cw_ref[...]
    tpos = lax.broadcasted_iota(jnp.int32, (tl, 1), 0) & (seq - 1)
    u1 = jnp.where(tpos >= 1, pltpu.roll(u, 1, 0), e1_ref[...])
    u2 = jnp.where(tpos >= 2, pltpu.roll(u, 2, 0), e2_ref[...])
    uo_ref[...] = u
    conv = u2 * cw[0:1, :] + u1 * cw[1:2, :] + u * cw[2:3, :]
    y_ref[:, 0:D_A] = (z_ref[:, _OFF_AB:_OFF_AB + D_A] * conv).astype(BF16)

    kk, lf2 = _gates(z_ref, slice(None), _lower_bound(lbp_ref))
    o, qg, kg, _ = _hgrn_intra(z_ref[:, _OFF_Q:_OFF_Q + D_B], kk, z_ref[:, _OFF_I:_OFF_I + D_B], lf2,
                               sel_ref, seq)
    o_ref[...] = jnp.concatenate(o, axis=1)
    qg_ref[...] = qg
    kg_ref[...] = kg
    lf_ref[...] = lf2
    ones = jnp.ones((seq, HEAD), BF16)

    def seq_body(i, carry_unused):
        rows = pl.ds(pl.multiple_of(i * seq, seq), seq)
        hi, mid = _split_bf16(lf_ref[rows, :])
        qg_i = qg_ref[rows, :].astype(BF16)
        kg_i = kg_ref[rows, :].astype(BF16)
        vb = z_ref[rows, _OFF_I:_OFF_I + D_B].astype(BF16)
        for h, sl in enumerate(_HEADS):
            s0 = s0_ref[i, h]
            o_ref[rows, sl] = o_ref[rows, sl] + _dot(qg_i[:, sl], s0.astype(BF16))
            g_last = _dot_tn(hi[:, sl], ones) + _dot_tn(mid[:, sl], ones)
            so_ref[i, h] = jnp.exp2(g_last) * s0 + _dot_tn(kg_i[:, sl], vb[:, sl])
        return carry_unused

    lax.fori_loop(0, tl // seq, seq_body, 0, unroll=4)

    _mixer_b_out(z_ref, o_ref, y_ref, hn_ref[...], slice(None))
    xo_ref[...] = x + _dot(y_ref[...], wout_ref[...])


def _const_spec(shape):
    nd = len(shape)
    return pl.BlockSpec(shape, lambda *_: (0,) * nd, pipeline_mode=pl.Buffered(1))


def _layer_spec(shape, layer):
    nd = len(shape) - 1
    return pl.BlockSpec((None,) + tuple(shape[1:]), lambda *_: (layer,) + (0,) * nd,
                        pipeline_mode=pl.Buffered(1))


def _mixer0_prompt(x2d, n_seq, seq_len, gn, win, cw, lbp, hn, wout, *, tl, nsplit):
    nt = seq_len // tl
    sel = jnp.asarray(_selector(HGRN_CHUNK, HGRN_CHUNK, True), BF16)
    kern = functools.partial(_mixer0_prompt_kernel, tl=tl, chunk=HGRN_CHUNK, nsplit=nsplit)
    row_spec = pl.BlockSpec((tl, D_MODEL), lambda n, t: (n * nt + t, 0))
    consts = (gn, win, cw, lbp, hn, wout, sel)
    return pl.pallas_call(
        kern,
        grid=(n_seq, nt),
        in_specs=[row_spec] + [_const_spec(a.shape) for a in consts],
        out_specs=[row_spec,
                   pl.BlockSpec((1, SUBLANES, D_A), lambda n, t: (n, 0, 0)),
                   pl.BlockSpec((1, N_HEADS, HEAD, HEAD), lambda n, t: (n, 0, 0, 0))],
        out_shape=[jax.ShapeDtypeStruct(x2d.shape, F32),
                   jax.ShapeDtypeStruct((n_seq, SUBLANES, D_A), F32),
                   jax.ShapeDtypeStruct((n_seq, N_HEADS, HEAD, HEAD), F32)],
        scratch_shapes=[pltpu.VMEM((tl, D_MODEL), BF16),
                        pltpu.VMEM((tl, D_IN), F32),
                        pltpu.VMEM((tl, D_B), F32),
                        pltpu.VMEM((tl, D_A + D_B), BF16),
                        pltpu.VMEM((SUBLANES + tl, D_A), F32),
                        pltpu.VMEM((N_HEADS, HEAD, HEAD), F32)],
        compiler_params=pltpu.CompilerParams(
            dimension_semantics=("arbitrary", "arbitrary"), vmem_limit_bytes=VMEM_LIMIT),
        name="mixer0_prompt",
    )(x2d, *consts)


def _mixer0_sample(x2d, e1, e2, s0, gn, win, cw, lbp, hn, wout, *, seq_len, bn):
    n_seq = s0.shape[0]
    tl = bn * seq_len
    sel = jnp.asarray(_selector(tl, seq_len, False), BF16)
    kern = functools.partial(_mixer0_sample_kernel, tl=tl, seq=seq_len)
    row_spec = pl.BlockSpec((tl, D_MODEL), lambda i: (i, 0))
    conv_spec = pl.BlockSpec((tl, D_A), lambda i: (i, 0))
    st_spec = pl.BlockSpec((bn, N_HEADS, HEAD, HEAD), lambda i: (i, 0, 0, 0))
    consts = (gn, win, cw, lbp, hn, wout, sel)
    return pl.pallas_call(
        kern,
        grid=(n_seq // bn,),
        in_specs=[row_spec, conv_spec, conv_spec, st_spec] + [_const_spec(a.shape) for a in consts],
        out_specs=[row_spec, conv_spec, st_spec],
        out_shape=[jax.ShapeDtypeStruct(x2d.shape, F32),
                   jax.ShapeDtypeStruct((x2d.shape[0], D_A), F32),
                   jax.ShapeDtypeStruct(s0.shape, F32)],
        scratch_shapes=[pltpu.VMEM((tl, D_IN), F32),
                        pltpu.VMEM((tl, D_B), F32),
                        pltpu.VMEM((tl, D_A + D_B), BF16),
                        pltpu.VMEM((tl, D_B), F32),
                        pltpu.VMEM((tl, D_B), F32),
                        pltpu.VMEM((tl, D_B), F32)],
        compiler_params=pltpu.CompilerParams(
            dimension_semantics=("arbitrary",), vmem_limit_bytes=VMEM_LIMIT),
        name="mixer0_sample",
    )(x2d, e1, e2, s0, *consts)


def _ffn_kernel(x_ref, st_ref, gn_ref, wgu_ref, cw_ref, cb_ref, wd_ref, gf_ref,
                xo_ref, sto_ref, ext_ref, *, tl, ns, padr, final):
    t = pl.program_id(1)

    @pl.when(t == 0)
    def _():
        ext_ref[0:padr, :] = st_ref[0]

    x = x_ref[...]
    hb = _rms(x, gn_ref[...]).astype(BF16)
    g = _dot(hb, wgu_ref[:, 0:D_FF])
    ext_ref[padr:padr + tl, :] = g
    cw = cw_ref[...]
    gc = (_shifted(ext_ref, padr, tl, 2 * ns) * cw[0:1, :]
          + _shifted(ext_ref, padr, tl, ns) * cw[1:2, :]
          + g * cw[2:3, :] + cb_ref[...])
    up = _dot(hb, wgu_ref[:, D_FF:2 * D_FF])
    act = (_silu(gc) * up).astype(BF16)
    xo = x + _dot(act, wd_ref[...])
    if final:
        xo = _rms(xo, gf_ref[...])
    xo_ref[...] = xo

    carry = ext_ref[tl:tl + padr, :]
    ext_ref[0:padr, :] = carry

    @pl.when(t == pl.num_programs(1) - 1)
    def _():
        sto_ref[0] = carry


def _ffn(x2d, state, p, layer, *, tl, ns, final):
    n_groups, padr, _ = state.shape
    rows_per_group = x2d.shape[0] // n_groups
    nt = rows_per_group // tl
    assert padr % SUBLANES == 0 and padr >= CONV_HIST * ns and (nt == 1 or tl >= padr)
    kern = functools.partial(_ffn_kernel, tl=tl, ns=ns, padr=padr, final=final)
    row_spec = pl.BlockSpec((tl, D_MODEL), lambda n, t: (n * nt + t, 0))
    st_spec = pl.BlockSpec((1, padr, D_FF), lambda n, t: (n, 0, 0))
    stacked = (p["norm_ffn"], p["wgu"], p["ffn_cw"], p["ffn_cb"], p["wd"])
    return pl.pallas_call(
        kern,
        grid=(n_groups, nt),
        in_specs=[row_spec, st_spec] + [_layer_spec(a.shape, layer) for a in stacked]
                 + [_const_spec(p["norm_final"].shape)],
        out_specs=[row_spec, st_spec],
        out_shape=[jax.ShapeDtypeStruct(x2d.shape, F32),
                   jax.ShapeDtypeStruct(state.shape, F32)],
        scratch_shapes=[pltpu.VMEM((padr + tl, D_FF), F32)],
        compiler_params=pltpu.CompilerParams(
            dimension_semantics=("arbitrary", "arbitrary"), vmem_limit_bytes=VMEM_LIMIT),
        name="ffn_final" if final else "ffn",
    )(x2d, state, *stacked, p["norm_final"])


def _pool_kernel(x_ref, st_ref, gn_ref, wp_ref, sc_ref, xo_ref, sto_ref, ext_ref,
                 *, tl, ns, padr, start):
    t = pl.program_id(1)

    @pl.when(t == 0)
    def _():
        ext_ref[0:padr, :] = st_ref[0]

    x = x_ref[...]
    h = _rms(x, gn_ref[...])
    ext_ref[padr:padr + tl, :] = h
    row = lax.broadcasted_iota(jnp.int32, (tl, 1), 0)
    pos = start + ((t * tl + row) >> _log2(ns))
    ys = []
    for gi, w in enumerate(POOL_WINDOWS):
        cols = slice(gi * POOL_GROUP, (gi + 1) * POOL_GROUP)
        hg = h[:, cols]
        acc = hg
        for k in range(1, w):
            acc = acc + _shifted(ext_ref, padr, tl, k * ns, cols)
        cnt = jnp.minimum(w, pos + 1).astype(F32)
        d = acc / cnt - hg
        ys.append(_dot(d.astype(BF16), wp_ref[gi]))
    xo_ref[...] = x + jnp.concatenate(ys, axis=1) * sc_ref[...]

    carry = ext_ref[tl:tl + padr, :]
    ext_ref[0:padr, :] = carry

    @pl.when(t == pl.num_programs(1) - 1)
    def _():
        sto_ref[0] = carry


def _pool(x2d, state, gn, wp, sc, *, tl, ns, start):
    n_groups, padr, _ = state.shape
    rows_per_group = x2d.shape[0] // n_groups
    nt = rows_per_group // tl
    assert padr % SUBLANES == 0 and padr >= POOL_BUF * ns and (nt == 1 or tl >= padr)
    kern = functools.partial(_pool_kernel, tl=tl, ns=ns, padr=padr, start=start)
    row_spec = pl.BlockSpec((tl, D_MODEL), lambda n, t: (n * nt + t, 0))
    st_spec = pl.BlockSpec((1, padr, D_MODEL), lambda n, t: (n, 0, 0))
    return pl.pallas_call(
        kern,
        grid=(n_groups, nt),
        in_specs=[row_spec, st_spec, _const_spec(gn.shape), _const_spec(wp.shape),
                  _const_spec(sc.shape)],
        out_specs=[row_spec, st_spec],
        out_shape=[jax.ShapeDtypeStruct(x2d.shape, F32),
                   jax.ShapeDtypeStruct(state.shape, F32)],
        scratch_shapes=[pltpu.VMEM((padr + tl, D_MODEL), F32)],
        compiler_params=pltpu.CompilerParams(
            dimension_semantics=("arbitrary", "arbitrary"), vmem_limit_bytes=VMEM_LIMIT),
        name="pool",
    )(x2d, state, gn, wp, sc)


def _trunk_tail(x1, ns, start, st_ffn0, st_pool, st_ffn1, p, *, tl_ffn, tl_pool):
    x2, ffn0 = _ffn(x1, st_ffn0, p, 0, tl=tl_ffn, ns=ns, final=False)
    x3, pool = _pool(x2, st_pool, p["norm_mix"][1], p["pool_w"], p["pool_scale"],
                     tl=tl_pool, ns=ns, start=start)
    y, ffn1 = _ffn(x3, st_ffn1, p, 1, tl=tl_ffn, ns=ns, final=True)
    return y, ffn0, pool, ffn1


def kernel(x_prompt, x_sample, state_conv_a, state_hgrn, state_pool, state_ffn, norm_mix, norm_ffn,
           norm_final, w_in, conv_a_w, hgrn_lower_bounds, hgrn_norm, w_out, pool_w, pool_scale,
           ffn_w_gu, ffn_conv_w, ffn_conv_b, ffn_w_down):
    batch, seq, _ = x_prompt.shape
    dec_batch, dec_seq, _ = x_sample.shape

    p = dict(
        norm_mix=norm_mix[:, None, :], norm_ffn=norm_ffn[:, None, :], norm_final=norm_final[None, :],
        wgu=ffn_w_gu.astype(BF16), wd=ffn_w_down.astype(BF16), ffn_cw=ffn_conv_w,
        ffn_cb=ffn_conv_b[:, None, :], pool_w=pool_w[0].astype(BF16), pool_scale=pool_scale[0][None, :])
    win = w_in[0].astype(BF16)
    wout = w_out[0].astype(BF16)
    mix_args = (p["norm_mix"][0], win, conv_a_w[0], hgrn_lower_bounds, hgrn_norm[0][None, :], wout)

    xp = x_prompt.reshape(batch * seq, D_MODEL)
    x1, conv_p, hgrn_p = _mixer0_prompt(xp, batch, seq, *mix_args, tl=512, nsplit=2)
    pad_ffn = _round_up(CONV_HIST, SUBLANES)
    pad_pool = _round_up(POOL_BUF, SUBLANES)
    zf = jnp.zeros((batch, pad_ffn, D_FF), F32)
    zp = jnp.zeros((batch, pad_pool, D_MODEL), F32)
    yp, ffn0_p, pool_p, ffn1_p = _trunk_tail(x1, 1, 0, zf, zp, zf, p, tl_ffn=256, tl_pool=512)
    y_prompt = yp.reshape(batch, seq, D_MODEL)
    conv_prompt = conv_p[None, :, SUBLANES - CONV_HIST:, :]
    hgrn_prompt = hgrn_p[None]
    pool_prompt = pool_p[None, :, pad_pool - POOL_BUF:, :]
    ffn_prompt = jnp.stack([ffn0_p, ffn1_p])[:, :, pad_ffn - CONV_HIST:, :]

    xs = x_sample.reshape(dec_batch * dec_seq, D_MODEL)
    cs = state_conv_a[0]
    zeros_tail = jnp.zeros((dec_batch, dec_seq - 1, D_A), F32)
    e1 = jnp.concatenate([cs[:, 1:2], zeros_tail], axis=1).reshape(dec_batch * dec_seq, D_A)
    e2 = jnp.concatenate([cs, zeros_tail[:, 1:]], axis=1).reshape(dec_batch * dec_seq, D_A)
    x1s, us, hgrn_s = _mixer0_sample(xs, e1, e2, state_hgrn[0], *mix_args, seq_len=dec_seq, bn=16)
    conv_sample = us.reshape(dec_batch, dec_seq, D_A)[None, :, dec_seq - CONV_HIST:, :]

    def to_tm(a):
        return a.transpose(1, 0, 2).reshape(1, a.shape[1] * dec_batch, a.shape[2])

    def from_tm(a, rows):
        return a.reshape(rows, dec_batch, a.shape[-1]).transpose(1, 0, 2)

    x1t = to_tm(x1s.reshape(dec_batch, dec_seq, D_MODEL))[0]
    ys, ffn0_s, pool_s, ffn1_s = _trunk_tail(
        x1t, dec_batch, PAST_LEN, to_tm(state_ffn[0]), to_tm(state_pool[0]), to_tm(state_ffn[1]), p,
        tl_ffn=256, tl_pool=dec_batch * dec_seq)
    y_sample = from_tm(ys, dec_seq)
    pool_sample = from_tm(pool_s[0], POOL_BUF)[None]
    ffn_sample = jnp.stack([from_tm(ffn0_s[0], CONV_HIST), from_tm(ffn1_s[0], CONV_HIST)])

    return (y_prompt, y_sample, conv_prompt, conv_sample, hgrn_prompt, hgrn_s[None],
            pool_prompt, pool_sample, ffn_prompt, ffn_sample)
```

```python
import functools

import jax
import jax.numpy as jnp
import numpy as np
from jax import lax
from jax.experimental import pallas as pl
from jax.experimental.pallas import tpu as pltpu

D_MODEL = 1024
D_A = 512
N_HEADS = 4
HEAD = 128
D_B = N_HEADS * HEAD
D_IN = 3 * D_A + 4 * D_B
D_FF = 2816
POOL_WINDOWS = (2, 4, 8, 16)
POOL_GROUP = D_MODEL // len(POOL_WINDOWS)
POOL_BUF = max(POOL_WINDOWS) - 1
CONV_HIST = 2
PAST_LEN = 16384
EPS = 1e-6
LOG2E = 1.4426950408889634

SUBLANES = 8
HGRN_CHUNK = 64
VMEM_LIMIT = 56 * 1024 * 1024

F32 = jnp.float32
BF16 = jnp.bfloat16

_OFF_AC, _OFF_AB, _OFF_AV = 0, D_A, 2 * D_A
_OFF_Q, _OFF_F, _OFF_I, _OFF_G = (3 * D_A, 3 * D_A + D_B, 3 * D_A + 2 * D_B, 3 * D_A + 3 * D_B)
_HEADS = [slice(h * HEAD, (h + 1) * HEAD) for h in range(N_HEADS)]


def _round_up(x, m):
    return (x + m - 1) // m * m


def _log2(x):
    assert x & (x - 1) == 0
    return x.bit_length() - 1


def _rms(x, g):
    return x * lax.rsqrt(jnp.mean(x * x, axis=-1, keepdims=True) + EPS) * g


def _silu(x):
    return x * jax.nn.sigmoid(x)


def _dot(a, b):
    return jnp.dot(a, b, preferred_element_type=F32)


def _dot_nt(a, b):
    return lax.dot_general(a, b, (((1,), (1,)), ((), ())), preferred_element_type=F32)


def _dot_tn(a, b):
    return lax.dot_general(a, b, (((0,), (0,)), ((), ())), preferred_element_type=F32)


def _shifted(ext_ref, padr, n, k_rows, cols=slice(None), row0=0):
    off = padr + row0 - k_rows
    assert off >= 0
    base = off // SUBLANES * SUBLANES
    rem = off - base
    if rem == 0:
        return ext_ref[pl.ds(off, n), cols]
    win = ext_ref[pl.ds(base, n + SUBLANES), cols]
    return pltpu.roll(win, SUBLANES - rem, 0)[SUBLANES:SUBLANES + n, :]


def _levels(seg):
    return [1 << i for i in range(_log2(seg))]


def _selector(c, seg, with_total):
    t = np.arange(c)[:, None]
    u = np.arange(c)[None, :]
    same_seg = (t // seg) == (u // seg)
    blocks = [same_seg & (u <= t), same_seg & (u > t)]
    for b in _levels(seg):
        pair = (t // (2 * b)) == (u // (2 * b))
        anchor = (t // (2 * b)) * 2 * b + b - 1
        right = (t // b) % 2 == 1
        blocks.append(pair & np.where(right, (u > anchor) & (u <= t), (u > t) & (u <= anchor)))
    if with_total:
        assert seg == c
        blocks.append(np.ones((SUBLANES, c), bool))
    sel = np.concatenate(blocks, axis=0).astype(np.float32)
    return np.concatenate([sel, sel], axis=1)


def _split_bf16(a):
    hi = a.astype(BF16)
    return hi, (a - hi.astype(F32)).astype(BF16)


def _hgrn_intra(q, kk, v, lf2, sel_ref, seg):
    c = q.shape[0]
    x = _dot(sel_ref[...], jnp.concatenate(_split_bf16(lf2), axis=0))

    row = lax.broadcasted_iota(jnp.int32, (c, 1), 0)
    rowi = lax.broadcasted_iota(jnp.int32, (c, c), 0)
    coli = lax.broadcasted_iota(jnp.int32, (c, c), 1)
    diff = rowi ^ coli
    lower = rowi > coli

    qb, kb, vb = q.astype(BF16), kk.astype(BF16), v.astype(BF16)
    amat = [jnp.where(rowi == coli, _dot_nt(qb[:, sl], kb[:, sl]), 0.0) for sl in _HEADS]
    for li, b in enumerate(_levels(seg)):
        w = jnp.exp2(x[(2 + li) * c:(3 + li) * c])
        right = ((row >> _log2(b)) & 1) == 1
        ql = jnp.where(right, q * w, 0.0).astype(BF16)
        kl = jnp.where(right, 0.0, kk * w).astype(BF16)
        mask = lower & ((diff >> _log2(b)) == 1)
        for h, sl in enumerate(_HEADS):
            amat[h] = jnp.where(mask, _dot_nt(ql[:, sl], kl[:, sl]), amat[h])
    o = [_dot(amat[h].astype(BF16), vb[:, sl]) for h, sl in enumerate(_HEADS)]
    return o, q * jnp.exp2(x[0:c]), kk * jnp.exp2(x[c:2 * c]), x


def _gates(z_ref, rows, lb):
    f = lb + (1.0 - lb) * jax.nn.sigmoid(z_ref[rows, _OFF_F:_OFF_F + D_B])
    return 1.0 - f, jnp.log(f) * LOG2E


def _lower_bound(lbp_ref):
    lbp = lbp_ref[...]
    lbe = jnp.exp(lbp - jnp.max(lbp, axis=0, keepdims=True))
    return lbe[0:1, :] / jnp.sum(lbe, axis=0, keepdims=True)


def _mixer_b_out(z_ref, o_ref, y_ref, hn, rows):
    for h, sl in enumerate(_HEADS):
        on = _rms(o_ref[rows, sl], hn[:, sl])
        gate = z_ref[rows, _OFF_G + h * HEAD:_OFF_G + (h + 1) * HEAD]
        y_ref[rows, D_A + h * HEAD:D_A + (h + 1) * HEAD] = (on * _silu(gate)).astype(BF16)


def _mixer0_prompt_kernel(x_ref, gn_ref, win_ref, cw_ref, lbp_ref, hn_ref, wout_ref, sel_ref,
                          xo_ref, convo_ref, so_ref,
                          hb_ref, z_ref, o_ref, y_ref, ext_ref, st_ref, *, tl, chunk, nsplit):
    t = pl.program_id(1)

    @pl.when(t == 0)
    def _():
        ext_ref[0:SUBLANES, :] = jnp.zeros((SUBLANES, D_A), F32)
        st_ref[...] = jnp.zeros((N_HEADS, HEAD, HEAD), F32)

    hb_ref[...] = _rms(x_ref[...], gn_ref[...]).astype(BF16)
    lb = _lower_bound(lbp_ref)
    cw = cw_ref[...]
    hn = hn_ref[...]
    part = tl // nsplit
    n_lev = _log2(chunk)

    for s in range(nsplit):
        r0 = s * part
        rows = slice(r0, r0 + part)
        hb = hb_ref[rows, :]
        z_ref[rows, _OFF_Q:_OFF_G] = _dot(hb, win_ref[:, _OFF_Q:_OFF_G])

        for c0 in range(r0, r0 + part, chunk):
            crows = slice(c0, c0 + chunk)
            kk, lf2 = _gates(z_ref, crows, lb)
            v = z_ref[crows, _OFF_I:_OFF_I + D_B]
            o, qg, kg, x = _hgrn_intra(z_ref[crows, _OFF_Q:_OFF_Q + D_B], kk, v, lf2, sel_ref, chunk)
            dec = jnp.exp2(x[(2 + n_lev) * chunk:(2 + n_lev) * chunk + 1])
            qg, kg, vb = qg.astype(BF16), kg.astype(BF16), v.astype(BF16)
            for h, sl in enumerate(_HEADS):
                st = st_ref[h]
                o_ref[crows, sl] = o[h] + _dot_nt(qg[:, sl], st.astype(BF16))
                st_ref[h] = dec[:, sl] * st + _dot_tn(vb[:, sl], kg[:, sl])

        z_ref[rows, 0:_OFF_Q] = _dot(hb, win_ref[:, 0:_OFF_Q])
        z_ref[rows, _OFF_G:D_IN] = _dot(hb, win_ref[:, _OFF_G:D_IN])

        u = z_ref[rows, _OFF_AC:_OFF_AC + D_A] * z_ref[rows, _OFF_AV:_OFF_AV + D_A]
        ext_ref[SUBLANES + r0:SUBLANES + r0 + part, :] = u
        conv = (_shifted(ext_ref, SUBLANES, part, 2, row0=r0) * cw[0:1, :]
                + _shifted(ext_ref, SUBLANES, part, 1, row0=r0) * cw[1:2, :] + u * cw[2:3, :])
        y_ref[rows, 0:D_A] = (z_ref[rows, _OFF_AB:_OFF_AB + D_A] * conv).astype(BF16)
        _mixer_b_out(z_ref, o_ref, y_ref, hn, rows)
        xo_ref[rows, :] = x_ref[rows, :] + _dot(y_ref[rows, :], wout_ref[...])

    carry = ext_ref[tl:tl + SUBLANES, :]
    ext_ref[0:SUBLANES, :] = carry

    @pl.when(t == pl.num_programs(1) - 1)
    def _():
        convo_ref[0] = carry
        for h in range(N_HEADS):
            so_ref[0, h] = st_ref[h].T


def _mixer0_sample_kernel(x_ref, e1_ref, e2_ref, s0_ref, gn_ref, win_ref, cw_ref, lbp_ref, hn_ref,
                          wout_ref, sel_ref,
                          xo_ref, uo_ref, so_ref,
                          z_ref, o_ref, y_ref, lf_ref, qg_ref, kg_ref, *, tl, seq):
    x = x_ref[...]
    z_ref[...] = _dot(_rms(x, gn_ref[...]).astype(BF16), win_ref[...])

    u = z_ref[:, _OFF_AC:_OFF_AC + D_A] * z_ref[:, _OFF_AV:_OFF_AV + D_A]
    cw = cw_ref[...]
    tpos = lax.broadcasted_iota(jnp.int32, (tl, 1), 0) & (seq - 1)
    u1 = jnp.where(tpos >= 1, pltpu.roll(u, 1, 0), e1_ref[...])
    u2 = jnp.where(tpos >= 2, pltpu.roll(u, 2, 0), e2_ref[...])
    uo_ref[...] = u
    conv = u2 * cw[0:1, :] + u1 * cw[1:2, :] + u * cw[2:3, :]
    y_ref[:, 0:D_A] = (z_ref[:, _OFF_AB:_OFF_AB + D_A] * conv).astype(BF16)

    kk, lf2 = _gates(z_ref, slice(None), _lower_bound(lbp_ref))
    o, qg, kg, _ = _hgrn_intra(z_ref[:, _OFF_Q:_OFF_Q + D_B], kk, z_ref[:, _OFF_I:_OFF_I + D_B], lf2,
                               sel_ref, seq)
    o_ref[...] = jnp.concatenate(o, axis=1)
    qg_ref[...] = qg
    kg_ref[...] = kg
    lf_ref[...] = lf2
    ones = jnp.ones((seq, HEAD), BF16)

    def seq_body(i, carry_unused):
        rows = pl.ds(pl.multiple_of(i * seq, seq), seq)
        hi, mid = _split_bf16(lf_ref[rows, :])
        qg_i = qg_ref[rows, :].astype(BF16)
        kg_i = kg_ref[rows, :].astype(BF16)
        vb = z_ref[rows, _OFF_I:_OFF_I + D_B].astype(BF16)
        for h, sl in enumerate(_HEADS):
            s0 = s0_ref[i, h]
            o_ref[rows, sl] = o_ref[rows, sl] + _dot(qg_i[:, sl], s0.astype(BF16))
            g_last = _dot_tn(hi[:, sl], ones) + _dot_tn(mid[:, sl], ones)
            so_ref[i, h] = jnp.exp2(g_last) * s0 + _dot_tn(kg_i[:, sl], vb[:, sl])
        return carry_unused

    lax.fori_loop(0, tl // seq, seq_body, 0, unroll=4)

    _mixer_b_out(z_ref, o_ref, y_ref, hn_ref[...], slice(None))
    xo_ref[...] = x + _dot(y_ref[...], wout_ref[...])


def _const_spec(shape):
    nd = len(shape)
    return pl.BlockSpec(shape, lambda *_: (0,) * nd, pipeline_mode=pl.Buffered(1))


def _layer_spec(shape, layer):
    nd = len(shape) - 1
    return pl.BlockSpec((None,) + tuple(shape[1:]), lambda *_: (layer,) + (0,) * nd,
                        pipeline_mode=pl.Buffered(1))


def _mixer0_prompt(x2d, n_seq, seq_len, gn, win, cw, lbp, hn, wout, *, tl, nsplit):
    nt = seq_len // tl
    sel = jnp.asarray(_selector(HGRN_CHUNK, HGRN_CHUNK, True), BF16)
    kern = functools.partial(_mixer0_prompt_kernel, tl=tl, chunk=HGRN_CHUNK, nsplit=nsplit)
    row_spec = pl.BlockSpec((tl, D_MODEL), lambda n, t: (n * nt + t, 0))
    consts = (gn, win, cw, lbp, hn, wout, sel)
    return pl.pallas_call(
        kern,
        grid=(n_seq, nt),
        in_specs=[row_spec] + [_const_spec(a.shape) for a in consts],
        out_specs=[row_spec,
                   pl.BlockSpec((1, SUBLANES, D_A), lambda n, t: (n, 0, 0)),
                   pl.BlockSpec((1, N_HEADS, HEAD, HEAD), lambda n, t: (n, 0, 0, 0))],
        out_shape=[jax.ShapeDtypeStruct(x2d.shape, F32),
                   jax.ShapeDtypeStruct((n_seq, SUBLANES, D_A), F32),
                   jax.ShapeDtypeStruct((n_seq, N_HEADS, HEAD, HEAD), F32)],
        scratch_shapes=[pltpu.VMEM((tl, D_MODEL), BF16),
                        pltpu.VMEM((tl, D_IN), F32),
                        pltpu.VMEM((tl, D_B), F32),
                        pltpu.VMEM((tl, D_A + D_B), BF16),
                        pltpu.VMEM((SUBLANES + tl, D_A), F32),
                        pltpu.VMEM((N_HEADS, HEAD, HEAD), F32)],
        compiler_params=pltpu.CompilerParams(
            dimension_semantics=("arbitrary", "arbitrary"), vmem_limit_bytes=VMEM_LIMIT),
        name="mixer0_prompt",
    )(x2d, *consts)


def _mixer0_sample(x2d, e1, e2, s0, gn, win, cw, lbp, hn, wout, *, seq_len, bn):
    n_seq = s0.shape[0]
    tl = bn * seq_len
    sel = jnp.asarray(_selector(tl, seq_len, False), BF16)
    kern = functools.partial(_mixer0_sample_kernel, tl=tl, seq=seq_len)
    row_spec = pl.BlockSpec((tl, D_MODEL), lambda i: (i, 0))
    conv_spec = pl.BlockSpec((tl, D_A), lambda i: (i, 0))
    st_spec = pl.BlockSpec((bn, N_HEADS, HEAD, HEAD), lambda i: (i, 0, 0, 0))
    consts = (gn, win, cw, lbp, hn, wout, sel)
    return pl.pallas_call(
        kern,
        grid=(n_seq // bn,),
        in_specs=[row_spec, conv_spec, conv_spec, st_spec] + [_const_spec(a.shape) for a in consts],
        out_specs=[row_spec, conv_spec, st_spec],
        out_shape=[jax.ShapeDtypeStruct(x2d.shape, F32),
                   jax.ShapeDtypeStruct((x2d.shape[0], D_A), F32),
                   jax.ShapeDtypeStruct(s0.shape, F32)],
        scratch_shapes=[pltpu.VMEM((tl, D_IN), F32),
                        pltpu.VMEM((tl, D_B), F32),
                        pltpu.VMEM((tl, D_A + D_B), BF16),
                        pltpu.VMEM((tl, D_B), F32),
                        pltpu.VMEM((tl, D_B), F32),
                        pltpu.VMEM((tl, D_B), F32)],
        compiler_params=pltpu.CompilerParams(
            dimension_semantics=("arbitrary",), vmem_limit_bytes=VMEM_LIMIT),
        name="mixer0_sample",
    )(x2d, e1, e2, s0, *consts)


def _pool_mix(x, gm, ext_ref, padr, row0, ns, pos0, wp_ref, sc):
    n = x.shape[0]
    h = _rms(x, gm)
    ext_ref[padr + row0:padr + row0 + n, :] = h
    pos = pos0 + (lax.broadcasted_iota(jnp.int32, (n, 1), 0) >> _log2(ns))
    ys = []
    for gi, w in enumerate(POOL_WINDOWS):
        cols = slice(gi * POOL_GROUP, (gi + 1) * POOL_GROUP)
        hg = h[:, cols]
        if ns == 1:
            hist = _round_up(POOL_BUF, SUBLANES)
            s = ext_ref[pl.ds(padr + row0 - hist, n + hist), cols]
            sh = 1
            while sh < w:
                s = s + pltpu.roll(s, sh, 0)
                sh *= 2
            acc = s[hist:hist + n, :]
        else:
            acc = hg
            for k in range(1, w):
                acc = acc + _shifted(ext_ref, padr, n, k * ns, cols, row0)
        cnt = jnp.minimum(w, pos + 1).astype(F32)
        ys.append(_dot((acc / cnt - hg).astype(BF16), wp_ref[gi]))
    return x + jnp.concatenate(ys, axis=1) * sc


def _ffn_kernel(*refs, tl, ns, padr, final, pool, nsplit):
    if pool is None:
        (x_ref, st_ref, gn_ref, wgu_ref, cw_ref, cb_ref, wd_ref, gf_ref,
         xo_ref, sto_ref, ext_ref) = refs
    else:
        (x_ref, st_ref, pst_ref, gm_ref, wp_ref, sc_ref, gn_ref, wgu_ref, cw_ref, cb_ref, wd_ref, gf_ref,
         xo_ref, sto_ref, psto_ref, ext_ref, pext_ref) = refs
        ppad, start = pool
    t = pl.program_id(1)

    @pl.when(t == 0)
    def _():
        ext_ref[0:padr, :] = st_ref[0]
        if pool is not None:
            pext_ref[0:ppad, :] = pst_ref[0]

    cw = cw_ref[...]
    cb = cb_ref[...]
    part = tl // nsplit
    for s in range(nsplit):
        r0 = s * part
        rows = slice(r0, r0 + part)
        x = x_ref[rows, :]
        if pool is not None:
            pos0 = start + ((t * tl + r0) >> _log2(ns))
            x = _pool_mix(x, gm_ref[...], pext_ref, ppad, r0, ns, pos0, wp_ref, sc_ref[...])
        hb = _rms(x, gn_ref[...]).astype(BF16)
        g = _dot(hb, wgu_ref[:, 0:D_FF])
        ext_ref[padr + r0:padr + r0 + part, :] = g
        gc = (_shifted(ext_ref, padr, part, 2 * ns, row0=r0) * cw[0:1, :]
              + _shifted(ext_ref, padr, part, ns, row0=r0) * cw[1:2, :]
              + g * cw[2:3, :] + cb)
        up = _dot(hb, wgu_ref[:, D_FF:2 * D_FF])
        act = (_silu(gc) * up).astype(BF16)
        xo = x + _dot(act, wd_ref[...])
        if final:
            xo = _rms(xo, gf_ref[...])
        xo_ref[rows, :] = xo

    carry = ext_ref[tl:tl + padr, :]
    ext_ref[0:padr, :] = carry
    if pool is not None:
        pcarry = pext_ref[tl:tl + ppad, :]
        pext_ref[0:ppad, :] = pcarry

    @pl.when(t == pl.num_programs(1) - 1)
    def _():
        sto_ref[0] = carry
        if pool is not None:
            psto_ref[0] = pcarry


def _ffn(x2d, state, p, layer, *, tl, ns, final, nsplit=1, pool_state=None, start=0):
    n_groups, padr, _ = state.shape
    rows_per_group = x2d.shape[0] // n_groups
    nt = rows_per_group // tl
    assert padr % SUBLANES == 0 and padr >= CONV_HIST * ns and (nt == 1 or tl >= padr)
    row_spec = pl.BlockSpec((tl, D_MODEL), lambda n, t: (n * nt + t, 0))
    st_spec = pl.BlockSpec((1, padr, D_FF), lambda n, t: (n, 0, 0))
    stacked = (p["norm_ffn"], p["wgu"], p["ffn_cw"], p["ffn_cb"], p["wd"])
    ins = [x2d, state]
    in_specs = [row_spec, st_spec]
    out_specs = [row_spec, st_spec]
    out_shape = [jax.ShapeDtypeStruct(x2d.shape, F32), jax.ShapeDtypeStruct(state.shape, F32)]
    scratch = [pltpu.VMEM((padr + tl, D_FF), F32)]
    pool = None
    if pool_state is not None:
        ppad = pool_state.shape[1]
        assert ppad % SUBLANES == 0 and ppad >= POOL_BUF * ns and (nt == 1 or tl >= ppad)
        pool = (ppad, start)
        pst_spec = pl.BlockSpec((1, ppad, D_MODEL), lambda n, t: (n, 0, 0))
        pool_consts = (p["norm_mix"][layer], p["pool_w"], p["pool_scale"])
        ins += [pool_state, *pool_consts]
        in_specs += [pst_spec] + [_const_spec(a.shape) for a in pool_consts]
        out_specs.append(pst_spec)
        out_shape.append(jax.ShapeDtypeStruct(pool_state.shape, F32))
        scratch.append(pltpu.VMEM((ppad + tl, D_MODEL), F32))
    kern = functools.partial(_ffn_kernel, tl=tl, ns=ns, padr=padr, final=final, pool=pool,
                             nsplit=nsplit)
    return pl.pallas_call(
        kern,
        grid=(n_groups, nt),
        in_specs=in_specs + [_layer_spec(a.shape, layer) for a in stacked]
                 + [_const_spec(p["norm_final"].shape)],
        out_specs=out_specs,
        out_shape=out_shape,
        scratch_shapes=scratch,
        compiler_params=pltpu.CompilerParams(
            dimension_semantics=("arbitrary", "arbitrary"), vmem_limit_bytes=VMEM_LIMIT),
        name=("pool_" if pool else "") + ("ffn_final" if final else "ffn"),
    )(*ins, *stacked, p["norm_final"])


def _pool_kernel(x_ref, st_ref, gn_ref, wp_ref, sc_ref, xo_ref, sto_ref, ext_ref,
                 *, tl, ns, padr, start):
    t = pl.program_id(1)

    @pl.when(t == 0)
    def _():
        ext_ref[0:padr, :] = st_ref[0]

    pos0 = start + ((t * tl) >> _log2(ns))
    xo_ref[...] = _pool_mix(x_ref[...], gn_ref[...], ext_ref, padr, 0, ns, pos0, wp_ref, sc_ref[...])

    carry = ext_ref[tl:tl + padr, :]
    ext_ref[0:padr, :] = carry

    @pl.when(t == pl.num_programs(1) - 1)
    def _():
        sto_ref[0] = carry


def _pool(x2d, state, gn, wp, sc, *, tl, ns, start):
    n_groups, padr, _ = state.shape
    rows_per_group = x2d.shape[0] // n_groups
    nt = rows_per_group // tl
    assert padr % SUBLANES == 0 and padr >= POOL_BUF * ns and (nt == 1 or tl >= padr)
    kern = functools.partial(_pool_kernel, tl=tl, ns=ns, padr=padr, start=start)
    row_spec = pl.BlockSpec((tl, D_MODEL), lambda n, t: (n * nt + t, 0))
    st_spec = pl.BlockSpec((1, padr, D_MODEL), lambda n, t: (n, 0, 0))
    return pl.pallas_call(
        kern,
        grid=(n_groups, nt),
        in_specs=[row_spec, st_spec, _const_spec(gn.shape), _const_spec(wp.shape),
                  _const_spec(sc.shape)],
        out_specs=[row_spec, st_spec],
        out_shape=[jax.ShapeDtypeStruct(x2d.shape, F32),
                   jax.ShapeDtypeStruct(state.shape, F32)],
        scratch_shapes=[pltpu.VMEM((padr + tl, D_MODEL), F32)],
        compiler_params=pltpu.CompilerParams(
            dimension_semantics=("arbitrary", "arbitrary"), vmem_limit_bytes=VMEM_LIMIT),
        name="pool",
    )(x2d, state, gn, wp, sc)


def _trunk_tail(x1, ns, start, st_ffn0, st_pool, st_ffn1, p, *, tl_ffn, nsplit, tl_pool):
    x2, ffn0 = _ffn(x1, st_ffn0, p, 0, tl=tl_ffn, ns=ns, final=False, nsplit=nsplit)
    if tl_pool is None:
        y, ffn1, pool = _ffn(x2, st_ffn1, p, 1, tl=tl_ffn, ns=ns, final=True, nsplit=nsplit,
                             pool_state=st_pool, start=start)
    else:
        x3, pool = _pool(x2, st_pool, p["norm_mix"][1], p["pool_w"], p["pool_scale"],
                         tl=tl_pool, ns=ns, start=start)
        y, ffn1 = _ffn(x3, st_ffn1, p, 1, tl=tl_ffn, ns=ns, final=True, nsplit=nsplit)
    return y, ffn0, pool, ffn1


def kernel(x_prompt, x_sample, state_conv_a, state_hgrn, state_pool, state_ffn, norm_mix, norm_ffn,
           norm_final, w_in, conv_a_w, hgrn_lower_bounds, hgrn_norm, w_out, pool_w, pool_scale,
           ffn_w_gu, ffn_conv_w, ffn_conv_b, ffn_w_down):
    batch, seq, _ = x_prompt.shape
    dec_batch, dec_seq, _ = x_sample.shape

    p = dict(
        norm_mix=norm_mix[:, None, :], norm_ffn=norm_ffn[:, None, :], norm_final=norm_final[None, :],
        wgu=ffn_w_gu.astype(BF16), wd=ffn_w_down.astype(BF16), ffn_cw=ffn_conv_w,
        ffn_cb=ffn_conv_b[:, None, :], pool_w=pool_w[0].astype(BF16), pool_scale=pool_scale[0][None, :])
    win = w_in[0].astype(BF16)
    wout = w_out[0].astype(BF16)
    mix_args = (p["norm_mix"][0], win, conv_a_w[0], hgrn_lower_bounds, hgrn_norm[0][None, :], wout)

    xp = x_prompt.reshape(batch * seq, D_MODEL)
    x1, conv_p, hgrn_p = _mixer0_prompt(xp, batch, seq, *mix_args, tl=512, nsplit=2)
    pad_ffn = _round_up(CONV_HIST, SUBLANES)
    pad_pool = _round_up(POOL_BUF, SUBLANES)
    zf = jnp.zeros((batch, pad_ffn, D_FF), F32)
    zp = jnp.zeros((batch, pad_pool, D_MODEL), F32)
    yp, ffn0_p, pool_p, ffn1_p = _trunk_tail(x1, 1, 0, zf, zp, zf, p, tl_ffn=512, nsplit=2,
                                             tl_pool=None)
    y_prompt = yp.reshape(batch, seq, D_MODEL)
    conv_prompt = conv_p[None, :, SUBLANES - CONV_HIST:, :]
    hgrn_prompt = hgrn_p[None]
    pool_prompt = pool_p[None, :, pad_pool - POOL_BUF:, :]
    ffn_prompt = jnp.stack([ffn0_p, ffn1_p])[:, :, pad_ffn - CONV_HIST:, :]

    xs = x_sample.reshape(dec_batch * dec_seq, D_MODEL)
    cs = state_conv_a[0]
    zeros_tail = jnp.zeros((dec_batch, dec_seq - 1, D_A), F32)
    e1 = jnp.concatenate([cs[:, 1:2], zeros_tail], axis=1).reshape(dec_batch * dec_seq, D_A)
    e2 = jnp.concatenate([cs, zeros_tail[:, 1:]], axis=1).reshape(dec_batch * dec_seq, D_A)
    x1s, us, hgrn_s = _mixer0_sample(xs, e1, e2, state_hgrn[0], *mix_args, seq_len=dec_seq, bn=16)
    conv_sample = us.reshape(dec_batch, dec_seq, D_A)[None, :, dec_seq - CONV_HIST:, :]

    def to_tm(a):
        return a.transpose(1, 0, 2).reshape(1, a.shape[1] * dec_batch, a.shape[2])

    def from_tm(a, rows):
        return a.reshape(rows, dec_batch, a.shape[-1]).transpose(1, 0, 2)

    x1t = to_tm(x1s.reshape(dec_batch, dec_seq, D_MODEL))[0]
    ys, ffn0_s, pool_s, ffn1_s = _trunk_tail(
        x1t, dec_batch, PAST_LEN, to_tm(state_ffn[0]), to_tm(state_pool[0]), to_tm(state_ffn[1]), p,
        tl_ffn=256, nsplit=1, tl_pool=dec_batch * dec_seq)
    y_sample = from_tm(ys, dec_seq)
    pool_sample = from_tm(pool_s[0], POOL_BUF)[None]
    ffn_sample = jnp.stack([from_tm(ffn0_s[0], CONV_HIST), from_tm(ffn1_s[0], CONV_HIST)])

    return (y_prompt, y_sample, conv_prompt, conv_sample, hgrn_prompt, hgrn_s[None],
            pool_prompt, pool_sample, ffn_prompt, ffn_sample)
```

```python
import functools

import jax
import jax.numpy as jnp
import numpy as np
from jax import lax
from jax.experimental import pallas as pl
from jax.experimental.pallas import tpu as pltpu

D_MODEL = 1024
D_A = 512
N_HEADS = 4
HEAD = 128
D_B = N_HEADS * HEAD
D_IN = 3 * D_A + 4 * D_B
D_FF = 2816
POOL_WINDOWS = (2, 4, 8, 16)
POOL_GROUP = D_MODEL // len(POOL_WINDOWS)
POOL_BUF = max(POOL_WINDOWS) - 1
CONV_HIST = 2
PAST_LEN = 16384
EPS = 1e-6
LOG2E = 1.4426950408889634

SUBLANES = 8
HGRN_CHUNK = 64
VMEM_LIMIT = 56 * 1024 * 1024

F32 = jnp.float32
BF16 = jnp.bfloat16

_OFF_AC, _OFF_AB, _OFF_AV = 0, D_A, 2 * D_A
_OFF_Q, _OFF_F, _OFF_I, _OFF_G = (3 * D_A, 3 * D_A + D_B, 3 * D_A + 2 * D_B, 3 * D_A + 3 * D_B)
_HEADS = [slice(h * HEAD, (h + 1) * HEAD) for h in range(N_HEADS)]


def _round_up(x, m):
    return (x + m - 1) // m * m


def _log2(x):
    assert x & (x - 1) == 0
    return x.bit_length() - 1


def _rms(x, g):
    return x * lax.rsqrt(jnp.mean(x * x, axis=-1, keepdims=True) + EPS) * g


def _silu(x):
    return x * jax.nn.sigmoid(x)


def _dot(a, b):
    return jnp.dot(a, b, preferred_element_type=F32)


def _dot_nt(a, b):
    return lax.dot_general(a, b, (((1,), (1,)), ((), ())), preferred_element_type=F32)


def _dot_tn(a, b):
    return lax.dot_general(a, b, (((0,), (0,)), ((), ())), preferred_element_type=F32)


def _shifted(ext_ref, padr, n, k_rows, cols=slice(None), row0=0):
    off = padr + row0 - k_rows
    assert off >= 0
    base = off // SUBLANES * SUBLANES
    rem = off - base
    if rem == 0:
        return ext_ref[pl.ds(off, n), cols]
    win = ext_ref[pl.ds(base, n + SUBLANES), cols]
    return pltpu.roll(win, SUBLANES - rem, 0)[SUBLANES:SUBLANES + n, :]


def _levels(seg):
    return [1 << i for i in range(_log2(seg))]


def _selector_levels(seg):
    return [b for b in _levels(seg) if 1 < b < SUBLANES]


def _selector(c, seg):
    t = np.arange(c)[:, None]
    u = np.arange(c)[None, :]
    same_seg = (t // seg) == (u // seg)
    blocks = [same_seg & (u <= t)]
    blocks.append(same_seg & (u > t) if seg < c else np.ones((SUBLANES, c), bool))
    for b in _selector_levels(seg):
        pair = (t // (2 * b)) == (u // (2 * b))
        anchor = (t // (2 * b)) * 2 * b + b - 1
        right = (t // b) % 2 == 1
        blocks.append(pair & np.where(right, (u > anchor) & (u <= t), (u > t) & (u <= anchor)))
    sel = np.concatenate(blocks, axis=0).astype(np.float32)
    return np.concatenate([sel, sel], axis=1)


def _split_bf16(a):
    hi = a.astype(BF16)
    return hi, (a - hi.astype(F32)).astype(BF16)


def _hgrn_intra(q, kk, f, lf2, sel_ref, g_ref, seg):
    c = q.shape[0]
    x = _dot(sel_ref[...], jnp.concatenate(_split_bf16(lf2), axis=0))
    g = x[0:c]

    row = lax.broadcasted_iota(jnp.int32, (c, 1), 0)
    rowi = lax.broadcasted_iota(jnp.int32, (c, c), 0)
    coli = lax.broadcasted_iota(jnp.int32, (c, c), 1)
    diff = rowi ^ coli
    lower = rowi > coli

    p0 = q * kk
    p1 = q * pltpu.roll(kk, 1, 0) * f
    amat = []
    for sl in _HEADS:
        a0 = jnp.sum(p0[:, sl], axis=-1, keepdims=True)
        a1 = jnp.sum(p1[:, sl], axis=-1, keepdims=True)
        amat.append(jnp.where(rowi == coli, a0, jnp.where(lower & (diff == 1), a1, 0.0)))

    sel_blocks = {b: x[off:off + c] for b, off in
                  zip(_selector_levels(seg), range(c + (c if seg < c else SUBLANES), x.shape[0], c))}
    if seg > SUBLANES:
        g_ref[...] = g
    for b in _levels(seg)[1:]:
        if b in sel_blocks:
            ex = sel_blocks[b]
        else:
            parts = []
            for pair in range(c // (2 * b)):
                anchor = g_ref[pl.ds(pair * 2 * b + b - 1, 1), :]
                parts.append(g[pair * 2 * b:(pair + 1) * 2 * b, :] - anchor)
            ex = -jnp.abs(parts[0] if len(parts) == 1 else jnp.concatenate(parts, axis=0))
        w = jnp.exp2(ex)
        right = ((row >> _log2(b)) & 1) == 1
        ql = jnp.where(right, q * w, 0.0).astype(BF16)
        kl = jnp.where(right, 0.0, kk * w).astype(BF16)
        mask = lower & ((diff >> _log2(b)) == 1)
        for h, sl in enumerate(_HEADS):
            amat[h] = jnp.where(mask, _dot_nt(ql[:, sl], kl[:, sl]), amat[h])
    return amat, q * jnp.exp2(g), x


def _gates(z_ref, rows, lb):
    f = lb + (1.0 - lb) * jax.nn.sigmoid(z_ref[rows, _OFF_F:_OFF_F + D_B])
    return f, 1.0 - f, jnp.log(f) * LOG2E


def _lower_bound(lbp_ref):
    lbp = lbp_ref[...]
    lbe = jnp.exp(lbp - jnp.max(lbp, axis=0, keepdims=True))
    return lbe[0:1, :] / jnp.sum(lbe, axis=0, keepdims=True)


def _mixer_b_out(z_ref, o_ref, y_ref, hn, rows):
    for h, sl in enumerate(_HEADS):
        on = _rms(o_ref[rows, sl], hn[:, sl])
        gate = z_ref[rows, _OFF_G + h * HEAD:_OFF_G + (h + 1) * HEAD]
        y_ref[rows, D_A + h * HEAD:D_A + (h + 1) * HEAD] = (on * _silu(gate)).astype(BF16)


def _mixer0_prompt_kernel(x_ref, gn_ref, win_ref, cw_ref, lbp_ref, hn_ref, wout_ref, sel_ref,
                          xo_ref, convo_ref, so_ref,
                          hb_ref, z_ref, o_ref, y_ref, ext_ref, st_ref, g_ref, *, tl, chunk, nsplit):
    t = pl.program_id(1)

    @pl.when(t == 0)
    def _():
        ext_ref[0:SUBLANES, :] = jnp.zeros((SUBLANES, D_A), F32)
        st_ref[...] = jnp.zeros((N_HEADS, HEAD, HEAD), F32)

    hb_ref[...] = _rms(x_ref[...], gn_ref[...]).astype(BF16)
    lb = _lower_bound(lbp_ref)
    cw = cw_ref[...]
    hn = hn_ref[...]
    part = tl // nsplit

    for s in range(nsplit):
        r0 = s * part
        rows = slice(r0, r0 + part)
        hb = hb_ref[rows, :]
        z_ref[rows, _OFF_Q:_OFF_G] = _dot(hb, win_ref[:, _OFF_Q:_OFF_G])

        for c0 in range(r0, r0 + part, chunk):
            crows = slice(c0, c0 + chunk)
            f, kk, lf2 = _gates(z_ref, crows, lb)
            amat, qg, x = _hgrn_intra(z_ref[crows, _OFF_Q:_OFF_Q + D_B], kk, f, lf2, sel_ref,
                                      g_ref.at[c0 // chunk], chunk)
            g_last = x[chunk:chunk + 1]
            dec = jnp.exp2(g_last)
            kg = (kk * jnp.exp2(g_last - x[0:chunk])).astype(BF16)
            qg = qg.astype(BF16)
            vb = z_ref[crows, _OFF_I:_OFF_I + D_B].astype(BF16)
            for h, sl in enumerate(_HEADS):
                st = st_ref[h]
                lhs = jnp.concatenate([qg[:, sl], amat[h].astype(BF16)], axis=1)
                rhs = jnp.concatenate([st.T.astype(BF16), vb[:, sl]], axis=0)
                o_ref[crows, sl] = _dot(lhs, rhs)
                st_ref[h] = dec[:, sl] * st + _dot_tn(vb[:, sl], kg[:, sl])

        z_ref[rows, 0:_OFF_Q] = _dot(hb, win_ref[:, 0:_OFF_Q])
        z_ref[rows, _OFF_G:D_IN] = _dot(hb, win_ref[:, _OFF_G:D_IN])

        u = z_ref[rows, _OFF_AC:_OFF_AC + D_A] * z_ref[rows, _OFF_AV:_OFF_AV + D_A]
        ext_ref[SUBLANES + r0:SUBLANES + r0 + part, :] = u
        conv = (_shifted(ext_ref, SUBLANES, part, 2, row0=r0) * cw[0:1, :]
                + _shifted(ext_ref, SUBLANES, part, 1, row0=r0) * cw[1:2, :] + u * cw[2:3, :])
        y_ref[rows, 0:D_A] = (z_ref[rows, _OFF_AB:_OFF_AB + D_A] * conv).astype(BF16)
        _mixer_b_out(z_ref, o_ref, y_ref, hn, rows)
        xo_ref[rows, :] = x_ref[rows, :] + _dot(y_ref[rows, :], wout_ref[...])

    carry = ext_ref[tl:tl + SUBLANES, :]
    ext_ref[0:SUBLANES, :] = carry

    @pl.when(t == pl.num_programs(1) - 1)
    def _():
        convo_ref[0] = carry
        for h in range(N_HEADS):
            so_ref[0, h] = st_ref[h].T


def _mixer0_sample_kernel(x_ref, e1_ref, e2_ref, s0_ref, gn_ref, win_ref, cw_ref, lbp_ref, hn_ref,
                          wout_ref, sel_ref,
                          xo_ref, uo_ref, so_ref,
                          z_ref, o_ref, y_ref, lf_ref, qg_ref, kg_ref, *, tl, seq):
    x = x_ref[...]
    z_ref[...] = _dot(_rms(x, gn_ref[...]).astype(BF16), win_ref[...])

    u = z_ref[:, _OFF_AC:_OFF_AC + D_A] * z_ref[:, _OFF_AV:_OFF_AV + D_A]
    cw = cw_ref[...]
    tpos = lax.broadcasted_iota(jnp.int32, (tl, 1), 0) & (seq - 1)
    u1 = jnp.where(tpos >= 1, pltpu.roll(u, 1, 0), e1_ref[...])
    u2 = jnp.where(tpos >= 2, pltpu.roll(u, 2, 0), e2_ref[...])
    uo_ref[...] = u
    conv = u2 * cw[0:1, :] + u1 * cw[1:2, :] + u * cw[2:3, :]
    y_ref[:, 0:D_A] = (z_ref[:, _OFF_AB:_OFF_AB + D_A] * conv).astype(BF16)

    f, kk, lf2 = _gates(z_ref, slice(None), _lower_bound(lbp_ref))
    amat, qg, xsel = _hgrn_intra(z_ref[:, _OFF_Q:_OFF_Q + D_B], kk, f, lf2, sel_ref, None, seq)
    v_all = z_ref[:, _OFF_I:_OFF_I + D_B].astype(BF16)
    o_ref[...] = jnp.concatenate(
        [_dot(amat[h].astype(BF16), v_all[:, sl]) for h, sl in enumerate(_HEADS)], axis=1)
    qg_ref[...] = qg
    kg_ref[...] = kk * jnp.exp2(xsel[tl:2 * tl])
    lf_ref[...] = lf2
    ones = jnp.ones((seq, HEAD), BF16)

    def seq_body(i, carry_unused):
        rows = pl.ds(pl.multiple_of(i * seq, seq), seq)
        hi, mid = _split_bf16(lf_ref[rows, :])
        qg_i = qg_ref[rows, :].astype(BF16)
        kg_i = kg_ref[rows, :].astype(BF16)
        vb = z_ref[rows, _OFF_I:_OFF_I + D_B].astype(BF16)
        for h, sl in enumerate(_HEADS):
            s0 = s0_ref[i, h]
            o_ref[rows, sl] = o_ref[rows, sl] + _dot(qg_i[:, sl], s0.astype(BF16))
            g_last = _dot_tn(hi[:, sl], ones) + _dot_tn(mid[:, sl], ones)
            so_ref[i, h] = jnp.exp2(g_last) * s0 + _dot_tn(kg_i[:, sl], vb[:, sl])
        return carry_unused

    lax.fori_loop(0, tl // seq, seq_body, 0, unroll=4)

    _mixer_b_out(z_ref, o_ref, y_ref, hn_ref[...], slice(None))
    xo_ref[...] = x + _dot(y_ref[...], wout_ref[...])


def _const_spec(shape):
    nd = len(shape)
    return pl.BlockSpec(shape, lambda *_: (0,) * nd, pipeline_mode=pl.Buffered(1))


def _layer_spec(shape, layer):
    nd = len(shape) - 1
    return pl.BlockSpec((None,) + tuple(shape[1:]), lambda *_: (layer,) + (0,) * nd,
                        pipeline_mode=pl.Buffered(1))


def _mixer0_prompt(x2d, n_seq, seq_len, gn, win, cw, lbp, hn, wout, *, tl, nsplit):
    nt = seq_len // tl
    sel = jnp.asarray(_selector(HGRN_CHUNK, HGRN_CHUNK), BF16)
    kern = functools.partial(_mixer0_prompt_kernel, tl=tl, chunk=HGRN_CHUNK, nsplit=nsplit)
    row_spec = pl.BlockSpec((tl, D_MODEL), lambda n, t: (n * nt + t, 0))
    consts = (gn, win, cw, lbp, hn, wout, sel)
    return pl.pallas_call(
        kern,
        grid=(n_seq, nt),
        in_specs=[row_spec] + [_const_spec(a.shape) for a in consts],
        out_specs=[row_spec,
                   pl.BlockSpec((1, SUBLANES, D_A), lambda n, t: (n, 0, 0)),
                   pl.BlockSpec((1, N_HEADS, HEAD, HEAD), lambda n, t: (n, 0, 0, 0))],
        out_shape=[jax.ShapeDtypeStruct(x2d.shape, F32),
                   jax.ShapeDtypeStruct((n_seq, SUBLANES, D_A), F32),
                   jax.ShapeDtypeStruct((n_seq, N_HEADS, HEAD, HEAD), F32)],
        scratch_shapes=[pltpu.VMEM((tl, D_MODEL), BF16),
                        pltpu.VMEM((tl, D_IN), F32),
                        pltpu.VMEM((tl, D_B), F32),
                        pltpu.VMEM((tl, D_A + D_B), BF16),
                        pltpu.VMEM((SUBLANES + tl, D_A), F32),
                        pltpu.VMEM((N_HEADS, HEAD, HEAD), F32),
                        pltpu.VMEM((tl // HGRN_CHUNK, HGRN_CHUNK, D_B), F32)],
        compiler_params=pltpu.CompilerParams(
            dimension_semantics=("arbitrary", "arbitrary"), vmem_limit_bytes=VMEM_LIMIT),
        name="mixer0_prompt",
    )(x2d, *consts)


def _mixer0_sample(x2d, e1, e2, s0, gn, win, cw, lbp, hn, wout, *, seq_len, bn):
    n_seq = s0.shape[0]
    tl = bn * seq_len
    sel = jnp.asarray(_selector(tl, seq_len), BF16)
    kern = functools.partial(_mixer0_sample_kernel, tl=tl, seq=seq_len)
    row_spec = pl.BlockSpec((tl, D_MODEL), lambda i: (i, 0))
    conv_spec = pl.BlockSpec((tl, D_A), lambda i: (i, 0))
    st_spec = pl.BlockSpec((bn, N_HEADS, HEAD, HEAD), lambda i: (i, 0, 0, 0))
    consts = (gn, win, cw, lbp, hn, wout, sel)
    return pl.pallas_call(
        kern,
        grid=(n_seq // bn,),
        in_specs=[row_spec, conv_spec, conv_spec, st_spec] + [_const_spec(a.shape) for a in consts],
        out_specs=[row_spec, conv_spec, st_spec],
        out_shape=[jax.ShapeDtypeStruct(x2d.shape, F32),
                   jax.ShapeDtypeStruct((x2d.shape[0], D_A), F32),
                   jax.ShapeDtypeStruct(s0.shape, F32)],
        scratch_shapes=[pltpu.VMEM((tl, D_IN), F32),
                        pltpu.VMEM((tl, D_B), F32),
                        pltpu.VMEM((tl, D_A + D_B), BF16),
                        pltpu.VMEM((tl, D_B), F32),
                        pltpu.VMEM((tl, D_B), F32),
                        pltpu.VMEM((tl, D_B), F32)],
        compiler_params=pltpu.CompilerParams(
            dimension_semantics=("arbitrary",), vmem_limit_bytes=VMEM_LIMIT),
        name="mixer0_sample",
    )(x2d, e1, e2, s0, *consts)


def _pool_mix(x, gm, ext_ref, padr, row0, ns, pos0, wp_ref, sc):
    n = x.shape[0]
    h = _rms(x, gm)
    ext_ref[padr + row0:padr + row0 + n, :] = h
    pos = pos0 + (lax.broadcasted_iota(jnp.int32, (n, 1), 0) >> _log2(ns))
    ys = []
    for gi, w in enumerate(POOL_WINDOWS):
        cols = slice(gi * POOL_GROUP, (gi + 1) * POOL_GROUP)
        hg = h[:, cols]
        if ns == 1:
            hist = _round_up(POOL_BUF, SUBLANES)
            s = ext_ref[pl.ds(padr + row0 - hist, n + hist), cols]
            sh = 1
            while sh < w:
                s = s + pltpu.roll(s, sh, 0)
                sh *= 2
            acc = s[hist:hist + n, :]
        else:
            acc = hg
            for k in range(1, w):
                acc = acc + _shifted(ext_ref, padr, n, k * ns, cols, row0)
        cnt = jnp.minimum(w, pos + 1).astype(F32)
        ys.append(_dot((acc / cnt - hg).astype(BF16), wp_ref[gi]))
    return x + jnp.concatenate(ys, axis=1) * sc


def _ffn_kernel(*refs, tl, ns, padr, final, pool, nsplit):
    if pool is None:
        (x_ref, st_ref, gn_ref, wgu_ref, cw_ref, cb_ref, wd_ref, gf_ref,
         xo_ref, sto_ref, ext_ref) = refs
    else:
        (x_ref, st_ref, pst_ref, gm_ref, wp_ref, sc_ref, gn_ref, wgu_ref, cw_ref, cb_ref, wd_ref, gf_ref,
         xo_ref, sto_ref, psto_ref, ext_ref, pext_ref) = refs
        ppad, start = pool
    t = pl.program_id(1)

    @pl.when(t == 0)
    def _():
        ext_ref[0:padr, :] = st_ref[0]
        if pool is not None:
            pext_ref[0:ppad, :] = pst_ref[0]

    cw = cw_ref[...]
    cb = cb_ref[...]
    part = tl // nsplit
    for s in range(nsplit):
        r0 = s * part
        rows = slice(r0, r0 + part)
        x = x_ref[rows, :]
        if pool is not None:
            pos0 = start + ((t * tl + r0) >> _log2(ns))
            x = _pool_mix(x, gm_ref[...], pext_ref, ppad, r0, ns, pos0, wp_ref, sc_ref[...])
        hb = _rms(x, gn_ref[...]).astype(BF16)
        g = _dot(hb, wgu_ref[:, 0:D_FF])
        ext_ref[padr + r0:padr + r0 + part, :] = g
        gc = (_shifted(ext_ref, padr, part, 2 * ns, row0=r0) * cw[0:1, :]
              + _shifted(ext_ref, padr, part, ns, row0=r0) * cw[1:2, :]
              + g * cw[2:3, :] + cb)
        up = _dot(hb, wgu_ref[:, D_FF:2 * D_FF])
        act = (_silu(gc) * up).astype(BF16)
        xo = x + _dot(act, wd_ref[...])
        if final:
            xo = _rms(xo, gf_ref[...])
        xo_ref[rows, :] = xo

    carry = ext_ref[tl:tl + padr, :]
    ext_ref[0:padr, :] = carry
    if pool is not None:
        pcarry = pext_ref[tl:tl + ppad, :]
        pext_ref[0:ppad, :] = pcarry

    @pl.when(t == pl.num_programs(1) - 1)
    def _():
        sto_ref[0] = carry
        if pool is not None:
            psto_ref[0] = pcarry


def _ffn(x2d, state, p, layer, *, tl, ns, final, nsplit=1, pool_state=None, start=0):
    n_groups, padr, _ = state.shape
    rows_per_group = x2d.shape[0] // n_groups
    nt = rows_per_group // tl
    assert padr % SUBLANES == 0 and padr >= CONV_HIST * ns and (nt == 1 or tl >= padr)
    row_spec = pl.BlockSpec((tl, D_MODEL), lambda n, t: (n * nt + t, 0))
    st_spec = pl.BlockSpec((1, padr, D_FF), lambda n, t: (n, 0, 0))
    stacked = (p["norm_ffn"], p["wgu"], p["ffn_cw"], p["ffn_cb"], p["wd"])
    ins = [x2d, state]
    in_specs = [row_spec, st_spec]
    out_specs = [row_spec, st_spec]
    out_shape = [jax.ShapeDtypeStruct(x2d.shape, F32), jax.ShapeDtypeStruct(state.shape, F32)]
    scratch = [pltpu.VMEM((padr + tl, D_FF), F32)]
    pool = None
    if pool_state is not None:
        ppad = pool_state.shape[1]
        assert ppad % SUBLANES == 0 and ppad >= POOL_BUF * ns and (nt == 1 or tl >= ppad)
        pool = (ppad, start)
        pst_spec = pl.BlockSpec((1, ppad, D_MODEL), lambda n, t: (n, 0, 0))
        pool_consts = (p["norm_mix"][layer], p["pool_w"], p["pool_scale"])
        ins += [pool_state, *pool_consts]
        in_specs += [pst_spec] + [_const_spec(a.shape) for a in pool_consts]
        out_specs.append(pst_spec)
        out_shape.append(jax.ShapeDtypeStruct(pool_state.shape, F32))
        scratch.append(pltpu.VMEM((ppad + tl, D_MODEL), F32))
    kern = functools.partial(_ffn_kernel, tl=tl, ns=ns, padr=padr, final=final, pool=pool,
                             nsplit=nsplit)
    return pl.pallas_call(
        kern,
        grid=(n_groups, nt),
        in_specs=in_specs + [_layer_spec(a.shape, layer) for a in stacked]
                 + [_const_spec(p["norm_final"].shape)],
        out_specs=out_specs,
        out_shape=out_shape,
        scratch_shapes=scratch,
        compiler_params=pltpu.CompilerParams(
            dimension_semantics=("arbitrary", "arbitrary"), vmem_limit_bytes=VMEM_LIMIT),
        name=("pool_" if pool else "") + ("ffn_final" if final else "ffn"),
    )(*ins, *stacked, p["norm_final"])


def _pool_kernel(x_ref, st_ref, gn_ref, wp_ref, sc_ref, xo_ref, sto_ref, ext_ref,
                 *, tl, ns, padr, start):
    t = pl.program_id(1)

    @pl.when(t == 0)
    def _():
        ext_ref[0:padr, :] = st_ref[0]

    pos0 = start + ((t * tl) >> _log2(ns))
    xo_ref[...] = _pool_mix(x_ref[...], gn_ref[...], ext_ref, padr, 0, ns, pos0, wp_ref, sc_ref[...])

    carry = ext_ref[tl:tl + padr, :]
    ext_ref[0:padr, :] = carry

    @pl.when(t == pl.num_programs(1) - 1)
    def _():
        sto_ref[0] = carry


def _pool(x2d, state, gn, wp, sc, *, tl, ns, start):
    n_groups, padr, _ = state.shape
    rows_per_group = x2d.shape[0] // n_groups
    nt = rows_per_group // tl
    assert padr % SUBLANES == 0 and padr >= POOL_BUF * ns and (nt == 1 or tl >= padr)
    kern = functools.partial(_pool_kernel, tl=tl, ns=ns, padr=padr, start=start)
    row_spec = pl.BlockSpec((tl, D_MODEL), lambda n, t: (n * nt + t, 0))
    st_spec = pl.BlockSpec((1, padr, D_MODEL), lambda n, t: (n, 0, 0))
    return pl.pallas_call(
        kern,
        grid=(n_groups, nt),
        in_specs=[row_spec, st_spec, _const_spec(gn.shape), _const_spec(wp.shape),
                  _const_spec(sc.shape)],
        out_specs=[row_spec, st_spec],
        out_shape=[jax.ShapeDtypeStruct(x2d.shape, F32),
                   jax.ShapeDtypeStruct(state.shape, F32)],
        scratch_shapes=[pltpu.VMEM((padr + tl, D_MODEL), F32)],
        compiler_params=pltpu.CompilerParams(
            dimension_semantics=("arbitrary", "arbitrary"), vmem_limit_bytes=VMEM_LIMIT),
        name="pool",
    )(x2d, state, gn, wp, sc)


def _trunk_tail(x1, ns, start, st_ffn0, st_pool, st_ffn1, p, *, tl_ffn, nsplit, tl_pool):
    x2, ffn0 = _ffn(x1, st_ffn0, p, 0, tl=tl_ffn, ns=ns, final=False, nsplit=nsplit)
    if tl_pool is None:
        y, ffn1, pool = _ffn(x2, st_ffn1, p, 1, tl=tl_ffn, ns=ns, final=True, nsplit=nsplit,
                             pool_state=st_pool, start=start)
    else:
        x3, pool = _pool(x2, st_pool, p["norm_mix"][1], p["pool_w"], p["pool_scale"],
                         tl=tl_pool, ns=ns, start=start)
        y, ffn1 = _ffn(x3, st_ffn1, p, 1, tl=tl_ffn, ns=ns, final=True, nsplit=nsplit)
    return y, ffn0, pool, ffn1


def kernel(x_prompt, x_sample, state_conv_a, state_hgrn, state_pool, state_ffn, norm_mix, norm_ffn,
           norm_final, w_in, conv_a_w, hgrn_lower_bounds, hgrn_norm, w_out, pool_w, pool_scale,
           ffn_w_gu, ffn_conv_w, ffn_conv_b, ffn_w_down):
    batch, seq, _ = x_prompt.shape
    dec_batch, dec_seq, _ = x_sample.shape

    p = dict(
        norm_mix=norm_mix[:, None, :], norm_ffn=norm_ffn[:, None, :], norm_final=norm_final[None, :],
        wgu=ffn_w_gu.astype(BF16), wd=ffn_w_down.astype(BF16), ffn_cw=ffn_conv_w,
        ffn_cb=ffn_conv_b[:, None, :], pool_w=pool_w[0].astype(BF16), pool_scale=pool_scale[0][None, :])
    win = w_in[0].astype(BF16)
    wout = w_out[0].astype(BF16)
    mix_args = (p["norm_mix"][0], win, conv_a_w[0], hgrn_lower_bounds, hgrn_norm[0][None, :], wout)

    xp = x_prompt.reshape(batch * seq, D_MODEL)
    x1, conv_p, hgrn_p = _mixer0_prompt(xp, batch, seq, *mix_args, tl=512, nsplit=2)
    pad_ffn = _round_up(CONV_HIST, SUBLANES)
    pad_pool = _round_up(POOL_BUF, SUBLANES)
    zf = jnp.zeros((batch, pad_ffn, D_FF), F32)
    zp = jnp.zeros((batch, pad_pool, D_MODEL), F32)
    yp, ffn0_p, pool_p, ffn1_p = _trunk_tail(x1, 1, 0, zf, zp, zf, p, tl_ffn=512, nsplit=2,
                                             tl_pool=None)
    y_prompt = yp.reshape(batch, seq, D_MODEL)
    conv_prompt = conv_p[None, :, SUBLANES - CONV_HIST:, :]
    hgrn_prompt = hgrn_p[None]
    pool_prompt = pool_p[None, :, pad_pool - POOL_BUF:, :]
    ffn_prompt = jnp.stack([ffn0_p, ffn1_p])[:, :, pad_ffn - CONV_HIST:, :]

    xs = x_sample.reshape(dec_batch * dec_seq, D_MODEL)
    cs = state_conv_a[0]
    zeros_tail = jnp.zeros((dec_batch, dec_seq - 1, D_A), F32)
    e1 = jnp.concatenate([cs[:, 1:2], zeros_tail], axis=1).reshape(dec_batch * dec_seq, D_A)
    e2 = jnp.concatenate([cs, zeros_tail[:, 1:]], axis=1).reshape(dec_batch * dec_seq, D_A)
    x1s, us, hgrn_s = _mixer0_sample(xs, e1, e2, state_hgrn[0], *mix_args, seq_len=dec_seq, bn=16)
    conv_sample = us.reshape(dec_batch, dec_seq, D_A)[None, :, dec_seq - CONV_HIST:, :]

    def to_tm(a):
        return a.transpose(1, 0, 2).reshape(1, a.shape[1] * dec_batch, a.shape[2])

    def from_tm(a, rows):
        return a.reshape(rows, dec_batch, a.shape[-1]).transpose(1, 0, 2)

    x1t = to_tm(x1s.reshape(dec_batch, dec_seq, D_MODEL))[0]
    ys, ffn0_s, pool_s, ffn1_s = _trunk_tail(
        x1t, dec_batch, PAST_LEN, to_tm(state_ffn[0]), to_tm(state_pool[0]), to_tm(state_ffn[1]), p,
        tl_ffn=512, nsplit=2, tl_pool=dec_batch * dec_seq)
    y_sample = from_tm(ys, dec_seq)
    pool_sample = from_tm(pool_s[0], POOL_BUF)[None]
    ffn_sample = jnp.stack([from_tm(ffn0_s[0], CONV_HIST), from_tm(ffn1_s[0], CONV_HIST)])

    return (y_prompt, y_sample, conv_prompt, conv_sample, hgrn_prompt, hgrn_s[None],
            pool_prompt, pool_sample, ffn_prompt, ffn_sample)
```

```python
import functools

import jax
import jax.numpy as jnp
import numpy as np
from jax import lax
from jax.experimental import pallas as pl
from jax.experimental.pallas import tpu as pltpu

D_MODEL = 1024
D_A = 512
N_HEADS = 4
HEAD = 128
D_B = N_HEADS * HEAD
D_IN = 3 * D_A + 4 * D_B
D_FF = 2816
POOL_WINDOWS = (2, 4, 8, 16)
POOL_GROUP = D_MODEL // len(POOL_WINDOWS)
POOL_BUF = max(POOL_WINDOWS) - 1
CONV_HIST = 2
PAST_LEN = 16384
EPS = 1e-6
LOG2E = 1.4426950408889634

SUBLANES = 8
HGRN_CHUNK = 64
VMEM_LIMIT = 56 * 1024 * 1024
CAST_STEPS = 16

F32 = jnp.float32
BF16 = jnp.bfloat16

_OFF_AC, _OFF_AB, _OFF_AV = 0, D_A, 2 * D_A
_OFF_Q, _OFF_F, _OFF_I, _OFF_G = (3 * D_A, 3 * D_A + D_B, 3 * D_A + 2 * D_B, 3 * D_A + 3 * D_B)
_HEADS = [slice(h * HEAD, (h + 1) * HEAD) for h in range(N_HEADS)]


def _round_up(x, m):
    return (x + m - 1) // m * m


def _log2(x):
    assert x & (x - 1) == 0
    return x.bit_length() - 1


def _rms(x, g):
    return x * lax.rsqrt(jnp.mean(x * x, axis=-1, keepdims=True) + EPS) * g


def _silu(x):
    return x * jax.nn.sigmoid(x)


def _dot(a, b):
    return jnp.dot(a, b, preferred_element_type=F32)


def _dot_nt(a, b):
    return lax.dot_general(a, b, (((1,), (1,)), ((), ())), preferred_element_type=F32)


def _dot_tn(a, b):
    return lax.dot_general(a, b, (((0,), (0,)), ((), ())), preferred_element_type=F32)


def _shifted(ext_ref, padr, n, k_rows, cols=slice(None), row0=0):
    off = padr + row0 - k_rows
    assert off >= 0
    base = off // SUBLANES * SUBLANES
    rem = off - base
    if rem == 0:
        return ext_ref[pl.ds(off, n), cols]
    win = ext_ref[pl.ds(base, n + SUBLANES), cols]
    return pltpu.roll(win, SUBLANES - rem, 0)[SUBLANES:SUBLANES + n, :]


def _levels(seg):
    return [1 << i for i in range(_log2(seg))]


def _selector_levels(seg):
    return [b for b in _levels(seg) if 1 < b < SUBLANES]


def _selector(c, seg):
    t = np.arange(c)[:, None]
    u = np.arange(c)[None, :]
    same_seg = (t // seg) == (u // seg)
    blocks = [same_seg & (u <= t)]
    blocks.append(same_seg & (u > t) if seg < c else np.ones((SUBLANES, c), bool))
    for b in _selector_levels(seg):
        pair = (t // (2 * b)) == (u // (2 * b))
        anchor = (t // (2 * b)) * 2 * b + b - 1
        right = (t // b) % 2 == 1
        blocks.append(pair & np.where(right, (u > anchor) & (u <= t), (u > t) & (u <= anchor)))
    sel = np.concatenate(blocks, axis=0).astype(np.float32)
    return np.concatenate([sel, sel], axis=1)


def _split_bf16(a):
    hi = a.astype(BF16)
    return hi, (a - hi.astype(F32)).astype(BF16)


def _hgrn_intra(q, kk, f, lf2, sel_ref, g_ref, seg):
    c = q.shape[0]
    x = _dot(sel_ref[...], jnp.concatenate(_split_bf16(lf2), axis=0))
    g = x[0:c]

    row = lax.broadcasted_iota(jnp.int32, (c, 1), 0)
    rowi = lax.broadcasted_iota(jnp.int32, (c, c), 0)
    coli = lax.broadcasted_iota(jnp.int32, (c, c), 1)
    diff = rowi ^ coli
    lower = rowi > coli

    p0 = q * kk
    p1 = q * pltpu.roll(kk, 1, 0) * f
    amat = []
    for sl in _HEADS:
        a0 = jnp.sum(p0[:, sl], axis=-1, keepdims=True)
        a1 = jnp.sum(p1[:, sl], axis=-1, keepdims=True)
        amat.append(jnp.where(rowi == coli, a0, jnp.where(lower & (diff == 1), a1, 0.0)))

    sel_blocks = {b: x[off:off + c] for b, off in
                  zip(_selector_levels(seg), range(c + (c if seg < c else SUBLANES), x.shape[0], c))}
    if seg > SUBLANES:
        g_ref[...] = g
    for b in _levels(seg)[1:]:
        if b in sel_blocks:
            ex = sel_blocks[b]
        else:
            parts = []
            for pair in range(c // (2 * b)):
                anchor = g_ref[pl.ds(pair * 2 * b + b - 1, 1), :]
                parts.append(g[pair * 2 * b:(pair + 1) * 2 * b, :] - anchor)
            ex = -jnp.abs(parts[0] if len(parts) == 1 else jnp.concatenate(parts, axis=0))
        w = jnp.exp2(ex)
        right = ((row >> _log2(b)) & 1) == 1
        ql = jnp.where(right, q * w, 0.0).astype(BF16)
        kl = jnp.where(right, 0.0, kk * w).astype(BF16)
        mask = lower & ((diff >> _log2(b)) == 1)
        for h, sl in enumerate(_HEADS):
            amat[h] = jnp.where(mask, _dot_nt(ql[:, sl], kl[:, sl]), amat[h])
    return amat, q * jnp.exp2(g), x


def _gates(z_ref, rows, lb):
    f = lb + (1.0 - lb) * jax.nn.sigmoid(z_ref[rows, _OFF_F:_OFF_F + D_B])
    return f, 1.0 - f, jnp.log(f) * LOG2E


def _lower_bound(lbp_ref):
    lbp = lbp_ref[...]
    lbe = jnp.exp(lbp - jnp.max(lbp, axis=0, keepdims=True))
    return lbe[0:1, :] / jnp.sum(lbe, axis=0, keepdims=True)


def _mixer_b_out(z_ref, o_ref, y_ref, hn, rows):
    for h, sl in enumerate(_HEADS):
        on = _rms(o_ref[rows, sl], hn[:, sl])
        gate = z_ref[rows, _OFF_G + h * HEAD:_OFF_G + (h + 1) * HEAD]
        y_ref[rows, D_A + h * HEAD:D_A + (h + 1) * HEAD] = (on * _silu(gate)).astype(BF16)


def _cast_rider(w, layer, nt):
    _, r, c = w.shape
    rows = r // CAST_STEPS
    assert rows * CAST_STEPS == r and rows % (2 * SUBLANES) == 0

    def blk(n, t):
        return jnp.minimum(n * nt + t, CAST_STEPS - 1)

    return (pl.BlockSpec((None, rows, c), lambda n, t: (layer, blk(n, t), 0)),
            pl.BlockSpec((rows, c), lambda n, t: (blk(n, t), 0)),
            jax.ShapeDtypeStruct((r, c), BF16))


def _run_cast_riders(nt, pairs):
    @pl.when(pl.program_id(0) * nt + pl.program_id(1) < CAST_STEPS)
    def _():
        for src, dst in pairs:
            dst[...] = src[...].astype(BF16)


def _mixer0_prompt_kernel(x_ref, gn_ref, win_ref, cw_ref, lbp_ref, hn_ref, wout_ref, sel_ref,
                          c0_ref, c1_ref,
                          xo_ref, convo_ref, so_ref, c0o_ref, c1o_ref,
                          hb_ref, z_ref, o_ref, y_ref, ext_ref, st_ref, g_ref, *, tl, chunk, nsplit):
    t = pl.program_id(1)
    _run_cast_riders(pl.num_programs(1), [(c0_ref, c0o_ref), (c1_ref, c1o_ref)])

    @pl.when(t == 0)
    def _():
        ext_ref[0:SUBLANES, :] = jnp.zeros((SUBLANES, D_A), F32)
        st_ref[...] = jnp.zeros((N_HEADS, HEAD, HEAD), F32)

    hb_ref[...] = _rms(x_ref[...], gn_ref[...]).astype(BF16)
    lb = _lower_bound(lbp_ref)
    cw = cw_ref[...]
    hn = hn_ref[...]
    part = tl // nsplit

    for s in range(nsplit):
        r0 = s * part
        rows = slice(r0, r0 + part)
        hb = hb_ref[rows, :]
        z_ref[rows, _OFF_Q:_OFF_G] = _dot(hb, win_ref[:, _OFF_Q:_OFF_G])

        for c0 in range(r0, r0 + part, chunk):
            crows = slice(c0, c0 + chunk)
            f, kk, lf2 = _gates(z_ref, crows, lb)
            amat, qg, x = _hgrn_intra(z_ref[crows, _OFF_Q:_OFF_Q + D_B], kk, f, lf2, sel_ref,
                                      g_ref.at[c0 // chunk], chunk)
            g_last = x[chunk:chunk + 1]
            dec = jnp.exp2(g_last)
            kg = (kk * jnp.exp2(g_last - x[0:chunk])).astype(BF16)
            qg = qg.astype(BF16)
            vb = z_ref[crows, _OFF_I:_OFF_I + D_B].astype(BF16)
            for h, sl in enumerate(_HEADS):
                st = st_ref[h]
                lhs = jnp.concatenate([qg[:, sl], amat[h].astype(BF16)], axis=1)
                rhs = jnp.concatenate([st.T.astype(BF16), vb[:, sl]], axis=0)
                o_ref[crows, sl] = _dot(lhs, rhs)
                st_ref[h] = dec[:, sl] * st + _dot_tn(vb[:, sl], kg[:, sl])

        z_ref[rows, 0:_OFF_Q] = _dot(hb, win_ref[:, 0:_OFF_Q])
        z_ref[rows, _OFF_G:D_IN] = _dot(hb, win_ref[:, _OFF_G:D_IN])

        u = z_ref[rows, _OFF_AC:_OFF_AC + D_A] * z_ref[rows, _OFF_AV:_OFF_AV + D_A]
        ext_ref[SUBLANES + r0:SUBLANES + r0 + part, :] = u
        conv = (_shifted(ext_ref, SUBLANES, part, 2, row0=r0) * cw[0:1, :]
                + _shifted(ext_ref, SUBLANES, part, 1, row0=r0) * cw[1:2, :] + u * cw[2:3, :])
        y_ref[rows, 0:D_A] = (z_ref[rows, _OFF_AB:_OFF_AB + D_A] * conv).astype(BF16)
        _mixer_b_out(z_ref, o_ref, y_ref, hn, rows)
        xo_ref[rows, :] = x_ref[rows, :] + _dot(y_ref[rows, :], wout_ref[...])

    carry = ext_ref[tl:tl + SUBLANES, :]
    ext_ref[0:SUBLANES, :] = carry

    @pl.when(t == pl.num_programs(1) - 1)
    def _():
        convo_ref[0] = carry
        for h in range(N_HEADS):
            so_ref[0, h] = st_ref[h].T


def _mixer0_sample_kernel(x_ref, e1_ref, e2_ref, s0_ref, gn_ref, win_ref, cw_ref, lbp_ref, hn_ref,
                          wout_ref, sel_ref,
                          xo_ref, uo_ref, so_ref,
                          z_ref, o_ref, y_ref, lf_ref, qg_ref, kg_ref, *, tl, seq):
    x = x_ref[...]
    z_ref[...] = _dot(_rms(x, gn_ref[...]).astype(BF16), win_ref[...])

    u = z_ref[:, _OFF_AC:_OFF_AC + D_A] * z_ref[:, _OFF_AV:_OFF_AV + D_A]
    cw = cw_ref[...]
    tpos = lax.broadcasted_iota(jnp.int32, (tl, 1), 0) & (seq - 1)
    u1 = jnp.where(tpos >= 1, pltpu.roll(u, 1, 0), e1_ref[...])
    u2 = jnp.where(tpos >= 2, pltpu.roll(u, 2, 0), e2_ref[...])
    uo_ref[...] = u
    conv = u2 * cw[0:1, :] + u1 * cw[1:2, :] + u * cw[2:3, :]
    y_ref[:, 0:D_A] = (z_ref[:, _OFF_AB:_OFF_AB + D_A] * conv).astype(BF16)

    f, kk, lf2 = _gates(z_ref, slice(None), _lower_bound(lbp_ref))
    amat, qg, xsel = _hgrn_intra(z_ref[:, _OFF_Q:_OFF_Q + D_B], kk, f, lf2, sel_ref, None, seq)
    v_all = z_ref[:, _OFF_I:_OFF_I + D_B].astype(BF16)
    o_ref[...] = jnp.concatenate(
        [_dot(amat[h].astype(BF16), v_all[:, sl]) for h, sl in enumerate(_HEADS)], axis=1)
    qg_ref[...] = qg
    kg_ref[...] = kk * jnp.exp2(xsel[tl:2 * tl])
    lf_ref[...] = lf2
    ones = jnp.ones((seq, HEAD), BF16)

    def seq_body(i, carry_unused):
        rows = pl.ds(pl.multiple_of(i * seq, seq), seq)
        hi, mid = _split_bf16(lf_ref[rows, :])
        qg_i = qg_ref[rows, :].astype(BF16)
        kg_i = kg_ref[rows, :].astype(BF16)
        vb = z_ref[rows, _OFF_I:_OFF_I + D_B].astype(BF16)
        for h, sl in enumerate(_HEADS):
            s0 = s0_ref[i, h]
            o_ref[rows, sl] = o_ref[rows, sl] + _dot(qg_i[:, sl], s0.astype(BF16))
            g_last = _dot_tn(hi[:, sl], ones) + _dot_tn(mid[:, sl], ones)
            so_ref[i, h] = jnp.exp2(g_last) * s0 + _dot_tn(kg_i[:, sl], vb[:, sl])
        return carry_unused

    lax.fori_loop(0, tl // seq, seq_body, 0, unroll=4)

    _mixer_b_out(z_ref, o_ref, y_ref, hn_ref[...], slice(None))
    xo_ref[...] = x + _dot(y_ref[...], wout_ref[...])


def _const_spec(shape):
    nd = len(shape)
    return pl.BlockSpec(shape, lambda *_: (0,) * nd, pipeline_mode=pl.Buffered(1))


def _layer_spec(shape, layer):
    nd = len(shape) - 1
    return pl.BlockSpec((None,) + tuple(shape[1:]), lambda *_: (layer,) + (0,) * nd,
                        pipeline_mode=pl.Buffered(1))


def _mixer0_prompt(x2d, n_seq, seq_len, gn, win, cw, lbp, hn, wout, cast, *, tl, nsplit):
    nt = seq_len // tl
    assert n_seq * nt >= CAST_STEPS
    sel = jnp.asarray(_selector(HGRN_CHUNK, HGRN_CHUNK), BF16)
    kern = functools.partial(_mixer0_prompt_kernel, tl=tl, chunk=HGRN_CHUNK, nsplit=nsplit)
    row_spec = pl.BlockSpec((tl, D_MODEL), lambda n, t: (n * nt + t, 0))
    consts = (gn, win, cw, lbp, hn, wout, sel)
    riders = [_cast_rider(w, layer, nt) for w, layer in cast]
    return pl.pallas_call(
        kern,
        grid=(n_seq, nt),
        in_specs=[row_spec] + [_const_spec(a.shape) for a in consts] + [r[0] for r in riders],
        out_specs=[row_spec,
                   pl.BlockSpec((1, SUBLANES, D_A), lambda n, t: (n, 0, 0)),
                   pl.BlockSpec((1, N_HEADS, HEAD, HEAD), lambda n, t: (n, 0, 0, 0))]
                  + [r[1] for r in riders],
        out_shape=[jax.ShapeDtypeStruct(x2d.shape, F32),
                   jax.ShapeDtypeStruct((n_seq, SUBLANES, D_A), F32),
                   jax.ShapeDtypeStruct((n_seq, N_HEADS, HEAD, HEAD), F32)]
                  + [r[2] for r in riders],
        scratch_shapes=[pltpu.VMEM((tl, D_MODEL), BF16),
                        pltpu.VMEM((tl, D_IN), F32),
                        pltpu.VMEM((tl, D_B), F32),
                        pltpu.VMEM((tl, D_A + D_B), BF16),
                        pltpu.VMEM((SUBLANES + tl, D_A), F32),
                        pltpu.VMEM((N_HEADS, HEAD, HEAD), F32),
                        pltpu.VMEM((tl // HGRN_CHUNK, HGRN_CHUNK, D_B), F32)],
        compiler_params=pltpu.CompilerParams(
            dimension_semantics=("arbitrary", "arbitrary"), vmem_limit_bytes=VMEM_LIMIT),
        name="mixer0_prompt",
    )(x2d, *consts, *[w for w, _ in cast])


def _mixer0_sample(x2d, e1, e2, s0, gn, win, cw, lbp, hn, wout, *, seq_len, bn):
    n_seq = s0.shape[0]
    tl = bn * seq_len
    sel = jnp.asarray(_selector(tl, seq_len), BF16)
    kern = functools.partial(_mixer0_sample_kernel, tl=tl, seq=seq_len)
    row_spec = pl.BlockSpec((tl, D_MODEL), lambda i: (i, 0))
    conv_spec = pl.BlockSpec((tl, D_A), lambda i: (i, 0))
    st_spec = pl.BlockSpec((bn, N_HEADS, HEAD, HEAD), lambda i: (i, 0, 0, 0))
    consts = (gn, win, cw, lbp, hn, wout, sel)
    return pl.pallas_call(
        kern,
        grid=(n_seq // bn,),
        in_specs=[row_spec, conv_spec, conv_spec, st_spec] + [_const_spec(a.shape) for a in consts],
        out_specs=[row_spec, conv_spec, st_spec],
        out_shape=[jax.ShapeDtypeStruct(x2d.shape, F32),
                   jax.ShapeDtypeStruct((x2d.shape[0], D_A), F32),
                   jax.ShapeDtypeStruct(s0.shape, F32)],
        scratch_shapes=[pltpu.VMEM((tl, D_IN), F32),
                        pltpu.VMEM((tl, D_B), F32),
                        pltpu.VMEM((tl, D_A + D_B), BF16),
                        pltpu.VMEM((tl, D_B), F32),
                        pltpu.VMEM((tl, D_B), F32),
                        pltpu.VMEM((tl, D_B), F32)],
        compiler_params=pltpu.CompilerParams(
            dimension_semantics=("arbitrary",), vmem_limit_bytes=VMEM_LIMIT),
        name="mixer0_sample",
    )(x2d, e1, e2, s0, *consts)


def _pool_mix(x, gm, ext_ref, padr, row0, ns, pos0, wp_ref, sc):
    n = x.shape[0]
    h = _rms(x, gm)
    ext_ref[padr + row0:padr + row0 + n, :] = h
    pos = pos0 + (lax.broadcasted_iota(jnp.int32, (n, 1), 0) >> _log2(ns))
    ys = []
    for gi, w in enumerate(POOL_WINDOWS):
        cols = slice(gi * POOL_GROUP, (gi + 1) * POOL_GROUP)
        hg = h[:, cols]
        if ns == 1:
            hist = _round_up(POOL_BUF, SUBLANES)
            s = ext_ref[pl.ds(padr + row0 - hist, n + hist), cols]
            sh = 1
            while sh < w:
                s = s + pltpu.roll(s, sh, 0)
                sh *= 2
            acc = s[hist:hist + n, :]
        else:
            acc = hg
            for k in range(1, w):
                acc = acc + _shifted(ext_ref, padr, n, k * ns, cols, row0)
        cnt = jnp.minimum(w, pos + 1).astype(F32)
        ys.append(_dot((acc / cnt - hg).astype(BF16), wp_ref[gi]))
    return x + jnp.concatenate(ys, axis=1) * sc


def _ffn_kernel(*refs, tl, ns, padr, final, pool, nsplit, n_cast):
    refs = list(refs)
    x_ref, st_ref = refs[:2]
    del refs[:2]
    if pool is not None:
        pst_ref, gm_ref, wp_ref, sc_ref = refs[:4]
        del refs[:4]
        ppad, start = pool
    gn_ref, wgu_ref, cw_ref, cb_ref, wd_ref, gf_ref = refs[:6]
    del refs[:6]
    cast_in = refs[:n_cast]
    del refs[:n_cast]
    xo_ref, sto_ref = refs[:2]
    del refs[:2]
    if pool is not None:
        psto_ref = refs.pop(0)
    cast_out = refs[:n_cast]
    del refs[:n_cast]
    ext_ref = refs.pop(0)
    if pool is not None:
        pext_ref = refs.pop(0)
    assert not refs
    t = pl.program_id(1)
    if n_cast:
        _run_cast_riders(pl.num_programs(1), list(zip(cast_in, cast_out)))

    @pl.when(t == 0)
    def _():
        ext_ref[0:padr, :] = st_ref[0]
        if pool is not None:
            pext_ref[0:ppad, :] = pst_ref[0]

    cw = cw_ref[...]
    cb = cb_ref[...]
    part = tl // nsplit
    for s in range(nsplit):
        r0 = s * part
        rows = slice(r0, r0 + part)
        x = x_ref[rows, :]
        if pool is not None:
            pos0 = start + ((t * tl + r0) >> _log2(ns))
            x = _pool_mix(x, gm_ref[...], pext_ref, ppad, r0, ns, pos0, wp_ref, sc_ref[...])
        hb = _rms(x, gn_ref[...]).astype(BF16)
        g = _dot(hb, wgu_ref[:, 0:D_FF])
        ext_ref[padr + r0:padr + r0 + part, :] = g
        gc = (_shifted(ext_ref, padr, part, 2 * ns, row0=r0) * cw[0:1, :]
              + _shifted(ext_ref, padr, part, ns, row0=r0) * cw[1:2, :]
              + g * cw[2:3, :] + cb)
        up = _dot(hb, wgu_ref[:, D_FF:2 * D_FF])
        act = (_silu(gc) * up).astype(BF16)
        xo = x + _dot(act, wd_ref[...])
        if final:
            xo = _rms(xo, gf_ref[...])
        xo_ref[rows, :] = xo

    carry = ext_ref[tl:tl + padr, :]
    ext_ref[0:padr, :] = carry
    if pool is not None:
        pcarry = pext_ref[tl:tl + ppad, :]
        pext_ref[0:ppad, :] = pcarry

    @pl.when(t == pl.num_programs(1) - 1)
    def _():
        sto_ref[0] = carry
        if pool is not None:
            psto_ref[0] = pcarry


def _ffn(x2d, state, p, layer, wgu, wd, *, tl, ns, final, nsplit=1, pool_state=None, start=0, cast=()):
    n_groups, padr, _ = state.shape
    rows_per_group = x2d.shape[0] // n_groups
    nt = rows_per_group // tl
    assert padr % SUBLANES == 0 and padr >= CONV_HIST * ns and (nt == 1 or tl >= padr)
    assert not cast or n_groups * nt >= CAST_STEPS
    row_spec = pl.BlockSpec((tl, D_MODEL), lambda n, t: (n * nt + t, 0))
    st_spec = pl.BlockSpec((1, padr, D_FF), lambda n, t: (n, 0, 0))
    ins = [x2d, state]
    in_specs = [row_spec, st_spec]
    out_specs = [row_spec, st_spec]
    out_shape = [jax.ShapeDtypeStruct(x2d.shape, F32), jax.ShapeDtypeStruct(state.shape, F32)]
    scratch = [pltpu.VMEM((padr + tl, D_FF), F32)]
    pool = None
    if pool_state is not None:
        ppad = pool_state.shape[1]
        assert ppad % SUBLANES == 0 and ppad >= POOL_BUF * ns and (nt == 1 or tl >= ppad)
        pool = (ppad, start)
        pst_spec = pl.BlockSpec((1, ppad, D_MODEL), lambda n, t: (n, 0, 0))
        pool_consts = (p["norm_mix"][layer], p["pool_w"], p["pool_scale"])
        ins += [pool_state, *pool_consts]
        in_specs += [pst_spec] + [_const_spec(a.shape) for a in pool_consts]
        out_specs.append(pst_spec)
        out_shape.append(jax.ShapeDtypeStruct(pool_state.shape, F32))
        scratch.append(pltpu.VMEM((ppad + tl, D_MODEL), F32))
    ins += [p["norm_ffn"], wgu, p["ffn_cw"], p["ffn_cb"], wd, p["norm_final"]]
    in_specs += [_layer_spec(p["norm_ffn"].shape, layer), _const_spec(wgu.shape),
                 _layer_spec(p["ffn_cw"].shape, layer), _layer_spec(p["ffn_cb"].shape, layer),
                 _const_spec(wd.shape), _const_spec(p["norm_final"].shape)]
    riders = [_cast_rider(w, l, nt) for w, l in cast]
    ins += [w for w, _ in cast]
    in_specs += [r[0] for r in riders]
    out_specs += [r[1] for r in riders]
    out_shape += [r[2] for r in riders]
    kern = functools.partial(_ffn_kernel, tl=tl, ns=ns, padr=padr, final=final, pool=pool,
                             nsplit=nsplit, n_cast=len(cast))
    return pl.pallas_call(
        kern,
        grid=(n_groups, nt),
        in_specs=in_specs,
        out_specs=out_specs,
        out_shape=out_shape,
        scratch_shapes=scratch,
        compiler_params=pltpu.CompilerParams(
            dimension_semantics=("arbitrary", "arbitrary"), vmem_limit_bytes=VMEM_LIMIT),
        name=("pool_" if pool else "") + ("ffn_final" if final else "ffn"),
    )(*ins)


def _pool_kernel(x_ref, st_ref, gn_ref, wp_ref, sc_ref, xo_ref, sto_ref, ext_ref,
                 *, tl, ns, padr, start):
    t = pl.program_id(1)

    @pl.when(t == 0)
    def _():
        ext_ref[0:padr, :] = st_ref[0]

    pos0 = start + ((t * tl) >> _log2(ns))
    xo_ref[...] = _pool_mix(x_ref[...], gn_ref[...], ext_ref, padr, 0, ns, pos0, wp_ref, sc_ref[...])

    carry = ext_ref[tl:tl + padr, :]
    ext_ref[0:padr, :] = carry

    @pl.when(t == pl.num_programs(1) - 1)
    def _():
        sto_ref[0] = carry


def _pool(x2d, state, gn, wp, sc, *, tl, ns, start):
    n_groups, padr, _ = state.shape
    rows_per_group = x2d.shape[0] // n_groups
    nt = rows_per_group // tl
    assert padr % SUBLANES == 0 and padr >= POOL_BUF * ns and (nt == 1 or tl >= padr)
    kern = functools.partial(_pool_kernel, tl=tl, ns=ns, padr=padr, start=start)
    row_spec = pl.BlockSpec((tl, D_MODEL), lambda n, t: (n * nt + t, 0))
    st_spec = pl.BlockSpec((1, padr, D_MODEL), lambda n, t: (n, 0, 0))
    return pl.pallas_call(
        kern,
        grid=(n_groups, nt),
        in_specs=[row_spec, st_spec, _const_spec(gn.shape), _const_spec(wp.shape),
                  _const_spec(sc.shape)],
        out_specs=[row_spec, st_spec],
        out_shape=[jax.ShapeDtypeStruct(x2d.shape, F32),
                   jax.ShapeDtypeStruct(state.shape, F32)],
        scratch_shapes=[pltpu.VMEM((padr + tl, D_MODEL), F32)],
        compiler_params=pltpu.CompilerParams(
            dimension_semantics=("arbitrary", "arbitrary"), vmem_limit_bytes=VMEM_LIMIT),
        name="pool",
    )(x2d, state, gn, wp, sc)


def kernel(x_prompt, x_sample, state_conv_a, state_hgrn, state_pool, state_ffn, norm_mix, norm_ffn,
           norm_final, w_in, conv_a_w, hgrn_lower_bounds, hgrn_norm, w_out, pool_w, pool_scale,
           ffn_w_gu, ffn_conv_w, ffn_conv_b, ffn_w_down):
    batch, seq, _ = x_prompt.shape
    dec_batch, dec_seq, _ = x_sample.shape

    p = dict(
        norm_mix=norm_mix[:, None, :], norm_ffn=norm_ffn[:, None, :], norm_final=norm_final[None, :],
        ffn_cw=ffn_conv_w, ffn_cb=ffn_conv_b[:, None, :], pool_w=pool_w[0].astype(BF16),
        pool_scale=pool_scale[0][None, :])
    win = w_in[0].astype(BF16)
    wout = w_out[0].astype(BF16)
    mix_args = (p["norm_mix"][0], win, conv_a_w[0], hgrn_lower_bounds, hgrn_norm[0][None, :], wout)

    xp = x_prompt.reshape(batch * seq, D_MODEL)
    x1, conv_p, hgrn_p, wgu0, wd0 = _mixer0_prompt(
        xp, batch, seq, *mix_args, [(ffn_w_gu, 0), (ffn_w_down, 0)], tl=512, nsplit=2)
    pad_ffn = _round_up(CONV_HIST, SUBLANES)
    pad_pool = _round_up(POOL_BUF, SUBLANES)
    zf = jnp.zeros((batch, pad_ffn, D_FF), F32)
    zp = jnp.zeros((batch, pad_pool, D_MODEL), F32)
    x2, ffn0_p, wgu1, wd1 = _ffn(x1, zf, p, 0, wgu0, wd0, tl=512, ns=1, final=False, nsplit=2,
                                 cast=[(ffn_w_gu, 1), (ffn_w_down, 1)])
    yp, ffn1_p, pool_p = _ffn(x2, zf, p, 1, wgu1, wd1, tl=512, ns=1, final=True, nsplit=2,
                              pool_state=zp, start=0)
    y_prompt = yp.reshape(batch, seq, D_MODEL)
    conv_prompt = conv_p[None, :, SUBLANES - CONV_HIST:, :]
    hgrn_prompt = hgrn_p[None]
    pool_prompt = pool_p[None, :, pad_pool - POOL_BUF:, :]
    ffn_prompt = jnp.stack([ffn0_p, ffn1_p])[:, :, pad_ffn - CONV_HIST:, :]

    xs = x_sample.reshape(dec_batch * dec_seq, D_MODEL)
    cs = state_conv_a[0]
    zeros_tail = jnp.zeros((dec_batch, dec_seq - 1, D_A), F32)
    e1 = jnp.concatenate([cs[:, 1:2], zeros_tail], axis=1).reshape(dec_batch * dec_seq, D_A)
    e2 = jnp.concatenate([cs, zeros_tail[:, 1:]], axis=1).reshape(dec_batch * dec_seq, D_A)
    x1s, us, hgrn_s = _mixer0_sample(xs, e1, e2, state_hgrn[0], *mix_args, seq_len=dec_seq, bn=16)
    conv_sample = us.reshape(dec_batch, dec_seq, D_A)[None, :, dec_seq - CONV_HIST:, :]

    def to_tm(a):
        return a.transpose(1, 0, 2).reshape(1, a.shape[1] * dec_batch, a.shape[2])

    def from_tm(a, rows):
        return a.reshape(rows, dec_batch, a.shape[-1]).transpose(1, 0, 2)

    x1t = to_tm(x1s.reshape(dec_batch, dec_seq, D_MODEL))[0]
    x2t, ffn0_s = _ffn(x1t, to_tm(state_ffn[0]), p, 0, wgu0, wd0, tl=512, ns=dec_batch, final=False,
                       nsplit=2)
    x3t, pool_s = _pool(x2t, to_tm(state_pool[0]), p["norm_mix"][1], p["pool_w"], p["pool_scale"],
                        tl=dec_batch * dec_seq, ns=dec_batch, start=PAST_LEN)
    ys, ffn1_s = _ffn(x3t, to_tm(state_ffn[1]), p, 1, wgu1, wd1, tl=512, ns=dec_batch, final=True,
                      nsplit=2)
    y_sample = from_tm(ys, dec_seq)
    pool_sample = from_tm(pool_s[0], POOL_BUF)[None]
    ffn_sample = jnp.stack([from_tm(ffn0_s[0], CONV_HIST), from_tm(ffn1_s[0], CONV_HIST)])

    return (y_prompt, y_sample, conv_prompt, conv_sample, hgrn_prompt, hgrn_s[None],
            pool_prompt, pool_sample, ffn_prompt, ffn_sample)
```

```python
import functools

import jax
import jax.numpy as jnp
import numpy as np
from jax import lax
from jax.experimental import pallas as pl
from jax.experimental.pallas import tpu as pltpu

D_MODEL = 1024
D_A = 512
N_HEADS = 4
HEAD = 128
D_B = N_HEADS * HEAD
D_IN = 3 * D_A + 4 * D_B
D_FF = 2816
POOL_WINDOWS = (2, 4, 8, 16)
POOL_GROUP = D_MODEL // len(POOL_WINDOWS)
POOL_BUF = max(POOL_WINDOWS) - 1
CONV_HIST = 2
PAST_LEN = 16384
EPS = 1e-6
LOG2E = 1.4426950408889634

SUBLANES = 8
HGRN_CHUNK = 64
VMEM_LIMIT = 56 * 1024 * 1024
CAST_STEPS = 16

F32 = jnp.float32
BF16 = jnp.bfloat16

_OFF_AC, _OFF_AB, _OFF_AV = 0, D_A, 2 * D_A
_OFF_Q, _OFF_F, _OFF_I, _OFF_G = (3 * D_A, 3 * D_A + D_B, 3 * D_A + 2 * D_B, 3 * D_A + 3 * D_B)
_HEADS = [slice(h * HEAD, (h + 1) * HEAD) for h in range(N_HEADS)]


def _round_up(x, m):
    return (x + m - 1) // m * m


def _log2(x):
    assert x & (x - 1) == 0
    return x.bit_length() - 1


def _rms(x, g):
    return x * lax.rsqrt(jnp.mean(x * x, axis=-1, keepdims=True) + EPS) * g


def _silu(x):
    return x * jax.nn.sigmoid(x)


def _dot(a, b):
    return jnp.dot(a, b, preferred_element_type=F32)


def _dot_nt(a, b):
    return lax.dot_general(a, b, (((1,), (1,)), ((), ())), preferred_element_type=F32)


def _dot_tn(a, b):
    return lax.dot_general(a, b, (((0,), (0,)), ((), ())), preferred_element_type=F32)


def _dep_zero(a):
    u = lax.bitcast_convert_type(a, jnp.int32)
    return lax.shift_right_logical(lax.shift_right_logical(u, 16), 16).astype(F32)


def _shifted(ext_ref, padr, n, k_rows, cols=slice(None), row0=0):
    off = padr + row0 - k_rows
    assert off >= 0
    base = off // SUBLANES * SUBLANES
    rem = off - base
    if rem == 0:
        return ext_ref[pl.ds(off, n), cols]
    win = ext_ref[pl.ds(base, n + SUBLANES), cols]
    return pltpu.roll(win, SUBLANES - rem, 0)[SUBLANES:SUBLANES + n, :]


def _levels(seg):
    return [1 << i for i in range(_log2(seg))]


def _selector_levels(seg):
    return [b for b in _levels(seg) if 1 < b < SUBLANES]


def _selector(c, seg):
    t = np.arange(c)[:, None]
    u = np.arange(c)[None, :]
    same_seg = (t // seg) == (u // seg)
    blocks = [same_seg & (u <= t)]
    blocks.append(same_seg & (u > t) if seg < c else np.ones((SUBLANES, c), bool))
    for b in _selector_levels(seg):
        pair = (t // (2 * b)) == (u // (2 * b))
        anchor = (t // (2 * b)) * 2 * b + b - 1
        right = (t // b) % 2 == 1
        blocks.append(pair & np.where(right, (u > anchor) & (u <= t), (u > t) & (u <= anchor)))
    sel = np.concatenate(blocks, axis=0).astype(np.float32)
    return np.concatenate([sel, sel], axis=1)


def _split_bf16(a):
    hi = a.astype(BF16)
    return hi, (a - hi.astype(F32)).astype(BF16)


def _hgrn_intra(q, kk, f, lf2, sel_ref, g_ref, seg):
    c = q.shape[0]
    x = _dot(sel_ref[...], jnp.concatenate(_split_bf16(lf2), axis=0))
    g = x[0:c]

    row = lax.broadcasted_iota(jnp.int32, (c, 1), 0)
    rowi = lax.broadcasted_iota(jnp.int32, (c, c), 0)
    coli = lax.broadcasted_iota(jnp.int32, (c, c), 1)
    diff = rowi ^ coli
    lower = rowi > coli

    p0 = q * kk
    p1 = q * pltpu.roll(kk, 1, 0) * f
    amat = []
    for sl in _HEADS:
        a0 = jnp.sum(p0[:, sl], axis=-1, keepdims=True)
        a1 = jnp.sum(p1[:, sl], axis=-1, keepdims=True)
        amat.append(jnp.where(rowi == coli, a0, jnp.where(lower & (diff == 1), a1, 0.0)))

    sel_blocks = {b: x[off:off + c] for b, off in
                  zip(_selector_levels(seg), range(c + (c if seg < c else SUBLANES), x.shape[0], c))}
    if seg > SUBLANES:
        g_ref[...] = g
    for b in _levels(seg)[1:]:
        if b in sel_blocks:
            ex = sel_blocks[b]
        else:
            parts = []
            for pair in range(c // (2 * b)):
                anchor = g_ref[pl.ds(pair * 2 * b + b - 1, 1), :]
                parts.append(g[pair * 2 * b:(pair + 1) * 2 * b, :] - anchor)
            ex = -jnp.abs(parts[0] if len(parts) == 1 else jnp.concatenate(parts, axis=0))
        w = jnp.exp2(ex)
        right = ((row >> _log2(b)) & 1) == 1
        ql = jnp.where(right, q * w, 0.0).astype(BF16)
        kl = jnp.where(right, 0.0, kk * w).astype(BF16)
        mask = lower & ((diff >> _log2(b)) == 1)
        for h, sl in enumerate(_HEADS):
            amat[h] = jnp.where(mask, _dot_nt(ql[:, sl], kl[:, sl]), amat[h])
    return amat, q * jnp.exp2(g), x


def _gates(z_ref, rows, lb):
    f = lb + (1.0 - lb) * jax.nn.sigmoid(z_ref[rows, _OFF_F:_OFF_F + D_B])
    return f, 1.0 - f, jnp.log(f) * LOG2E


def _lower_bound(lbp_ref):
    lbp = lbp_ref[...]
    lbe = jnp.exp(lbp - jnp.max(lbp, axis=0, keepdims=True))
    return lbe[0:1, :] / jnp.sum(lbe, axis=0, keepdims=True)


def _mixer_b_out(z_ref, o_ref, y_ref, hn, rows):
    for h, sl in enumerate(_HEADS):
        on = _rms(o_ref[rows, sl], hn[:, sl])
        gate = z_ref[rows, _OFF_G + h * HEAD:_OFF_G + (h + 1) * HEAD]
        y_ref[rows, D_A + h * HEAD:D_A + (h + 1) * HEAD] = (on * _silu(gate)).astype(BF16)


def _cast_rider(w, layer, nt):
    _, r, c = w.shape
    rows = r // CAST_STEPS
    assert rows * CAST_STEPS == r and rows % (2 * SUBLANES) == 0

    def blk(n, t):
        return jnp.minimum(n * nt + t, CAST_STEPS - 1)

    return (pl.BlockSpec((None, rows, c), lambda n, t: (layer, blk(n, t), 0)),
            pl.BlockSpec((rows, c), lambda n, t: (blk(n, t), 0)),
            jax.ShapeDtypeStruct((r, c), BF16))


def _run_cast_riders(nt, pairs):
    @pl.when(pl.program_id(0) * nt + pl.program_id(1) < CAST_STEPS)
    def _():
        for src, dst in pairs:
            dst[...] = src[...].astype(BF16)


def _mixer0_prompt_kernel(x_ref, gn_ref, win_ref, cw_ref, lbp_ref, hn_ref, wout_ref, sel_ref,
                          c0_ref, c1_ref,
                          xo_ref, convo_ref, so_ref, c0o_ref, c1o_ref,
                          hb_ref, z_ref, o_ref, y_ref, ext_ref, st_ref, g_ref, *, tl, chunk, nsplit):
    t = pl.program_id(1)
    _run_cast_riders(pl.num_programs(1), [(c0_ref, c0o_ref), (c1_ref, c1o_ref)])

    @pl.when(t == 0)
    def _():
        ext_ref[0:SUBLANES, :] = jnp.zeros((SUBLANES, D_A), F32)
        st_ref[...] = jnp.zeros((N_HEADS, HEAD, HEAD), F32)

    hb_ref[...] = _rms(x_ref[...], gn_ref[...]).astype(BF16)
    lb = _lower_bound(lbp_ref)
    cw = cw_ref[...]
    hn = hn_ref[...]
    part = tl // nsplit

    for s in range(nsplit):
        r0 = s * part
        rows = slice(r0, r0 + part)
        hb = hb_ref[rows, :]
        z_ref[rows, _OFF_Q:_OFF_G] = _dot(hb, win_ref[:, _OFF_Q:_OFF_G])

        for c0 in range(r0, r0 + part, chunk):
            crows = slice(c0, c0 + chunk)
            f, kk, lf2 = _gates(z_ref, crows, lb)
            amat, qg, x = _hgrn_intra(z_ref[crows, _OFF_Q:_OFF_Q + D_B], kk, f, lf2, sel_ref,
                                      g_ref.at[c0 // chunk], chunk)
            g_last = x[chunk:chunk + 1]
            dec = jnp.exp2(g_last)
            kg = (kk * jnp.exp2(g_last - x[0:chunk])).astype(BF16)
            qg = qg.astype(BF16)
            vb = z_ref[crows, _OFF_I:_OFF_I + D_B].astype(BF16)
            for h, sl in enumerate(_HEADS):
                st = st_ref[h]
                lhs = jnp.concatenate([qg[:, sl], amat[h].astype(BF16)], axis=1)
                rhs = jnp.concatenate([st.T.astype(BF16), vb[:, sl]], axis=0)
                o_ref[crows, sl] = _dot(lhs, rhs)
                st_ref[h] = dec[:, sl] * st + _dot_tn(vb[:, sl], kg[:, sl])

        z_ref[rows, 0:_OFF_Q] = _dot(hb, win_ref[:, 0:_OFF_Q])
        z_ref[rows, _OFF_G:D_IN] = _dot(hb, win_ref[:, _OFF_G:D_IN])

        u = z_ref[rows, _OFF_AC:_OFF_AC + D_A] * z_ref[rows, _OFF_AV:_OFF_AV + D_A]
        ext_ref[SUBLANES + r0:SUBLANES + r0 + part, :] = u
        conv = (_shifted(ext_ref, SUBLANES, part, 2, row0=r0) * cw[0:1, :]
                + _shifted(ext_ref, SUBLANES, part, 1, row0=r0) * cw[1:2, :] + u * cw[2:3, :])
        y_ref[rows, 0:D_A] = (z_ref[rows, _OFF_AB:_OFF_AB + D_A] * conv).astype(BF16)
        _mixer_b_out(z_ref, o_ref, y_ref, hn, rows)
        xo_ref[rows, :] = x_ref[rows, :] + _dot(y_ref[rows, :], wout_ref[...])

    carry = ext_ref[tl:tl + SUBLANES, :]
    ext_ref[0:SUBLANES, :] = carry

    @pl.when(t == pl.num_programs(1) - 1)
    def _():
        convo_ref[0] = carry
        for h in range(N_HEADS):
            so_ref[0, h] = st_ref[h].T


def _mixer0_sample_kernel(x_ref, e1_ref, e2_ref, s0_ref, gn_ref, win_ref, cw_ref, lbp_ref, hn_ref,
                          wout_ref, sel_ref,
                          xo_ref, uo_ref, so_ref,
                          z_ref, o_ref, y_ref, lf_ref, qg_ref, kg_ref, *, tl, seq):
    x = x_ref[...]
    z_ref[...] = _dot(_rms(x, gn_ref[...]).astype(BF16), win_ref[...])

    u = z_ref[:, _OFF_AC:_OFF_AC + D_A] * z_ref[:, _OFF_AV:_OFF_AV + D_A]
    cw = cw_ref[...]
    tpos = lax.broadcasted_iota(jnp.int32, (tl, 1), 0) & (seq - 1)
    u1 = jnp.where(tpos >= 1, pltpu.roll(u, 1, 0), e1_ref[...])
    u2 = jnp.where(tpos >= 2, pltpu.roll(u, 2, 0), e2_ref[...])
    uo_ref[...] = u
    conv = u2 * cw[0:1, :] + u1 * cw[1:2, :] + u * cw[2:3, :]
    y_ref[:, 0:D_A] = (z_ref[:, _OFF_AB:_OFF_AB + D_A] * conv).astype(BF16)

    f, kk, lf2 = _gates(z_ref, slice(None), _lower_bound(lbp_ref))
    amat, qg, xsel = _hgrn_intra(z_ref[:, _OFF_Q:_OFF_Q + D_B], kk, f, lf2, sel_ref, None, seq)
    v_all = z_ref[:, _OFF_I:_OFF_I + D_B].astype(BF16)
    o_ref[...] = jnp.concatenate(
        [_dot(amat[h].astype(BF16), v_all[:, sl]) for h, sl in enumerate(_HEADS)], axis=1)
    qg_ref[...] = qg
    kg_ref[...] = kk * jnp.exp2(xsel[tl:2 * tl])
    lf_ref[...] = lf2
    ones = jnp.ones((seq, HEAD), BF16)

    def seq_body(i, carry_unused):
        rows = pl.ds(pl.multiple_of(i * seq, seq), seq)
        hi, mid = _split_bf16(lf_ref[rows, :])
        qg_i = qg_ref[rows, :].astype(BF16)
        kg_i = kg_ref[rows, :].astype(BF16)
        vb = z_ref[rows, _OFF_I:_OFF_I + D_B].astype(BF16)
        for h, sl in enumerate(_HEADS):
            s0 = s0_ref[i, h]
            o_ref[rows, sl] = o_ref[rows, sl] + _dot(qg_i[:, sl], s0.astype(BF16))
            g_last = _dot_tn(hi[:, sl], ones) + _dot_tn(mid[:, sl], ones)
            so_ref[i, h] = jnp.exp2(g_last) * s0 + _dot_tn(kg_i[:, sl], vb[:, sl])
        return carry_unused

    lax.fori_loop(0, tl // seq, seq_body, 0, unroll=4)

    _mixer_b_out(z_ref, o_ref, y_ref, hn_ref[...], slice(None))
    xo_ref[...] = x + _dot(y_ref[...], wout_ref[...])


def _const_spec(shape):
    nd = len(shape)
    return pl.BlockSpec(shape, lambda *_: (0,) * nd, pipeline_mode=pl.Buffered(1))


def _layer_spec(shape, layer):
    nd = len(shape) - 1
    return pl.BlockSpec((None,) + tuple(shape[1:]), lambda *_: (layer,) + (0,) * nd,
                        pipeline_mode=pl.Buffered(1))


def _mixer0_prompt(x2d, n_seq, seq_len, gn, win, cw, lbp, hn, wout, cast, *, tl, nsplit):
    nt = seq_len // tl
    assert n_seq * nt >= CAST_STEPS
    sel = jnp.asarray(_selector(HGRN_CHUNK, HGRN_CHUNK), BF16)
    kern = functools.partial(_mixer0_prompt_kernel, tl=tl, chunk=HGRN_CHUNK, nsplit=nsplit)
    row_spec = pl.BlockSpec((tl, D_MODEL), lambda n, t: (n * nt + t, 0))
    consts = (gn, win, cw, lbp, hn, wout, sel)
    riders = [_cast_rider(w, layer, nt) for w, layer in cast]
    return pl.pallas_call(
        kern,
        grid=(n_seq, nt),
        in_specs=[row_spec] + [_const_spec(a.shape) for a in consts] + [r[0] for r in riders],
        out_specs=[row_spec,
                   pl.BlockSpec((1, SUBLANES, D_A), lambda n, t: (n, 0, 0)),
                   pl.BlockSpec((1, N_HEADS, HEAD, HEAD), lambda n, t: (n, 0, 0, 0))]
                  + [r[1] for r in riders],
        out_shape=[jax.ShapeDtypeStruct(x2d.shape, F32),
                   jax.ShapeDtypeStruct((n_seq, SUBLANES, D_A), F32),
                   jax.ShapeDtypeStruct((n_seq, N_HEADS, HEAD, HEAD), F32)]
                  + [r[2] for r in riders],
        scratch_shapes=[pltpu.VMEM((tl, D_MODEL), BF16),
                        pltpu.VMEM((tl, D_IN), F32),
                        pltpu.VMEM((tl, D_B), F32),
                        pltpu.VMEM((tl, D_A + D_B), BF16),
                        pltpu.VMEM((SUBLANES + tl, D_A), F32),
                        pltpu.VMEM((N_HEADS, HEAD, HEAD), F32),
                        pltpu.VMEM((tl // HGRN_CHUNK, HGRN_CHUNK, D_B), F32)],
        compiler_params=pltpu.CompilerParams(
            dimension_semantics=("arbitrary", "arbitrary"), vmem_limit_bytes=VMEM_LIMIT),
        name="mixer0_prompt",
    )(x2d, *consts, *[w for w, _ in cast])


def _mixer0_sample(x2d, e1, e2, s0, gn, win, cw, lbp, hn, wout, *, seq_len, bn):
    n_seq = s0.shape[0]
    tl = bn * seq_len
    sel = jnp.asarray(_selector(tl, seq_len), BF16)
    kern = functools.partial(_mixer0_sample_kernel, tl=tl, seq=seq_len)
    row_spec = pl.BlockSpec((tl, D_MODEL), lambda i: (i, 0))
    conv_spec = pl.BlockSpec((tl, D_A), lambda i: (i, 0))
    st_spec = pl.BlockSpec((bn, N_HEADS, HEAD, HEAD), lambda i: (i, 0, 0, 0))
    consts = (gn, win, cw, lbp, hn, wout, sel)
    return pl.pallas_call(
        kern,
        grid=(n_seq // bn,),
        in_specs=[row_spec, conv_spec, conv_spec, st_spec] + [_const_spec(a.shape) for a in consts],
        out_specs=[row_spec, conv_spec, st_spec],
        out_shape=[jax.ShapeDtypeStruct(x2d.shape, F32),
                   jax.ShapeDtypeStruct((x2d.shape[0], D_A), F32),
                   jax.ShapeDtypeStruct(s0.shape, F32)],
        scratch_shapes=[pltpu.VMEM((tl, D_IN), F32),
                        pltpu.VMEM((tl, D_B), F32),
                        pltpu.VMEM((tl, D_A + D_B), BF16),
                        pltpu.VMEM((tl, D_B), F32),
                        pltpu.VMEM((tl, D_B), F32),
                        pltpu.VMEM((tl, D_B), F32)],
        compiler_params=pltpu.CompilerParams(
            dimension_semantics=("arbitrary",), vmem_limit_bytes=VMEM_LIMIT),
        name="mixer0_sample",
    )(x2d, e1, e2, s0, *consts)


def _pool_mix(x, gm, ext_ref, padr, row0, ns, pos0, wp_ref, sc):
    n = x.shape[0]
    h = _rms(x, gm)
    ext_ref[padr + row0:padr + row0 + n, :] = h
    pos = pos0 + (lax.broadcasted_iota(jnp.int32, (n, 1), 0) >> _log2(ns))
    ys = []
    for gi, w in enumerate(POOL_WINDOWS):
        cols = slice(gi * POOL_GROUP, (gi + 1) * POOL_GROUP)
        hg = h[:, cols]
        if ns == 1:
            hist = _round_up(POOL_BUF, SUBLANES)
            s = ext_ref[pl.ds(padr + row0 - hist, n + hist), cols]
            sh = 1
            while sh < w:
                s = s + pltpu.roll(s, sh, 0)
                sh *= 2
            acc = s[hist:hist + n, :]
        else:
            acc = hg
            for k in range(1, w):
                acc = acc + _shifted(ext_ref, padr, n, k * ns, cols, row0)
        cnt = jnp.minimum(w, pos + 1).astype(F32)
        ys.append(_dot((acc / cnt - hg).astype(BF16), wp_ref[gi]))
    return x + jnp.concatenate(ys, axis=1) * sc


def _ffn_kernel(*refs, tl, ns, padr, final, pool, nsplit, n_cast, lookahead):
    refs = list(refs)
    x_ref, st_ref = refs[:2]
    del refs[:2]
    if lookahead:
        xn_ref = refs.pop(0)
    if pool is not None:
        pst_ref, gm_ref, wp_ref, sc_ref = refs[:4]
        del refs[:4]
        ppad, start = pool
    gn_ref, wgu_ref, cw_ref, cb_ref, wd_ref, gf_ref = refs[:6]
    del refs[:6]
    cast_in = refs[:n_cast]
    del refs[:n_cast]
    xo_ref, sto_ref = refs[:2]
    del refs[:2]
    if pool is not None:
        psto_ref = refs.pop(0)
    cast_out = refs[:n_cast]
    del refs[:n_cast]
    ext_ref = refs.pop(0)
    if pool is not None:
        pext_ref = refs.pop(0)
    if lookahead:
        hb_buf = refs.pop(0)
        if pool is not None:
            xm_buf = refs.pop(0)
    assert not refs
    t = pl.program_id(1)
    last = pl.num_programs(1) - 1
    part = tl // nsplit
    if n_cast:
        _run_cast_riders(pl.num_programs(1), list(zip(cast_in, cast_out)))

    def front(src_ref, tile, r0, anchor=None):
        zero = 0.0 if anchor is None else _dep_zero(anchor)
        x = src_ref[r0:r0 + part, :]
        if pool is not None:
            pos0 = start + ((tile * tl + r0) >> _log2(ns))
            x = _pool_mix(x, gm_ref[...] + zero, pext_ref, ppad, r0, ns, pos0, wp_ref, sc_ref[...])
        return x, _rms(x, gn_ref[...] + zero).astype(BF16)

    def front_tile(src_ref, tile, slot, anchors=None):
        for s in range(nsplit):
            rows = slice(s * part, (s + 1) * part)
            x, hb = front(src_ref, tile, s * part, None if anchors is None else anchors[s])
            hb_buf[slot, rows, :] = hb
            if pool is not None:
                xm_buf[slot, rows, :] = x
        if pool is not None:
            pext_ref[0:ppad, :] = pext_ref[tl:tl + ppad, :]

    @pl.when(t == 0)
    def _():
        ext_ref[0:padr, :] = st_ref[0]
        if pool is not None:
            pext_ref[0:ppad, :] = pst_ref[0]
        if lookahead:
            front_tile(x_ref, 0, 0)

    cw = cw_ref[...]
    cb = cb_ref[...]
    slot = t % 2
    anchors = []
    for s in range(nsplit):
        r0 = s * part
        rows = slice(r0, r0 + part)
        if lookahead:
            hb = hb_buf[slot, rows, :]
            x = xm_buf[slot, rows, :] if pool is not None else x_ref[rows, :]
        else:
            x, hb = front(x_ref, t, r0)
        g = _dot(hb, wgu_ref[:, 0:D_FF])
        anchors.append(g[0:1, 0:D_MODEL])
        ext_ref[padr + r0:padr + r0 + part, :] = g
        gc = (_shifted(ext_ref, padr, part, 2 * ns, row0=r0) * cw[0:1, :]
              + _shifted(ext_ref, padr, part, ns, row0=r0) * cw[1:2, :]
              + g * cw[2:3, :] + cb)
        up = _dot(hb, wgu_ref[:, D_FF:2 * D_FF])
        act = (_silu(gc) * up).astype(BF16)
        xo = x + _dot(act, wd_ref[...])
        if final:
            xo = _rms(xo, gf_ref[...])
        xo_ref[rows, :] = xo

    if lookahead:
        front_tile(xn_ref, jnp.minimum(t + 1, last), 1 - slot, anchors)
    elif pool is not None:
        pext_ref[0:ppad, :] = pext_ref[tl:tl + ppad, :]
    carry = ext_ref[tl:tl + padr, :]
    ext_ref[0:padr, :] = carry

    @pl.when(t == last)
    def _():
        sto_ref[0] = carry
        if pool is not None:
            psto_ref[0] = pext_ref[0:ppad, :]


def _ffn(x2d, state, p, layer, wgu, wd, *, tl, ns, final, nsplit=1, pool_state=None, start=0, cast=(),
         lookahead=False):
    n_groups, padr, _ = state.shape
    rows_per_group = x2d.shape[0] // n_groups
    nt = rows_per_group // tl
    assert padr % SUBLANES == 0 and padr >= CONV_HIST * ns and (nt == 1 or tl >= padr)
    assert not cast or n_groups * nt >= CAST_STEPS
    row_spec = pl.BlockSpec((tl, D_MODEL), lambda n, t: (n * nt + t, 0))
    st_spec = pl.BlockSpec((1, padr, D_FF), lambda n, t: (n, 0, 0))
    ins = [x2d, state]
    in_specs = [row_spec, st_spec]
    if lookahead:
        assert nt > 1
        ins.append(x2d)
        in_specs.append(pl.BlockSpec((tl, D_MODEL), lambda n, t: (n * nt + jnp.minimum(t + 1, nt - 1), 0)))
        if pool_state is not None:
            in_specs[0] = pl.BlockSpec((tl, D_MODEL), lambda n, t: (n * nt, 0))
    out_specs = [row_spec, st_spec]
    out_shape = [jax.ShapeDtypeStruct(x2d.shape, F32), jax.ShapeDtypeStruct(state.shape, F32)]
    scratch = [pltpu.VMEM((padr + tl, D_FF), F32)]
    pool = None
    if pool_state is not None:
        ppad = pool_state.shape[1]
        assert ppad % SUBLANES == 0 and ppad >= POOL_BUF * ns and (nt == 1 or tl >= ppad)
        pool = (ppad, start)
        pst_spec = pl.BlockSpec((1, ppad, D_MODEL), lambda n, t: (n, 0, 0))
        pool_consts = (p["norm_mix"][layer], p["pool_w"], p["pool_scale"])
        ins += [pool_state, *pool_consts]
        in_specs += [pst_spec] + [_const_spec(a.shape) for a in pool_consts]
        out_specs.append(pst_spec)
        out_shape.append(jax.ShapeDtypeStruct(pool_state.shape, F32))
        scratch.append(pltpu.VMEM((ppad + tl, D_MODEL), F32))
    ins += [p["norm_ffn"], wgu, p["ffn_cw"], p["ffn_cb"], wd, p["norm_final"]]
    in_specs += [_layer_spec(p["norm_ffn"].shape, layer), _const_spec(wgu.shape),
                 _layer_spec(p["ffn_cw"].shape, layer), _layer_spec(p["ffn_cb"].shape, layer),
                 _const_spec(wd.shape), _const_spec(p["norm_final"].shape)]
    riders = [_cast_rider(w, l, nt) for w, l in cast]
    ins += [w for w, _ in cast]
    in_specs += [r[0] for r in riders]
    out_specs += [r[1] for r in riders]
    out_shape += [r[2] for r in riders]
    if lookahead:
        scratch.append(pltpu.VMEM((2, tl, D_MODEL), BF16))
        if pool is not None:
            scratch.append(pltpu.VMEM((2, tl, D_MODEL), F32))
    kern = functools.partial(_ffn_kernel, tl=tl, ns=ns, padr=padr, final=final, pool=pool,
                             nsplit=nsplit, n_cast=len(cast), lookahead=lookahead)
    return pl.pallas_call(
        kern,
        grid=(n_groups, nt),
        in_specs=in_specs,
        out_specs=out_specs,
        out_shape=out_shape,
        scratch_shapes=scratch,
        compiler_params=pltpu.CompilerParams(
            dimension_semantics=("arbitrary", "arbitrary"), vmem_limit_bytes=VMEM_LIMIT),
        name=("pool_" if pool else "") + ("ffn_final" if final else "ffn"),
    )(*ins)


def _pool_kernel(x_ref, st_ref, gn_ref, wp_ref, sc_ref, xo_ref, sto_ref, ext_ref,
                 *, tl, ns, padr, start):
    t = pl.program_id(1)

    @pl.when(t == 0)
    def _():
        ext_ref[0:padr, :] = st_ref[0]

    pos0 = start + ((t * tl) >> _log2(ns))
    xo_ref[...] = _pool_mix(x_ref[...], gn_ref[...], ext_ref, padr, 0, ns, pos0, wp_ref, sc_ref[...])

    carry = ext_ref[tl:tl + padr, :]
    ext_ref[0:padr, :] = carry

    @pl.when(t == pl.num_programs(1) - 1)
    def _():
        sto_ref[0] = carry


def _pool(x2d, state, gn, wp, sc, *, tl, ns, start):
    n_groups, padr, _ = state.shape
    rows_per_group = x2d.shape[0] // n_groups
    nt = rows_per_group // tl
    assert padr % SUBLANES == 0 and padr >= POOL_BUF * ns and (nt == 1 or tl >= padr)
    kern = functools.partial(_pool_kernel, tl=tl, ns=ns, padr=padr, start=start)
    row_spec = pl.BlockSpec((tl, D_MODEL), lambda n, t: (n * nt + t, 0))
    st_spec = pl.BlockSpec((1, padr, D_MODEL), lambda n, t: (n, 0, 0))
    return pl.pallas_call(
        kern,
        grid=(n_groups, nt),
        in_specs=[row_spec, st_spec, _const_spec(gn.shape), _const_spec(wp.shape),
                  _const_spec(sc.shape)],
        out_specs=[row_spec, st_spec],
        out_shape=[jax.ShapeDtypeStruct(x2d.shape, F32),
                   jax.ShapeDtypeStruct(state.shape, F32)],
        scratch_shapes=[pltpu.VMEM((padr + tl, D_MODEL), F32)],
        compiler_params=pltpu.CompilerParams(
            dimension_semantics=("arbitrary", "arbitrary"), vmem_limit_bytes=VMEM_LIMIT),
        name="pool",
    )(x2d, state, gn, wp, sc)


def kernel(x_prompt, x_sample, state_conv_a, state_hgrn, state_pool, state_ffn, norm_mix, norm_ffn,
           norm_final, w_in, conv_a_w, hgrn_lower_bounds, hgrn_norm, w_out, pool_w, pool_scale,
           ffn_w_gu, ffn_conv_w, ffn_conv_b, ffn_w_down):
    batch, seq, _ = x_prompt.shape
    dec_batch, dec_seq, _ = x_sample.shape

    p = dict(
        norm_mix=norm_mix[:, None, :], norm_ffn=norm_ffn[:, None, :], norm_final=norm_final[None, :],
        ffn_cw=ffn_conv_w, ffn_cb=ffn_conv_b[:, None, :], pool_w=pool_w[0].astype(BF16),
        pool_scale=pool_scale[0][None, :])
    win = w_in[0].astype(BF16)
    wout = w_out[0].astype(BF16)
    mix_args = (p["norm_mix"][0], win, conv_a_w[0], hgrn_lower_bounds, hgrn_norm[0][None, :], wout)

    xp = x_prompt.reshape(batch * seq, D_MODEL)
    x1, conv_p, hgrn_p, wgu0, wd0 = _mixer0_prompt(
        xp, batch, seq, *mix_args, [(ffn_w_gu, 0), (ffn_w_down, 0)], tl=512, nsplit=2)
    pad_ffn = _round_up(CONV_HIST, SUBLANES)
    pad_pool = _round_up(POOL_BUF, SUBLANES)
    zf = jnp.zeros((batch, pad_ffn, D_FF), F32)
    zp = jnp.zeros((batch, pad_pool, D_MODEL), F32)
    x2, ffn0_p, wgu1, wd1 = _ffn(x1, zf, p, 0, wgu0, wd0, tl=512, ns=1, final=False, nsplit=2,
                                 cast=[(ffn_w_gu, 1), (ffn_w_down, 1)])
    yp, ffn1_p, pool_p = _ffn(x2, zf, p, 1, wgu1, wd1, tl=512, ns=1, final=True, nsplit=2,
                              pool_state=zp, start=0, lookahead=True)
    y_prompt = yp.reshape(batch, seq, D_MODEL)
    conv_prompt = conv_p[None, :, SUBLANES - CONV_HIST:, :]
    hgrn_prompt = hgrn_p[None]
    pool_prompt = pool_p[None, :, pad_pool - POOL_BUF:, :]
    ffn_prompt = jnp.stack([ffn0_p, ffn1_p])[:, :, pad_ffn - CONV_HIST:, :]

    xs = x_sample.reshape(dec_batch * dec_seq, D_MODEL)
    cs = state_conv_a[0]
    zeros_tail = jnp.zeros((dec_batch, dec_seq - 1, D_A), F32)
    e1 = jnp.concatenate([cs[:, 1:2], zeros_tail], axis=1).reshape(dec_batch * dec_seq, D_A)
    e2 = jnp.concatenate([cs, zeros_tail[:, 1:]], axis=1).reshape(dec_batch * dec_seq, D_A)
    x1s, us, hgrn_s = _mixer0_sample(xs, e1, e2, state_hgrn[0], *mix_args, seq_len=dec_seq, bn=16)
    conv_sample = us.reshape(dec_batch, dec_seq, D_A)[None, :, dec_seq - CONV_HIST:, :]

    def to_tm(a):
        return a.transpose(1, 0, 2).reshape(1, a.shape[1] * dec_batch, a.shape[2])

    def from_tm(a, rows):
        return a.reshape(rows, dec_batch, a.shape[-1]).transpose(1, 0, 2)

    x1t = to_tm(x1s.reshape(dec_batch, dec_seq, D_MODEL))[0]
    x2t, ffn0_s = _ffn(x1t, to_tm(state_ffn[0]), p, 0, wgu0, wd0, tl=512, ns=dec_batch, final=False,
                       nsplit=2)
    x3t, pool_s = _pool(x2t, to_tm(state_pool[0]), p["norm_mix"][1], p["pool_w"], p["pool_scale"],
                        tl=dec_batch * dec_seq, ns=dec_batch, start=PAST_LEN)
    ys, ffn1_s = _ffn(x3t, to_tm(state_ffn[1]), p, 1, wgu1, wd1, tl=512, ns=dec_batch, final=True,
                      nsplit=2)
    y_sample = from_tm(ys, dec_seq)
    pool_sample = from_tm(pool_s[0], POOL_BUF)[None]
    ffn_sample = jnp.stack([from_tm(ffn0_s[0], CONV_HIST), from_tm(ffn1_s[0], CONV_HIST)])

    return (y_prompt, y_sample, conv_prompt, conv_sample, hgrn_prompt, hgrn_s[None],
            pool_prompt, pool_sample, ffn_prompt, ffn_sample)
```

```python
import functools

import jax
import jax.numpy as jnp
import numpy as np
from jax import lax
from jax.experimental import pallas as pl
from jax.experimental.pallas import tpu as pltpu

D_MODEL = 1024
D_A = 512
N_HEADS = 4
HEAD = 128
D_B = N_HEADS * HEAD
D_IN = 3 * D_A + 4 * D_B
D_FF = 2816
POOL_WINDOWS = (2, 4, 8, 16)
POOL_GROUP = D_MODEL // len(POOL_WINDOWS)
POOL_BUF = max(POOL_WINDOWS) - 1
CONV_HIST = 2
PAST_LEN = 16384
EPS = 1e-6
LOG2E = 1.4426950408889634

SUBLANES = 8
HGRN_CHUNK = 128
VMEM_LIMIT = 56 * 1024 * 1024
CAST_STEPS = 16

F32 = jnp.float32
BF16 = jnp.bfloat16

_OFF_AC, _OFF_AB, _OFF_AV = 0, D_A, 2 * D_A
_OFF_Q, _OFF_F, _OFF_I, _OFF_G = (3 * D_A, 3 * D_A + D_B, 3 * D_A + 2 * D_B, 3 * D_A + 3 * D_B)
_HEADS = [slice(h * HEAD, (h + 1) * HEAD) for h in range(N_HEADS)]


def _round_up(x, m):
    return (x + m - 1) // m * m


def _log2(x):
    assert x & (x - 1) == 0
    return x.bit_length() - 1


def _rms(x, g):
    return x * lax.rsqrt(jnp.mean(x * x, axis=-1, keepdims=True) + EPS) * g


def _silu(x):
    return x * jax.nn.sigmoid(x)


def _dot(a, b):
    return jnp.dot(a, b, preferred_element_type=F32)


def _dot_nt(a, b):
    return lax.dot_general(a, b, (((1,), (1,)), ((), ())), preferred_element_type=F32)


def _dot_tn(a, b):
    return lax.dot_general(a, b, (((0,), (0,)), ((), ())), preferred_element_type=F32)


def _dep_zero(a):
    u = lax.bitcast_convert_type(a, jnp.int32)
    return lax.shift_right_logical(lax.shift_right_logical(u, 16), 16).astype(F32)


def _shifted(ext_ref, padr, n, k_rows, cols=slice(None), row0=0):
    off = padr + row0 - k_rows
    assert off >= 0
    base = off // SUBLANES * SUBLANES
    rem = off - base
    if rem == 0:
        return ext_ref[pl.ds(off, n), cols]
    win = ext_ref[pl.ds(base, n + SUBLANES), cols]
    return pltpu.roll(win, SUBLANES - rem, 0)[SUBLANES:SUBLANES + n, :]


def _levels(seg):
    return [1 << i for i in range(_log2(seg))]


def _selector_levels(seg):
    return [b for b in _levels(seg) if 1 < b < SUBLANES]


def _selector(c, seg):
    t = np.arange(c)[:, None]
    u = np.arange(c)[None, :]
    same_seg = (t // seg) == (u // seg)
    blocks = [same_seg & (u <= t)]
    blocks.append(same_seg & (u > t) if seg < c else np.ones((SUBLANES, c), bool))
    for b in _selector_levels(seg):
        pair = (t // (2 * b)) == (u // (2 * b))
        anchor = (t // (2 * b)) * 2 * b + b - 1
        right = (t // b) % 2 == 1
        blocks.append(pair & np.where(right, (u > anchor) & (u <= t), (u > t) & (u <= anchor)))
    sel = np.concatenate(blocks, axis=0).astype(np.float32)
    return np.concatenate([sel, sel], axis=1)


def _split_bf16(a):
    hi = a.astype(BF16)
    return hi, (a - hi.astype(F32)).astype(BF16)


def _hgrn_intra(q, kk, f, lf2, sel_ref, g_ref, seg):
    c = q.shape[0]
    x = _dot(sel_ref[...], jnp.concatenate(_split_bf16(lf2), axis=0))
    g = x[0:c]

    row = lax.broadcasted_iota(jnp.int32, (c, 1), 0)
    rowi = lax.broadcasted_iota(jnp.int32, (c, c), 0)
    coli = lax.broadcasted_iota(jnp.int32, (c, c), 1)
    diff = rowi ^ coli
    lower = rowi > coli

    p0 = q * kk
    p1 = q * pltpu.roll(kk, 1, 0) * f
    amat = []
    for sl in _HEADS:
        a0 = jnp.sum(p0[:, sl], axis=-1, keepdims=True)
        a1 = jnp.sum(p1[:, sl], axis=-1, keepdims=True)
        amat.append(jnp.where(rowi == coli, a0, jnp.where(lower & (diff == 1), a1, 0.0)))

    sel_blocks = {b: x[off:off + c] for b, off in
                  zip(_selector_levels(seg), range(c + (c if seg < c else SUBLANES), x.shape[0], c))}
    if seg > SUBLANES:
        g_ref[...] = g
    for b in _levels(seg)[1:]:
        if b in sel_blocks:
            ex = sel_blocks[b]
        else:
            parts = []
            for pair in range(c // (2 * b)):
                anchor = g_ref[pl.ds(pair * 2 * b + b - 1, 1), :]
                parts.append(g[pair * 2 * b:(pair + 1) * 2 * b, :] - anchor)
            ex = -jnp.abs(parts[0] if len(parts) == 1 else jnp.concatenate(parts, axis=0))
        w = jnp.exp2(ex)
        right = ((row >> _log2(b)) & 1) == 1
        ql = jnp.where(right, q * w, 0.0).astype(BF16)
        kl = jnp.where(right, 0.0, kk * w).astype(BF16)
        mask = lower & ((diff >> _log2(b)) == 1)
        for h, sl in enumerate(_HEADS):
            amat[h] = jnp.where(mask, _dot_nt(ql[:, sl], kl[:, sl]), amat[h])
    return amat, q * jnp.exp2(g), x


def _gates(z_ref, rows, lb):
    f = lb + (1.0 - lb) * jax.nn.sigmoid(z_ref[rows, _OFF_F:_OFF_F + D_B])
    return f, 1.0 - f, jnp.log(f) * LOG2E


def _lower_bound(lbp_ref):
    lbp = lbp_ref[...]
    lbe = jnp.exp(lbp - jnp.max(lbp, axis=0, keepdims=True))
    return lbe[0:1, :] / jnp.sum(lbe, axis=0, keepdims=True)


def _mixer_b_out(z_ref, o_ref, y_ref, hn, rows):
    for h, sl in enumerate(_HEADS):
        on = _rms(o_ref[rows, sl], hn[:, sl])
        gate = z_ref[rows, _OFF_G + h * HEAD:_OFF_G + (h + 1) * HEAD]
        y_ref[rows, D_A + h * HEAD:D_A + (h + 1) * HEAD] = (on * _silu(gate)).astype(BF16)


def _cast_rider(w, layer, nt):
    _, r, c = w.shape
    rows = r // CAST_STEPS
    assert rows * CAST_STEPS == r and rows % (2 * SUBLANES) == 0

    def blk(n, t):
        return jnp.minimum(n * nt + t, CAST_STEPS - 1)

    return (pl.BlockSpec((None, rows, c), lambda n, t: (layer, blk(n, t), 0)),
            pl.BlockSpec((rows, c), lambda n, t: (blk(n, t), 0)),
            jax.ShapeDtypeStruct((r, c), BF16))


def _run_cast_riders(nt, pairs):
    @pl.when(pl.program_id(0) * nt + pl.program_id(1) < CAST_STEPS)
    def _():
        for src, dst in pairs:
            dst[...] = src[...].astype(BF16)


def _mixer0_prompt_kernel(x_ref, gn_ref, win_ref, cw_ref, lbp_ref, hn_ref, wout_ref, sel_ref,
                          c0_ref, c1_ref,
                          xo_ref, convo_ref, so_ref, c0o_ref, c1o_ref,
                          hb_ref, z_ref, o_ref, y_ref, ext_ref, st_ref, g_ref, *, tl, chunk, nsplit):
    t = pl.program_id(1)
    _run_cast_riders(pl.num_programs(1), [(c0_ref, c0o_ref), (c1_ref, c1o_ref)])

    @pl.when(t == 0)
    def _():
        ext_ref[0:SUBLANES, :] = jnp.zeros((SUBLANES, D_A), F32)
        st_ref[...] = jnp.zeros((N_HEADS, HEAD, HEAD), F32)

    hb_ref[...] = _rms(x_ref[...], gn_ref[...]).astype(BF16)
    lb = _lower_bound(lbp_ref)
    cw = cw_ref[...]
    hn = hn_ref[...]
    part = tl // nsplit

    for s in range(nsplit):
        r0 = s * part
        rows = slice(r0, r0 + part)
        hb = hb_ref[rows, :]
        z_ref[rows, _OFF_Q:_OFF_G] = _dot(hb, win_ref[:, _OFF_Q:_OFF_G])

        for c0 in range(r0, r0 + part, chunk):
            crows = slice(c0, c0 + chunk)
            f, kk, lf2 = _gates(z_ref, crows, lb)
            amat, qg, x = _hgrn_intra(z_ref[crows, _OFF_Q:_OFF_Q + D_B], kk, f, lf2, sel_ref,
                                      g_ref.at[c0 // chunk], chunk)
            g_last = x[chunk:chunk + 1]
            dec = jnp.exp2(g_last)
            kg = (kk * jnp.exp2(g_last - x[0:chunk])).astype(BF16)
            qg = qg.astype(BF16)
            vb = z_ref[crows, _OFF_I:_OFF_I + D_B].astype(BF16)
            for h, sl in enumerate(_HEADS):
                st = st_ref[h]
                lhs = jnp.concatenate([qg[:, sl], amat[h].astype(BF16)], axis=1)
                rhs = jnp.concatenate([st.T.astype(BF16), vb[:, sl]], axis=0)
                o_ref[crows, sl] = _dot(lhs, rhs)
                st_ref[h] = dec[:, sl] * st + _dot_tn(vb[:, sl], kg[:, sl])

        z_ref[rows, 0:_OFF_Q] = _dot(hb, win_ref[:, 0:_OFF_Q])
        z_ref[rows, _OFF_G:D_IN] = _dot(hb, win_ref[:, _OFF_G:D_IN])

        u = z_ref[rows, _OFF_AC:_OFF_AC + D_A] * z_ref[rows, _OFF_AV:_OFF_AV + D_A]
        ext_ref[SUBLANES + r0:SUBLANES + r0 + part, :] = u
        conv = (_shifted(ext_ref, SUBLANES, part, 2, row0=r0) * cw[0:1, :]
                + _shifted(ext_ref, SUBLANES, part, 1, row0=r0) * cw[1:2, :] + u * cw[2:3, :])
        y_ref[rows, 0:D_A] = (z_ref[rows, _OFF_AB:_OFF_AB + D_A] * conv).astype(BF16)
        _mixer_b_out(z_ref, o_ref, y_ref, hn, rows)
        xo_ref[rows, :] = x_ref[rows, :] + _dot(y_ref[rows, :], wout_ref[...])

    carry = ext_ref[tl:tl + SUBLANES, :]
    ext_ref[0:SUBLANES, :] = carry

    @pl.when(t == pl.num_programs(1) - 1)
    def _():
        convo_ref[0] = carry
        for h in range(N_HEADS):
            so_ref[0, h] = st_ref[h].T


def _mixer0_sample_kernel(x_ref, e1_ref, e2_ref, s0_ref, gn_ref, win_ref, cw_ref, lbp_ref, hn_ref,
                          wout_ref, sel_ref,
                          xo_ref, uo_ref, so_ref,
                          z_ref, o_ref, y_ref, lf_ref, qg_ref, kg_ref, *, tl, seq):
    x = x_ref[...]
    z_ref[...] = _dot(_rms(x, gn_ref[...]).astype(BF16), win_ref[...])

    u = z_ref[:, _OFF_AC:_OFF_AC + D_A] * z_ref[:, _OFF_AV:_OFF_AV + D_A]
    cw = cw_ref[...]
    tpos = lax.broadcasted_iota(jnp.int32, (tl, 1), 0) & (seq - 1)
    u1 = jnp.where(tpos >= 1, pltpu.roll(u, 1, 0), e1_ref[...])
    u2 = jnp.where(tpos >= 2, pltpu.roll(u, 2, 0), e2_ref[...])
    uo_ref[...] = u
    conv = u2 * cw[0:1, :] + u1 * cw[1:2, :] + u * cw[2:3, :]
    y_ref[:, 0:D_A] = (z_ref[:, _OFF_AB:_OFF_AB + D_A] * conv).astype(BF16)

    f, kk, lf2 = _gates(z_ref, slice(None), _lower_bound(lbp_ref))
    amat, qg, xsel = _hgrn_intra(z_ref[:, _OFF_Q:_OFF_Q + D_B], kk, f, lf2, sel_ref, None, seq)
    v_all = z_ref[:, _OFF_I:_OFF_I + D_B].astype(BF16)
    o_ref[...] = jnp.concatenate(
        [_dot(amat[h].astype(BF16), v_all[:, sl]) for h, sl in enumerate(_HEADS)], axis=1)
    qg_ref[...] = qg
    kg_ref[...] = kk * jnp.exp2(xsel[tl:2 * tl])
    lf_ref[...] = lf2
    ones = jnp.ones((seq, HEAD), BF16)

    def seq_body(i, carry_unused):
        rows = pl.ds(pl.multiple_of(i * seq, seq), seq)
        hi, mid = _split_bf16(lf_ref[rows, :])
        qg_i = qg_ref[rows, :].astype(BF16)
        kg_i = kg_ref[rows, :].astype(BF16)
        vb = z_ref[rows, _OFF_I:_OFF_I + D_B].astype(BF16)
        for h, sl in enumerate(_HEADS):
            s0 = s0_ref[i, h]
            o_ref[rows, sl] = o_ref[rows, sl] + _dot(qg_i[:, sl], s0.astype(BF16))
            g_last = _dot_tn(hi[:, sl], ones) + _dot_tn(mid[:, sl], ones)
            so_ref[i, h] = jnp.exp2(g_last) * s0 + _dot_tn(kg_i[:, sl], vb[:, sl])
        return carry_unused

    lax.fori_loop(0, tl // seq, seq_body, 0, unroll=4)

    _mixer_b_out(z_ref, o_ref, y_ref, hn_ref[...], slice(None))
    xo_ref[...] = x + _dot(y_ref[...], wout_ref[...])


def _const_spec(shape):
    nd = len(shape)
    return pl.BlockSpec(shape, lambda *_: (0,) * nd, pipeline_mode=pl.Buffered(1))


def _layer_spec(shape, layer):
    nd = len(shape) - 1
    return pl.BlockSpec((None,) + tuple(shape[1:]), lambda *_: (layer,) + (0,) * nd,
                        pipeline_mode=pl.Buffered(1))


def _mixer0_prompt(x2d, n_seq, seq_len, gn, win, cw, lbp, hn, wout, cast, *, tl, nsplit):
    nt = seq_len // tl
    assert n_seq * nt >= CAST_STEPS
    sel = jnp.asarray(_selector(HGRN_CHUNK, HGRN_CHUNK), BF16)
    kern = functools.partial(_mixer0_prompt_kernel, tl=tl, chunk=HGRN_CHUNK, nsplit=nsplit)
    row_spec = pl.BlockSpec((tl, D_MODEL), lambda n, t: (n * nt + t, 0))
    consts = (gn, win, cw, lbp, hn, wout, sel)
    riders = [_cast_rider(w, layer, nt) for w, layer in cast]
    return pl.pallas_call(
        kern,
        grid=(n_seq, nt),
        in_specs=[row_spec] + [_const_spec(a.shape) for a in consts] + [r[0] for r in riders],
        out_specs=[row_spec,
                   pl.BlockSpec((1, SUBLANES, D_A), lambda n, t: (n, 0, 0)),
                   pl.BlockSpec((1, N_HEADS, HEAD, HEAD), lambda n, t: (n, 0, 0, 0))]
                  + [r[1] for r in riders],
        out_shape=[jax.ShapeDtypeStruct(x2d.shape, F32),
                   jax.ShapeDtypeStruct((n_seq, SUBLANES, D_A), F32),
                   jax.ShapeDtypeStruct((n_seq, N_HEADS, HEAD, HEAD), F32)]
                  + [r[2] for r in riders],
        scratch_shapes=[pltpu.VMEM((tl, D_MODEL), BF16),
                        pltpu.VMEM((tl, D_IN), F32),
                        pltpu.VMEM((tl, D_B), F32),
                        pltpu.VMEM((tl, D_A + D_B), BF16),
                        pltpu.VMEM((SUBLANES + tl, D_A), F32),
                        pltpu.VMEM((N_HEADS, HEAD, HEAD), F32),
                        pltpu.VMEM((tl // HGRN_CHUNK, HGRN_CHUNK, D_B), F32)],
        compiler_params=pltpu.CompilerParams(
            dimension_semantics=("arbitrary", "arbitrary"), vmem_limit_bytes=VMEM_LIMIT),
        name="mixer0_prompt",
    )(x2d, *consts, *[w for w, _ in cast])


def _mixer0_sample(x2d, e1, e2, s0, gn, win, cw, lbp, hn, wout, *, seq_len, bn):
    n_seq = s0.shape[0]
    tl = bn * seq_len
    sel = jnp.asarray(_selector(tl, seq_len), BF16)
    kern = functools.partial(_mixer0_sample_kernel, tl=tl, seq=seq_len)
    row_spec = pl.BlockSpec((tl, D_MODEL), lambda i: (i, 0))
    conv_spec = pl.BlockSpec((tl, D_A), lambda i: (i, 0))
    st_spec = pl.BlockSpec((bn, N_HEADS, HEAD, HEAD), lambda i: (i, 0, 0, 0))
    consts = (gn, win, cw, lbp, hn, wout, sel)
    return pl.pallas_call(
        kern,
        grid=(n_seq // bn,),
        in_specs=[row_spec, conv_spec, conv_spec, st_spec] + [_const_spec(a.shape) for a in consts],
        out_specs=[row_spec, conv_spec, st_spec],
        out_shape=[jax.ShapeDtypeStruct(x2d.shape, F32),
                   jax.ShapeDtypeStruct((x2d.shape[0], D_A), F32),
                   jax.ShapeDtypeStruct(s0.shape, F32)],
        scratch_shapes=[pltpu.VMEM((tl, D_IN), F32),
                        pltpu.VMEM((tl, D_B), F32),
                        pltpu.VMEM((tl, D_A + D_B), BF16),
                        pltpu.VMEM((tl, D_B), F32),
                        pltpu.VMEM((tl, D_B), F32),
                        pltpu.VMEM((tl, D_B), F32)],
        compiler_params=pltpu.CompilerParams(
            dimension_semantics=("arbitrary",), vmem_limit_bytes=VMEM_LIMIT),
        name="mixer0_sample",
    )(x2d, e1, e2, s0, *consts)


def _pool_mix(x, gm, ext_ref, padr, row0, ns, pos0, wp_ref, sc):
    n = x.shape[0]
    h = _rms(x, gm)
    ext_ref[padr + row0:padr + row0 + n, :] = h
    pos = pos0 + (lax.broadcasted_iota(jnp.int32, (n, 1), 0) >> _log2(ns))
    ys = []
    for gi, w in enumerate(POOL_WINDOWS):
        cols = slice(gi * POOL_GROUP, (gi + 1) * POOL_GROUP)
        hg = h[:, cols]
        if ns == 1:
            hist = _round_up(POOL_BUF, SUBLANES)
            s = ext_ref[pl.ds(padr + row0 - hist, n + hist), cols]
            sh = 1
            while sh < w:
                s = s + pltpu.roll(s, sh, 0)
                sh *= 2
            acc = s[hist:hist + n, :]
        else:
            acc = hg
            for k in range(1, w):
                acc = acc + _shifted(ext_ref, padr, n, k * ns, cols, row0)
        cnt = jnp.minimum(w, pos + 1).astype(F32)
        ys.append(_dot((acc / cnt - hg).astype(BF16), wp_ref[gi]))
    return x + jnp.concatenate(ys, axis=1) * sc


def _ffn_kernel(*refs, tl, ns, padr, final, pool, nsplit, n_cast, lookahead):
    refs = list(refs)
    x_ref, st_ref = refs[:2]
    del refs[:2]
    if lookahead:
        xn_ref = refs.pop(0)
    if pool is not None:
        pst_ref, gm_ref, wp_ref, sc_ref = refs[:4]
        del refs[:4]
        ppad, start = pool
    gn_ref, wgu_ref, cw_ref, cb_ref, wd_ref, gf_ref = refs[:6]
    del refs[:6]
    cast_in = refs[:n_cast]
    del refs[:n_cast]
    xo_ref, sto_ref = refs[:2]
    del refs[:2]
    if pool is not None:
        psto_ref = refs.pop(0)
    cast_out = refs[:n_cast]
    del refs[:n_cast]
    ext_ref = refs.pop(0)
    if pool is not None:
        pext_ref = refs.pop(0)
    if lookahead:
        hb_buf = refs.pop(0)
        if pool is not None:
            xm_buf = refs.pop(0)
    assert not refs
    t = pl.program_id(1)
    last = pl.num_programs(1) - 1
    part = tl // nsplit
    if n_cast:
        _run_cast_riders(pl.num_programs(1), list(zip(cast_in, cast_out)))

    def front(src_ref, tile, r0, anchor=None):
        zero = 0.0 if anchor is None else _dep_zero(anchor)
        x = src_ref[r0:r0 + part, :]
        if pool is not None:
            pos0 = start + ((tile * tl + r0) >> _log2(ns))
            x = _pool_mix(x, gm_ref[...] + zero, pext_ref, ppad, r0, ns, pos0, wp_ref, sc_ref[...])
        return x, _rms(x, gn_ref[...] + zero).astype(BF16)

    def front_tile(src_ref, tile, slot, anchors=None):
        for s in range(nsplit):
            rows = slice(s * part, (s + 1) * part)
            x, hb = front(src_ref, tile, s * part, None if anchors is None else anchors[s])
            hb_buf[slot, rows, :] = hb
            if pool is not None:
                xm_buf[slot, rows, :] = x
        if pool is not None:
            pext_ref[0:ppad, :] = pext_ref[tl:tl + ppad, :]

    @pl.when(t == 0)
    def _():
        ext_ref[0:padr, :] = st_ref[0]
        if pool is not None:
            pext_ref[0:ppad, :] = pst_ref[0]
        if lookahead:
            front_tile(x_ref, 0, 0)

    cw = cw_ref[...]
    cb = cb_ref[...]
    slot = t % 2
    anchors = []
    for s in range(nsplit):
        r0 = s * part
        rows = slice(r0, r0 + part)
        if lookahead:
            hb = hb_buf[slot, rows, :]
            x = xm_buf[slot, rows, :] if pool is not None else x_ref[rows, :]
        else:
            x, hb = front(x_ref, t, r0)
        g = _dot(hb, wgu_ref[:, 0:D_FF])
        anchors.append(g[0:1, 0:D_MODEL])
        ext_ref[padr + r0:padr + r0 + part, :] = g
        gc = (_shifted(ext_ref, padr, part, 2 * ns, row0=r0) * cw[0:1, :]
              + _shifted(ext_ref, padr, part, ns, row0=r0) * cw[1:2, :]
              + g * cw[2:3, :] + cb)
        up = _dot(hb, wgu_ref[:, D_FF:2 * D_FF])
        act = (_silu(gc) * up).astype(BF16)
        xo = x + _dot(act, wd_ref[...])
        if final:
            xo = _rms(xo, gf_ref[...])
        xo_ref[rows, :] = xo

    if lookahead:
        front_tile(xn_ref, jnp.minimum(t + 1, last), 1 - slot, anchors)
    elif pool is not None:
        pext_ref[0:ppad, :] = pext_ref[tl:tl + ppad, :]
    carry = ext_ref[tl:tl + padr, :]
    ext_ref[0:padr, :] = carry

    @pl.when(t == last)
    def _():
        sto_ref[0] = carry
        if pool is not None:
            psto_ref[0] = pext_ref[0:ppad, :]


def _ffn(x2d, state, p, layer, wgu, wd, *, tl, ns, final, nsplit=1, pool_state=None, start=0, cast=(),
         lookahead=False):
    n_groups, padr, _ = state.shape
    rows_per_group = x2d.shape[0] // n_groups
    nt = rows_per_group // tl
    assert padr % SUBLANES == 0 and padr >= CONV_HIST * ns and (nt == 1 or tl >= padr)
    assert not cast or n_groups * nt >= CAST_STEPS
    row_spec = pl.BlockSpec((tl, D_MODEL), lambda n, t: (n * nt + t, 0))
    st_spec = pl.BlockSpec((1, padr, D_FF), lambda n, t: (n, 0, 0))
    ins = [x2d, state]
    in_specs = [row_spec, st_spec]
    if lookahead:
        assert nt > 1
        ins.append(x2d)
        in_specs.append(pl.BlockSpec((tl, D_MODEL), lambda n, t: (n * nt + jnp.minimum(t + 1, nt - 1), 0)))
        if pool_state is not None:
            in_specs[0] = pl.BlockSpec((tl, D_MODEL), lambda n, t: (n * nt, 0))
    out_specs = [row_spec, st_spec]
    out_shape = [jax.ShapeDtypeStruct(x2d.shape, F32), jax.ShapeDtypeStruct(state.shape, F32)]
    scratch = [pltpu.VMEM((padr + tl, D_FF), F32)]
    pool = None
    if pool_state is not None:
        ppad = pool_state.shape[1]
        assert ppad % SUBLANES == 0 and ppad >= POOL_BUF * ns and (nt == 1 or tl >= ppad)
        pool = (ppad, start)
        pst_spec = pl.BlockSpec((1, ppad, D_MODEL), lambda n, t: (n, 0, 0))
        pool_consts = (p["norm_mix"][layer], p["pool_w"], p["pool_scale"])
        ins += [pool_state, *pool_consts]
        in_specs += [pst_spec] + [_const_spec(a.shape) for a in pool_consts]
        out_specs.append(pst_spec)
        out_shape.append(jax.ShapeDtypeStruct(pool_state.shape, F32))
        scratch.append(pltpu.VMEM((ppad + tl, D_MODEL), F32))
    ins += [p["norm_ffn"], wgu, p["ffn_cw"], p["ffn_cb"], wd, p["norm_final"]]
    in_specs += [_layer_spec(p["norm_ffn"].shape, layer), _const_spec(wgu.shape),
                 _layer_spec(p["ffn_cw"].shape, layer), _layer_spec(p["ffn_cb"].shape, layer),
                 _const_spec(wd.shape), _const_spec(p["norm_final"].shape)]
    riders = [_cast_rider(w, l, nt) for w, l in cast]
    ins += [w for w, _ in cast]
    in_specs += [r[0] for r in riders]
    out_specs += [r[1] for r in riders]
    out_shape += [r[2] for r in riders]
    if lookahead:
        scratch.append(pltpu.VMEM((2, tl, D_MODEL), BF16))
        if pool is not None:
            scratch.append(pltpu.VMEM((2, tl, D_MODEL), F32))
    kern = functools.partial(_ffn_kernel, tl=tl, ns=ns, padr=padr, final=final, pool=pool,
                             nsplit=nsplit, n_cast=len(cast), lookahead=lookahead)
    return pl.pallas_call(
        kern,
        grid=(n_groups, nt),
        in_specs=in_specs,
        out_specs=out_specs,
        out_shape=out_shape,
        scratch_shapes=scratch,
        compiler_params=pltpu.CompilerParams(
            dimension_semantics=("arbitrary", "arbitrary"), vmem_limit_bytes=VMEM_LIMIT),
        name=("pool_" if pool else "") + ("ffn_final" if final else "ffn"),
    )(*ins)


def _pool_kernel(x_ref, st_ref, gn_ref, wp_ref, sc_ref, xo_ref, sto_ref, ext_ref,
                 *, tl, ns, padr, start):
    t = pl.program_id(1)

    @pl.when(t == 0)
    def _():
        ext_ref[0:padr, :] = st_ref[0]

    pos0 = start + ((t * tl) >> _log2(ns))
    xo_ref[...] = _pool_mix(x_ref[...], gn_ref[...], ext_ref, padr, 0, ns, pos0, wp_ref, sc_ref[...])

    carry = ext_ref[tl:tl + padr, :]
    ext_ref[0:padr, :] = carry

    @pl.when(t == pl.num_programs(1) - 1)
    def _():
        sto_ref[0] = carry


def _pool(x2d, state, gn, wp, sc, *, tl, ns, start):
    n_groups, padr, _ = state.shape
    rows_per_group = x2d.shape[0] // n_groups
    nt = rows_per_group // tl
    assert padr % SUBLANES == 0 and padr >= POOL_BUF * ns and (nt == 1 or tl >= padr)
    kern = functools.partial(_pool_kernel, tl=tl, ns=ns, padr=padr, start=start)
    row_spec = pl.BlockSpec((tl, D_MODEL), lambda n, t: (n * nt + t, 0))
    st_spec = pl.BlockSpec((1, padr, D_MODEL), lambda n, t: (n, 0, 0))
    return pl.pallas_call(
        kern,
        grid=(n_groups, nt),
        in_specs=[row_spec, st_spec, _const_spec(gn.shape), _const_spec(wp.shape),
                  _const_spec(sc.shape)],
        out_specs=[row_spec, st_spec],
        out_shape=[jax.ShapeDtypeStruct(x2d.shape, F32),
                   jax.ShapeDtypeStruct(state.shape, F32)],
        scratch_shapes=[pltpu.VMEM((padr + tl, D_MODEL), F32)],
        compiler_params=pltpu.CompilerParams(
            dimension_semantics=("arbitrary", "arbitrary"), vmem_limit_bytes=VMEM_LIMIT),
        name="pool",
    )(x2d, state, gn, wp, sc)


def kernel(x_prompt, x_sample, state_conv_a, state_hgrn, state_pool, state_ffn, norm_mix, norm_ffn,
           norm_final, w_in, conv_a_w, hgrn_lower_bounds, hgrn_norm, w_out, pool_w, pool_scale,
           ffn_w_gu, ffn_conv_w, ffn_conv_b, ffn_w_down):
    batch, seq, _ = x_prompt.shape
    dec_batch, dec_seq, _ = x_sample.shape

    p = dict(
        norm_mix=norm_mix[:, None, :], norm_ffn=norm_ffn[:, None, :], norm_final=norm_final[None, :],
        ffn_cw=ffn_conv_w, ffn_cb=ffn_conv_b[:, None, :], pool_w=pool_w[0].astype(BF16),
        pool_scale=pool_scale[0][None, :])
    win = w_in[0].astype(BF16)
    wout = w_out[0].astype(BF16)
    mix_args = (p["norm_mix"][0], win, conv_a_w[0], hgrn_lower_bounds, hgrn_norm[0][None, :], wout)

    xp = x_prompt.reshape(batch * seq, D_MODEL)
    x1, conv_p, hgrn_p, wgu0, wd0 = _mixer0_prompt(
        xp, batch, seq, *mix_args, [(ffn_w_gu, 0), (ffn_w_down, 0)], tl=512, nsplit=2)
    pad_ffn = _round_up(CONV_HIST, SUBLANES)
    pad_pool = _round_up(POOL_BUF, SUBLANES)
    zf = jnp.zeros((batch, pad_ffn, D_FF), F32)
    zp = jnp.zeros((batch, pad_pool, D_MODEL), F32)
    x2, ffn0_p, wgu1, wd1 = _ffn(x1, zf, p, 0, wgu0, wd0, tl=512, ns=1, final=False, nsplit=2,
                                 cast=[(ffn_w_gu, 1), (ffn_w_down, 1)])
    yp, ffn1_p, pool_p = _ffn(x2, zf, p, 1, wgu1, wd1, tl=512, ns=1, final=True, nsplit=2,
                              pool_state=zp, start=0, lookahead=True)
    y_prompt = yp.reshape(batch, seq, D_MODEL)
    conv_prompt = conv_p[None, :, SUBLANES - CONV_HIST:, :]
    hgrn_prompt = hgrn_p[None]
    pool_prompt = pool_p[None, :, pad_pool - POOL_BUF:, :]
    ffn_prompt = jnp.stack([ffn0_p, ffn1_p])[:, :, pad_ffn - CONV_HIST:, :]

    xs = x_sample.reshape(dec_batch * dec_seq, D_MODEL)
    cs = state_conv_a[0]
    zeros_tail = jnp.zeros((dec_batch, dec_seq - 1, D_A), F32)
    e1 = jnp.concatenate([cs[:, 1:2], zeros_tail], axis=1).reshape(dec_batch * dec_seq, D_A)
    e2 = jnp.concatenate([cs, zeros_tail[:, 1:]], axis=1).reshape(dec_batch * dec_seq, D_A)
    x1s, us, hgrn_s = _mixer0_sample(xs, e1, e2, state_hgrn[0], *mix_args, seq_len=dec_seq, bn=16)
    conv_sample = us.reshape(dec_batch, dec_seq, D_A)[None, :, dec_seq - CONV_HIST:, :]

    def to_tm(a):
        return a.transpose(1, 0, 2).reshape(1, a.shape[1] * dec_batch, a.shape[2])

    def from_tm(a, rows):
        return a.reshape(rows, dec_batch, a.shape[-1]).transpose(1, 0, 2)

    x1t = to_tm(x1s.reshape(dec_batch, dec_seq, D_MODEL))[0]
    x2t, ffn0_s = _ffn(x1t, to_tm(state_ffn[0]), p, 0, wgu0, wd0, tl=512, ns=dec_batch, final=False,
                       nsplit=2)
    x3t, pool_s = _pool(x2t, to_tm(state_pool[0]), p["norm_mix"][1], p["pool_w"], p["pool_scale"],
                        tl=dec_batch * dec_seq, ns=dec_batch, start=PAST_LEN)
    ys, ffn1_s = _ffn(x3t, to_tm(state_ffn[1]), p, 1, wgu1, wd1, tl=512, ns=dec_batch, final=True,
                      nsplit=2)
    y_sample = from_tm(ys, dec_seq)
    pool_sample = from_tm(pool_s[0], POOL_BUF)[None]
    ffn_sample = jnp.stack([from_tm(ffn0_s[0], CONV_HIST), from_tm(ffn1_s[0], CONV_HIST)])

    return (y_prompt, y_sample, conv_prompt, conv_sample, hgrn_prompt, hgrn_s[None],
            pool_prompt, pool_sample, ffn_prompt, ffn_sample)
```

```python
import functools

import jax
import jax.numpy as jnp
import numpy as np
from jax import lax
from jax.experimental import pallas as pl
from jax.experimental.pallas import tpu as pltpu

D_MODEL = 1024
D_A = 512
N_HEADS = 4
HEAD = 128
D_B = N_HEADS * HEAD
D_IN = 3 * D_A + 4 * D_B
D_FF = 2816
POOL_WINDOWS = (2, 4, 8, 16)
POOL_GROUP = D_MODEL // len(POOL_WINDOWS)
POOL_BUF = max(POOL_WINDOWS) - 1
CONV_HIST = 2
PAST_LEN = 16384
EPS = 1e-6
LOG2E = 1.4426950408889634

SUBLANES = 8
HGRN_CHUNK = 128
VMEM_LIMIT = 56 * 1024 * 1024
CAST_STEPS = 16

F32 = jnp.float32
BF16 = jnp.bfloat16

_OFF_AC, _OFF_AB, _OFF_AV = 0, D_A, 2 * D_A
_OFF_Q, _OFF_F, _OFF_I, _OFF_G = (3 * D_A, 3 * D_A + D_B, 3 * D_A + 2 * D_B, 3 * D_A + 3 * D_B)
_HEADS = [slice(h * HEAD, (h + 1) * HEAD) for h in range(N_HEADS)]


def _round_up(x, m):
    return (x + m - 1) // m * m


def _log2(x):
    assert x & (x - 1) == 0
    return x.bit_length() - 1


def _rms(x, g):
    return x * lax.rsqrt(jnp.mean(x * x, axis=-1, keepdims=True) + EPS) * g


def _silu(x):
    return x * jax.nn.sigmoid(x)


def _dot(a, b):
    return jnp.dot(a, b, preferred_element_type=F32)


def _dot_nt(a, b):
    return lax.dot_general(a, b, (((1,), (1,)), ((), ())), preferred_element_type=F32)


def _dot_tn(a, b):
    return lax.dot_general(a, b, (((0,), (0,)), ((), ())), preferred_element_type=F32)


def _dep_zero(a):
    u = lax.bitcast_convert_type(a, jnp.int32)
    return lax.shift_right_logical(lax.shift_right_logical(u, 16), 16).astype(F32)


def _shifted(ext_ref, padr, n, k_rows, cols=slice(None), row0=0):
    off = padr + row0 - k_rows
    assert off >= 0
    base = off // SUBLANES * SUBLANES
    rem = off - base
    if rem == 0:
        return ext_ref[pl.ds(off, n), cols]
    win = ext_ref[pl.ds(base, n + SUBLANES), cols]
    return pltpu.roll(win, SUBLANES - rem, 0)[SUBLANES:SUBLANES + n, :]


def _levels(seg):
    return [1 << i for i in range(_log2(seg))]


def _selector_levels(seg):
    return [b for b in _levels(seg) if 1 < b < SUBLANES]


def _selector(c, seg):
    t = np.arange(c)[:, None]
    u = np.arange(c)[None, :]
    same_seg = (t // seg) == (u // seg)
    blocks = [same_seg & (u <= t)]
    blocks.append(same_seg & (u > t) if seg < c else np.ones((SUBLANES, c), bool))
    for b in _selector_levels(seg):
        pair = (t // (2 * b)) == (u // (2 * b))
        anchor = (t // (2 * b)) * 2 * b + b - 1
        right = (t // b) % 2 == 1
        blocks.append(pair & np.where(right, (u > anchor) & (u <= t), (u > t) & (u <= anchor)))
    sel = np.concatenate(blocks, axis=0).astype(np.float32)
    return np.concatenate([sel, sel], axis=1)


def _split_bf16(a):
    hi = a.astype(BF16)
    return hi, (a - hi.astype(F32)).astype(BF16)


def _hgrn_intra(q, kk, f, lf2, sel_ref, g_ref, seg):
    c = q.shape[0]
    x = _dot(sel_ref[...], jnp.concatenate(_split_bf16(lf2), axis=0))
    g = x[0:c]

    row = lax.broadcasted_iota(jnp.int32, (c, 1), 0)
    rowi = lax.broadcasted_iota(jnp.int32, (c, c), 0)
    coli = lax.broadcasted_iota(jnp.int32, (c, c), 1)
    diff = rowi ^ coli
    lower = rowi > coli

    p0 = q * kk
    p1 = q * pltpu.roll(kk, 1, 0) * f
    amat = []
    for sl in _HEADS:
        a0 = jnp.sum(p0[:, sl], axis=-1, keepdims=True)
        a1 = jnp.sum(p1[:, sl], axis=-1, keepdims=True)
        amat.append(jnp.where(rowi == coli, a0, jnp.where(lower & (diff == 1), a1, 0.0)))

    sel_blocks = {b: x[off:off + c] for b, off in
                  zip(_selector_levels(seg), range(c + (c if seg < c else SUBLANES), x.shape[0], c))}
    if seg > SUBLANES:
        g_ref[...] = g
    for b in _levels(seg)[1:]:
        if b in sel_blocks:
            ex = sel_blocks[b]
        else:
            parts = []
            for pair in range(c // (2 * b)):
                anchor = g_ref[pl.ds(pair * 2 * b + b - 1, 1), :]
                parts.append(g[pair * 2 * b:(pair + 1) * 2 * b, :] - anchor)
            ex = -jnp.abs(parts[0] if len(parts) == 1 else jnp.concatenate(parts, axis=0))
        w = jnp.exp2(ex)
        right = ((row >> _log2(b)) & 1) == 1
        ql = jnp.where(right, q * w, 0.0).astype(BF16)
        kl = jnp.where(right, 0.0, kk * w).astype(BF16)
        mask = lower & ((diff >> _log2(b)) == 1)
        for h, sl in enumerate(_HEADS):
            amat[h] = jnp.where(mask, _dot_nt(ql[:, sl], kl[:, sl]), amat[h])
    return amat, q * jnp.exp2(g), x


def _gates(z_ref, rows, lb):
    f = lb + (1.0 - lb) * jax.nn.sigmoid(z_ref[rows, _OFF_F:_OFF_F + D_B])
    return f, 1.0 - f, jnp.log(f) * LOG2E


def _lower_bound(lbp_ref):
    lbp = lbp_ref[...]
    lbe = jnp.exp(lbp - jnp.max(lbp, axis=0, keepdims=True))
    return lbe[0:1, :] / jnp.sum(lbe, axis=0, keepdims=True)


def _mixer_b_out(z_ref, o_ref, y_ref, hn, rows):
    for h, sl in enumerate(_HEADS):
        on = _rms(o_ref[rows, sl], hn[:, sl])
        gate = z_ref[rows, _OFF_G + h * HEAD:_OFF_G + (h + 1) * HEAD]
        y_ref[rows, D_A + h * HEAD:D_A + (h + 1) * HEAD] = (on * _silu(gate)).astype(BF16)


def _cast_rider(w, layer, nt):
    _, r, c = w.shape
    rows = r // CAST_STEPS
    assert rows * CAST_STEPS == r and rows % (2 * SUBLANES) == 0

    def blk(n, t):
        return jnp.minimum(n * nt + t, CAST_STEPS - 1)

    return (pl.BlockSpec((None, rows, c), lambda n, t: (layer, blk(n, t), 0)),
            pl.BlockSpec((rows, c), lambda n, t: (blk(n, t), 0)),
            jax.ShapeDtypeStruct((r, c), BF16))


def _run_cast_riders(nt, pairs):
    @pl.when(pl.program_id(0) * nt + pl.program_id(1) < CAST_STEPS)
    def _():
        for src, dst in pairs:
            dst[...] = src[...].astype(BF16)


def _mixer0_prompt_kernel(x_ref, gn_ref, win_ref, cw_ref, lbp_ref, hn_ref, wout_ref, sel_ref,
                          c0_ref, c1_ref,
                          xo_ref, convo_ref, so_ref, c0o_ref, c1o_ref,
                          hb_ref, z_ref, o_ref, y_ref, ext_ref, st_ref, g_ref, *, tl, chunk, nsplit):
    t = pl.program_id(1)
    _run_cast_riders(pl.num_programs(1), [(c0_ref, c0o_ref), (c1_ref, c1o_ref)])

    @pl.when(t == 0)
    def _():
        ext_ref[0:SUBLANES, :] = jnp.zeros((SUBLANES, D_A), F32)
        st_ref[...] = jnp.zeros((N_HEADS, HEAD, HEAD), F32)

    hb_ref[...] = _rms(x_ref[...], gn_ref[...]).astype(BF16)
    lb = _lower_bound(lbp_ref)
    cw = cw_ref[...]
    hn = hn_ref[...]
    part = tl // nsplit

    for s in range(nsplit):
        r0 = s * part
        rows = slice(r0, r0 + part)
        hb = hb_ref[rows, :]
        z_ref[rows, _OFF_Q:_OFF_G] = _dot(hb, win_ref[:, _OFF_Q:_OFF_G])

        for c0 in range(r0, r0 + part, chunk):
            crows = slice(c0, c0 + chunk)
            f, kk, lf2 = _gates(z_ref, crows, lb)
            amat, qg, x = _hgrn_intra(z_ref[crows, _OFF_Q:_OFF_Q + D_B], kk, f, lf2, sel_ref,
                                      g_ref.at[c0 // chunk], chunk)
            g_last = x[chunk:chunk + 1]
            dec = jnp.exp2(g_last)
            kg = (kk * jnp.exp2(g_last - x[0:chunk])).astype(BF16)
            qg = qg.astype(BF16)
            vb = z_ref[crows, _OFF_I:_OFF_I + D_B].astype(BF16)
            for h, sl in enumerate(_HEADS):
                st = st_ref[h]
                lhs = jnp.concatenate([qg[:, sl], amat[h].astype(BF16)], axis=1)
                rhs = jnp.concatenate([st.T.astype(BF16), vb[:, sl]], axis=0)
                o_ref[crows, sl] = _dot(lhs, rhs)
                st_ref[h] = dec[:, sl] * st + _dot_tn(vb[:, sl], kg[:, sl])

        z_ref[rows, 0:_OFF_Q] = _dot(hb, win_ref[:, 0:_OFF_Q])
        z_ref[rows, _OFF_G:D_IN] = _dot(hb, win_ref[:, _OFF_G:D_IN])

        u = z_ref[rows, _OFF_AC:_OFF_AC + D_A] * z_ref[rows, _OFF_AV:_OFF_AV + D_A]
        ext_ref[SUBLANES + r0:SUBLANES + r0 + part, :] = u
        conv = (_shifted(ext_ref, SUBLANES, part, 2, row0=r0) * cw[0:1, :]
                + _shifted(ext_ref, SUBLANES, part, 1, row0=r0) * cw[1:2, :] + u * cw[2:3, :])
        y_ref[rows, 0:D_A] = (z_ref[rows, _OFF_AB:_OFF_AB + D_A] * conv).astype(BF16)
        _mixer_b_out(z_ref, o_ref, y_ref, hn, rows)
        xo_ref[rows, :] = x_ref[rows, :] + _dot(y_ref[rows, :], wout_ref[...])

    carry = ext_ref[tl:tl + SUBLANES, :]
    ext_ref[0:SUBLANES, :] = carry

    @pl.when(t == pl.num_programs(1) - 1)
    def _():
        convo_ref[0] = carry
        for h in range(N_HEADS):
            so_ref[0, h] = st_ref[h].T


def _mixer0_sample_kernel(x_ref, e1_ref, e2_ref, s0_ref, gn_ref, win_ref, cw_ref, lbp_ref, hn_ref,
                          wout_ref, sel_ref,
                          xo_ref, uo_ref, so_ref,
                          z_ref, o_ref, y_ref, lf_ref, qg_ref, kg_ref, *, tl, seq):
    x = x_ref[...]
    z_ref[...] = _dot(_rms(x, gn_ref[...]).astype(BF16), win_ref[...])

    u = z_ref[:, _OFF_AC:_OFF_AC + D_A] * z_ref[:, _OFF_AV:_OFF_AV + D_A]
    cw = cw_ref[...]
    tpos = lax.broadcasted_iota(jnp.int32, (tl, 1), 0) & (seq - 1)
    u1 = jnp.where(tpos >= 1, pltpu.roll(u, 1, 0), e1_ref[...])
    u2 = jnp.where(tpos >= 2, pltpu.roll(u, 2, 0), e2_ref[...])
    uo_ref[...] = u
    conv = u2 * cw[0:1, :] + u1 * cw[1:2, :] + u * cw[2:3, :]
    y_ref[:, 0:D_A] = (z_ref[:, _OFF_AB:_OFF_AB + D_A] * conv).astype(BF16)

    f, kk, lf2 = _gates(z_ref, slice(None), _lower_bound(lbp_ref))
    amat, qg, xsel = _hgrn_intra(z_ref[:, _OFF_Q:_OFF_Q + D_B], kk, f, lf2, sel_ref, None, seq)
    v_all = z_ref[:, _OFF_I:_OFF_I + D_B].astype(BF16)
    o_ref[...] = jnp.concatenate(
        [_dot(amat[h].astype(BF16), v_all[:, sl]) for h, sl in enumerate(_HEADS)], axis=1)
    qg_ref[...] = qg
    kg_ref[...] = kk * jnp.exp2(xsel[tl:2 * tl])
    lf_ref[...] = lf2
    ones = jnp.ones((seq, HEAD), BF16)

    def seq_body(i, carry_unused):
        rows = pl.ds(pl.multiple_of(i * seq, seq), seq)
        hi, mid = _split_bf16(lf_ref[rows, :])
        qg_i = qg_ref[rows, :].astype(BF16)
        kg_i = kg_ref[rows, :].astype(BF16)
        vb = z_ref[rows, _OFF_I:_OFF_I + D_B].astype(BF16)
        for h, sl in enumerate(_HEADS):
            s0 = s0_ref[i, h]
            o_ref[rows, sl] = o_ref[rows, sl] + _dot(qg_i[:, sl], s0.astype(BF16))
            g_last = _dot_tn(hi[:, sl], ones) + _dot_tn(mid[:, sl], ones)
            so_ref[i, h] = jnp.exp2(g_last) * s0 + _dot_tn(kg_i[:, sl], vb[:, sl])
        return carry_unused

    lax.fori_loop(0, tl // seq, seq_body, 0, unroll=4)

    _mixer_b_out(z_ref, o_ref, y_ref, hn_ref[...], slice(None))
    xo_ref[...] = x + _dot(y_ref[...], wout_ref[...])


def _const_spec(shape):
    nd = len(shape)
    return pl.BlockSpec(shape, lambda *_: (0,) * nd, pipeline_mode=pl.Buffered(1))


def _layer_spec(shape, layer):
    nd = len(shape) - 1
    return pl.BlockSpec((None,) + tuple(shape[1:]), lambda *_: (layer,) + (0,) * nd,
                        pipeline_mode=pl.Buffered(1))


def _mixer0_prompt(x2d, n_seq, seq_len, gn, win, cw, lbp, hn, wout, cast, *, tl, nsplit):
    nt = seq_len // tl
    assert n_seq * nt >= CAST_STEPS
    sel = jnp.asarray(_selector(HGRN_CHUNK, HGRN_CHUNK), BF16)
    kern = functools.partial(_mixer0_prompt_kernel, tl=tl, chunk=HGRN_CHUNK, nsplit=nsplit)
    row_spec = pl.BlockSpec((tl, D_MODEL), lambda n, t: (n * nt + t, 0))
    consts = (gn, win, cw, lbp, hn, wout, sel)
    riders = [_cast_rider(w, layer, nt) for w, layer in cast]
    return pl.pallas_call(
        kern,
        grid=(n_seq, nt),
        in_specs=[row_spec] + [_const_spec(a.shape) for a in consts] + [r[0] for r in riders],
        out_specs=[row_spec,
                   pl.BlockSpec((1, SUBLANES, D_A), lambda n, t: (n, 0, 0)),
                   pl.BlockSpec((1, N_HEADS, HEAD, HEAD), lambda n, t: (n, 0, 0, 0))]
                  + [r[1] for r in riders],
        out_shape=[jax.ShapeDtypeStruct(x2d.shape, F32),
                   jax.ShapeDtypeStruct((n_seq, SUBLANES, D_A), F32),
                   jax.ShapeDtypeStruct((n_seq, N_HEADS, HEAD, HEAD), F32)]
                  + [r[2] for r in riders],
        scratch_shapes=[pltpu.VMEM((tl, D_MODEL), BF16),
                        pltpu.VMEM((tl, D_IN), F32),
                        pltpu.VMEM((tl, D_B), F32),
                        pltpu.VMEM((tl, D_A + D_B), BF16),
                        pltpu.VMEM((SUBLANES + tl, D_A), F32),
                        pltpu.VMEM((N_HEADS, HEAD, HEAD), F32),
                        pltpu.VMEM((tl // HGRN_CHUNK, HGRN_CHUNK, D_B), F32)],
        compiler_params=pltpu.CompilerParams(
            dimension_semantics=("arbitrary", "arbitrary"), vmem_limit_bytes=VMEM_LIMIT),
        name="mixer0_prompt",
    )(x2d, *consts, *[w for w, _ in cast])


def _mixer0_sample(x2d, e1, e2, s0, gn, win, cw, lbp, hn, wout, *, seq_len, bn):
    n_seq = s0.shape[0]
    tl = bn * seq_len
    sel = jnp.asarray(_selector(tl, seq_len), BF16)
    kern = functools.partial(_mixer0_sample_kernel, tl=tl, seq=seq_len)
    row_spec = pl.BlockSpec((tl, D_MODEL), lambda i: (i, 0))
    conv_spec = pl.BlockSpec((tl, D_A), lambda i: (i, 0))
    st_spec = pl.BlockSpec((bn, N_HEADS, HEAD, HEAD), lambda i: (i, 0, 0, 0))
    consts = (gn, win, cw, lbp, hn, wout, sel)
    return pl.pallas_call(
        kern,
        grid=(n_seq // bn,),
        in_specs=[row_spec, conv_spec, conv_spec, st_spec] + [_const_spec(a.shape) for a in consts],
        out_specs=[row_spec, conv_spec, st_spec],
        out_shape=[jax.ShapeDtypeStruct(x2d.shape, F32),
                   jax.ShapeDtypeStruct((x2d.shape[0], D_A), F32),
                   jax.ShapeDtypeStruct(s0.shape, F32)],
        scratch_shapes=[pltpu.VMEM((tl, D_IN), F32),
                        pltpu.VMEM((tl, D_B), F32),
                        pltpu.VMEM((tl, D_A + D_B), BF16),
                        pltpu.VMEM((tl, D_B), F32),
                        pltpu.VMEM((tl, D_B), F32),
                        pltpu.VMEM((tl, D_B), F32)],
        compiler_params=pltpu.CompilerParams(
            dimension_semantics=("arbitrary",), vmem_limit_bytes=VMEM_LIMIT),
        name="mixer0_sample",
    )(x2d, e1, e2, s0, *consts)


def _pool_mix(x, gm, ext_ref, padr, row0, ns, pos0, wp_ref, sc):
    n = x.shape[0]
    h = _rms(x, gm)
    ext_ref[padr + row0:padr + row0 + n, :] = h
    pos = pos0 + (lax.broadcasted_iota(jnp.int32, (n, 1), 0) >> _log2(ns))
    ys = []
    for gi, w in enumerate(POOL_WINDOWS):
        cols = slice(gi * POOL_GROUP, (gi + 1) * POOL_GROUP)
        hg = h[:, cols]
        if ns == 1:
            hist = _round_up(POOL_BUF, SUBLANES)
            s = ext_ref[pl.ds(padr + row0 - hist, n + hist), cols]
            sh = 1
            while sh < w:
                s = s + pltpu.roll(s, sh, 0)
                sh *= 2
            acc = s[hist:hist + n, :]
        else:
            acc = hg
            for k in range(1, w):
                acc = acc + _shifted(ext_ref, padr, n, k * ns, cols, row0)
        cnt = jnp.minimum(w, pos + 1).astype(F32)
        ys.append(_dot((acc / cnt - hg).astype(BF16), wp_ref[gi]))
    return x + jnp.concatenate(ys, axis=1) * sc


def _ffn_kernel(*refs, tl, ns, padr, final, pool, nsplit, n_cast, lookahead):
    refs = list(refs)
    x_ref, st_ref = refs[:2]
    del refs[:2]
    if lookahead:
        xn_ref = refs.pop(0)
    if pool is not None:
        pst_ref, gm_ref, wp_ref, sc_ref = refs[:4]
        del refs[:4]
        ppad, start = pool
    gn_ref, wgu_ref, cw_ref, cb_ref, wd_ref, gf_ref = refs[:6]
    del refs[:6]
    cast_in = refs[:n_cast]
    del refs[:n_cast]
    xo_ref, sto_ref = refs[:2]
    del refs[:2]
    if pool is not None:
        psto_ref = refs.pop(0)
    cast_out = refs[:n_cast]
    del refs[:n_cast]
    ext_ref = refs.pop(0)
    if pool is not None:
        pext_ref = refs.pop(0)
    if lookahead:
        hb_buf = refs.pop(0)
        if pool is not None:
            xm_buf = refs.pop(0)
    assert not refs
    t = pl.program_id(1)
    last = pl.num_programs(1) - 1
    part = tl // nsplit
    if n_cast:
        _run_cast_riders(pl.num_programs(1), list(zip(cast_in, cast_out)))

    def front(src_ref, tile, r0, anchor=None):
        zero = 0.0 if anchor is None else _dep_zero(anchor)
        x = src_ref[r0:r0 + part, :]
        if pool is not None:
            pos0 = start + ((tile * tl + r0) >> _log2(ns))
            x = _pool_mix(x, gm_ref[...] + zero, pext_ref, ppad, r0, ns, pos0, wp_ref, sc_ref[...])
        return x, _rms(x, gn_ref[...] + zero).astype(BF16)

    def front_tile(src_ref, tile, slot, anchors=None):
        for s in range(nsplit):
            rows = slice(s * part, (s + 1) * part)
            x, hb = front(src_ref, tile, s * part, None if anchors is None else anchors[s])
            hb_buf[slot, rows, :] = hb
            if pool is not None:
                xm_buf[slot, rows, :] = x
        if pool is not None:
            pext_ref[0:ppad, :] = pext_ref[tl:tl + ppad, :]

    @pl.when(t == 0)
    def _():
        ext_ref[0:padr, :] = st_ref[0]
        if pool is not None:
            pext_ref[0:ppad, :] = pst_ref[0]
        if lookahead:
            front_tile(x_ref, 0, 0)

    cw = cw_ref[...]
    cb = cb_ref[...]
    slot = t % 2
    anchors = []
    for s in range(nsplit):
        r0 = s * part
        rows = slice(r0, r0 + part)
        if lookahead:
            hb = hb_buf[slot, rows, :]
            x = xm_buf[slot, rows, :] if pool is not None else x_ref[rows, :]
        else:
            x, hb = front(x_ref, t, r0)
        g = _dot(hb, wgu_ref[:, 0:D_FF])
        anchors.append(g[0:1, 0:D_MODEL])
        ext_ref[padr + r0:padr + r0 + part, :] = g
        gc = (_shifted(ext_ref, padr, part, 2 * ns, row0=r0) * cw[0:1, :]
              + _shifted(ext_ref, padr, part, ns, row0=r0) * cw[1:2, :]
              + g * cw[2:3, :] + cb)
        up = _dot(hb, wgu_ref[:, D_FF:2 * D_FF])
        act = (_silu(gc) * up).astype(BF16)
        xo = x + _dot(act, wd_ref[...])
        if final:
            xo = _rms(xo, gf_ref[...])
        xo_ref[rows, :] = xo

    if lookahead:
        front_tile(xn_ref, jnp.minimum(t + 1, last), 1 - slot, anchors)
    elif pool is not None:
        pext_ref[0:ppad, :] = pext_ref[tl:tl + ppad, :]
    carry = ext_ref[tl:tl + padr, :]
    ext_ref[0:padr, :] = carry

    @pl.when(t == last)
    def _():
        sto_ref[0] = carry
        if pool is not None:
            psto_ref[0] = pext_ref[0:ppad, :]


def _ffn(x2d, state, p, layer, wgu, wd, *, tl, ns, final, nsplit=1, pool_state=None, start=0, cast=(),
         lookahead=False):
    n_groups, padr, _ = state.shape
    rows_per_group = x2d.shape[0] // n_groups
    nt = rows_per_group // tl
    assert padr % SUBLANES == 0 and padr >= CONV_HIST * ns and (nt == 1 or tl >= padr)
    assert not cast or n_groups * nt >= CAST_STEPS
    row_spec = pl.BlockSpec((tl, D_MODEL), lambda n, t: (n * nt + t, 0))
    st_spec = pl.BlockSpec((1, padr, D_FF), lambda n, t: (n, 0, 0))
    ins = [x2d, state]
    in_specs = [row_spec, st_spec]
    if lookahead:
        assert nt > 1
        ins.append(x2d)
        in_specs.append(pl.BlockSpec((tl, D_MODEL), lambda n, t: (n * nt + jnp.minimum(t + 1, nt - 1), 0)))
        if pool_state is not None:
            in_specs[0] = pl.BlockSpec((tl, D_MODEL), lambda n, t: (n * nt, 0))
    out_specs = [row_spec, st_spec]
    out_shape = [jax.ShapeDtypeStruct(x2d.shape, F32), jax.ShapeDtypeStruct(state.shape, F32)]
    scratch = [pltpu.VMEM((padr + tl, D_FF), F32)]
    pool = None
    if pool_state is not None:
        ppad = pool_state.shape[1]
        assert ppad % SUBLANES == 0 and ppad >= POOL_BUF * ns and (nt == 1 or tl >= ppad)
        pool = (ppad, start)
        pst_spec = pl.BlockSpec((1, ppad, D_MODEL), lambda n, t: (n, 0, 0))
        pool_consts = (p["norm_mix"][layer], p["pool_w"], p["pool_scale"])
        ins += [pool_state, *pool_consts]
        in_specs += [pst_spec] + [_const_spec(a.shape) for a in pool_consts]
        out_specs.append(pst_spec)
        out_shape.append(jax.ShapeDtypeStruct(pool_state.shape, F32))
        scratch.append(pltpu.VMEM((ppad + tl, D_MODEL), F32))
    ins += [p["norm_ffn"], wgu, p["ffn_cw"], p["ffn_cb"], wd, p["norm_final"]]
    in_specs += [_layer_spec(p["norm_ffn"].shape, layer), _const_spec(wgu.shape),
                 _layer_spec(p["ffn_cw"].shape, layer), _layer_spec(p["ffn_cb"].shape, layer),
                 _const_spec(wd.shape), _const_spec(p["norm_final"].shape)]
    riders = [_cast_rider(w, l, nt) for w, l in cast]
    ins += [w for w, _ in cast]
    in_specs += [r[0] for r in riders]
    out_specs += [r[1] for r in riders]
    out_shape += [r[2] for r in riders]
    if lookahead:
        scratch.append(pltpu.VMEM((2, tl, D_MODEL), BF16))
        if pool is not None:
            scratch.append(pltpu.VMEM((2, tl, D_MODEL), F32))
    kern = functools.partial(_ffn_kernel, tl=tl, ns=ns, padr=padr, final=final, pool=pool,
                             nsplit=nsplit, n_cast=len(cast), lookahead=lookahead)
    return pl.pallas_call(
        kern,
        grid=(n_groups, nt),
        in_specs=in_specs,
        out_specs=out_specs,
        out_shape=out_shape,
        scratch_shapes=scratch,
        compiler_params=pltpu.CompilerParams(
            dimension_semantics=("arbitrary", "arbitrary"), vmem_limit_bytes=VMEM_LIMIT),
        name=("pool_" if pool else "") + ("ffn_final" if final else "ffn"),
    )(*ins)


def _pool_kernel(x_ref, st_ref, gn_ref, wp_ref, sc_ref, xo_ref, sto_ref, ext_ref,
                 *, tl, ns, padr, start):
    t = pl.program_id(1)

    @pl.when(t == 0)
    def _():
        ext_ref[0:padr, :] = st_ref[0]

    pos0 = start + ((t * tl) >> _log2(ns))
    xo_ref[...] = _pool_mix(x_ref[...], gn_ref[...], ext_ref, padr, 0, ns, pos0, wp_ref, sc_ref[...])

    carry = ext_ref[tl:tl + padr, :]
    ext_ref[0:padr, :] = carry

    @pl.when(t == pl.num_programs(1) - 1)
    def _():
        sto_ref[0] = carry


def _pool(x2d, state, gn, wp, sc, *, tl, ns, start):
    n_groups, padr, _ = state.shape
    rows_per_group = x2d.shape[0] // n_groups
    nt = rows_per_group // tl
    assert padr % SUBLANES == 0 and padr >= POOL_BUF * ns and (nt == 1 or tl >= padr)
    kern = functools.partial(_pool_kernel, tl=tl, ns=ns, padr=padr, start=start)
    row_spec = pl.BlockSpec((tl, D_MODEL), lambda n, t: (n * nt + t, 0))
    st_spec = pl.BlockSpec((1, padr, D_MODEL), lambda n, t: (n, 0, 0))
    return pl.pallas_call(
        kern,
        grid=(n_groups, nt),
        in_specs=[row_spec, st_spec, _const_spec(gn.shape), _const_spec(wp.shape),
                  _const_spec(sc.shape)],
        out_specs=[row_spec, st_spec],
        out_shape=[jax.ShapeDtypeStruct(x2d.shape, F32),
                   jax.ShapeDtypeStruct(state.shape, F32)],
        scratch_shapes=[pltpu.VMEM((padr + tl, D_MODEL), F32)],
        compiler_params=pltpu.CompilerParams(
            dimension_semantics=("arbitrary", "arbitrary"), vmem_limit_bytes=VMEM_LIMIT),
        name="pool",
    )(x2d, state, gn, wp, sc)


def kernel(x_prompt, x_sample, state_conv_a, state_hgrn, state_pool, state_ffn, norm_mix, norm_ffn,
           norm_final, w_in, conv_a_w, hgrn_lower_bounds, hgrn_norm, w_out, pool_w, pool_scale,
           ffn_w_gu, ffn_conv_w, ffn_conv_b, ffn_w_down):
    batch, seq, _ = x_prompt.shape
    dec_batch, dec_seq, _ = x_sample.shape

    p = dict(
        norm_mix=norm_mix[:, None, :], norm_ffn=norm_ffn[:, None, :], norm_final=norm_final[None, :],
        ffn_cw=ffn_conv_w, ffn_cb=ffn_conv_b[:, None, :], pool_w=pool_w[0].astype(BF16),
        pool_scale=pool_scale[0][None, :])
    win = w_in[0].astype(BF16)
    wout = w_out[0].astype(BF16)
    mix_args = (p["norm_mix"][0], win, conv_a_w[0], hgrn_lower_bounds, hgrn_norm[0][None, :], wout)

    xp = x_prompt.reshape(batch * seq, D_MODEL)
    x1, conv_p, hgrn_p, wgu0, wd0 = _mixer0_prompt(
        xp, batch, seq, *mix_args, [(ffn_w_gu, 0), (ffn_w_down, 0)], tl=1024, nsplit=4)
    pad_ffn = _round_up(CONV_HIST, SUBLANES)
    pad_pool = _round_up(POOL_BUF, SUBLANES)
    zf = jnp.zeros((batch, pad_ffn, D_FF), F32)
    zp = jnp.zeros((batch, pad_pool, D_MODEL), F32)
    x2, ffn0_p, wgu1, wd1 = _ffn(x1, zf, p, 0, wgu0, wd0, tl=512, ns=1, final=False, nsplit=2,
                                 cast=[(ffn_w_gu, 1), (ffn_w_down, 1)])
    yp, ffn1_p, pool_p = _ffn(x2, zf, p, 1, wgu1, wd1, tl=512, ns=1, final=True, nsplit=2,
                              pool_state=zp, start=0, lookahead=True)
    y_prompt = yp.reshape(batch, seq, D_MODEL)
    conv_prompt = conv_p[None, :, SUBLANES - CONV_HIST:, :]
    hgrn_prompt = hgrn_p[None]
    pool_prompt = pool_p[None, :, pad_pool - POOL_BUF:, :]
    ffn_prompt = jnp.stack([ffn0_p, ffn1_p])[:, :, pad_ffn - CONV_HIST:, :]

    xs = x_sample.reshape(dec_batch * dec_seq, D_MODEL)
    cs = state_conv_a[0]
    zeros_tail = jnp.zeros((dec_batch, dec_seq - 1, D_A), F32)
    e1 = jnp.concatenate([cs[:, 1:2], zeros_tail], axis=1).reshape(dec_batch * dec_seq, D_A)
    e2 = jnp.concatenate([cs, zeros_tail[:, 1:]], axis=1).reshape(dec_batch * dec_seq, D_A)
    x1s, us, hgrn_s = _mixer0_sample(xs, e1, e2, state_hgrn[0], *mix_args, seq_len=dec_seq, bn=16)
    conv_sample = us.reshape(dec_batch, dec_seq, D_A)[None, :, dec_seq - CONV_HIST:, :]

    def to_tm(a):
        return a.transpose(1, 0, 2).reshape(1, a.shape[1] * dec_batch, a.shape[2])

    def from_tm(a, rows):
        return a.reshape(rows, dec_batch, a.shape[-1]).transpose(1, 0, 2)

    x1t = to_tm(x1s.reshape(dec_batch, dec_seq, D_MODEL))[0]
    x2t, ffn0_s = _ffn(x1t, to_tm(state_ffn[0]), p, 0, wgu0, wd0, tl=512, ns=dec_batch, final=False,
                       nsplit=2)
    x3t, pool_s = _pool(x2t, to_tm(state_pool[0]), p["norm_mix"][1], p["pool_w"], p["pool_scale"],
                        tl=dec_batch * dec_seq, ns=dec_batch, start=PAST_LEN)
    ys, ffn1_s = _ffn(x3t, to_tm(state_ffn[1]), p, 1, wgu1, wd1, tl=512, ns=dec_batch, final=True,
                      nsplit=2)
    y_sample = from_tm(ys, dec_seq)
    pool_sample = from_tm(pool_s[0], POOL_BUF)[None]
    ffn_sample = jnp.stack([from_tm(ffn0_s[0], CONV_HIST), from_tm(ffn1_s[0], CONV_HIST)])

    return (y_prompt, y_sample, conv_prompt, conv_sample, hgrn_prompt, hgrn_s[None],
            pool_prompt, pool_sample, ffn_prompt, ffn_sample)
```

```python
import functools

import jax
import jax.numpy as jnp
import numpy as np
from jax import lax
from jax.experimental import pallas as pl
from jax.experimental.pallas import tpu as pltpu

D_MODEL = 1024
D_A = 512
N_HEADS = 4
HEAD = 128
D_B = N_HEADS * HEAD
D_IN = 3 * D_A + 4 * D_B
D_FF = 2816
POOL_WINDOWS = (2, 4, 8, 16)
POOL_GROUP = D_MODEL // len(POOL_WINDOWS)
POOL_BUF = max(POOL_WINDOWS) - 1
CONV_HIST = 2
PAST_LEN = 16384
EPS = 1e-6
LOG2E = 1.4426950408889634

SUBLANES = 8
HGRN_CHUNK = 128
VMEM_LIMIT = 56 * 1024 * 1024
CAST_STEPS = 16
MIXER_TILE, MIXER_PARTS = 1024, 4
FFN_TILE, FFN_PARTS = 1024, 4
POOL_FFN_TILE, POOL_FFN_PARTS = 512, 2
SAMPLE_FFN_TILE, SAMPLE_FFN_PARTS = 512, 2
SAMPLE_SEQS_PER_TILE = 16

F32 = jnp.float32
BF16 = jnp.bfloat16

_OFF_AC, _OFF_AB, _OFF_AV = 0, D_A, 2 * D_A
_OFF_Q, _OFF_F, _OFF_I, _OFF_G = (3 * D_A, 3 * D_A + D_B, 3 * D_A + 2 * D_B, 3 * D_A + 3 * D_B)
_HEADS = [slice(h * HEAD, (h + 1) * HEAD) for h in range(N_HEADS)]


def _round_up(x, m):
    return (x + m - 1) // m * m


def _log2(x):
    assert x & (x - 1) == 0
    return x.bit_length() - 1


def _rms(x, g):
    return x * lax.rsqrt(jnp.mean(x * x, axis=-1, keepdims=True) + EPS) * g


def _silu(x):
    return x * jax.nn.sigmoid(x)


def _dot(a, b):
    return jnp.dot(a, b, preferred_element_type=F32)


def _dot_nt(a, b):
    return lax.dot_general(a, b, (((1,), (1,)), ((), ())), preferred_element_type=F32)


def _dot_tn(a, b):
    return lax.dot_general(a, b, (((0,), (0,)), ((), ())), preferred_element_type=F32)


def _dep_zero(a):
    u = lax.bitcast_convert_type(a, jnp.int32)
    return lax.shift_right_logical(lax.shift_right_logical(u, 16), 16).astype(F32)


def _shifted(ext_ref, padr, n, k_rows, cols=slice(None), row0=0):
    off = padr + row0 - k_rows
    assert off >= 0
    base = off // SUBLANES * SUBLANES
    rem = off - base
    if rem == 0:
        return ext_ref[pl.ds(off, n), cols]
    win = ext_ref[pl.ds(base, n + SUBLANES), cols]
    return pltpu.roll(win, SUBLANES - rem, 0)[SUBLANES:SUBLANES + n, :]


def _levels(seg):
    return [1 << i for i in range(_log2(seg))]


def _selector_levels(seg):
    return [b for b in _levels(seg) if 1 < b < SUBLANES]


def _selector(c, seg):
    t = np.arange(c)[:, None]
    u = np.arange(c)[None, :]
    same_seg = (t // seg) == (u // seg)
    blocks = [same_seg & (u <= t)]
    blocks.append(same_seg & (u > t) if seg < c else np.ones((SUBLANES, c), bool))
    for b in _selector_levels(seg):
        pair = (t // (2 * b)) == (u // (2 * b))
        anchor = (t // (2 * b)) * 2 * b + b - 1
        right = (t // b) % 2 == 1
        blocks.append(pair & np.where(right, (u > anchor) & (u <= t), (u > t) & (u <= anchor)))
    sel = np.concatenate(blocks, axis=0).astype(np.float32)
    return np.concatenate([sel, sel], axis=1)


def _split_bf16(a):
    hi = a.astype(BF16)
    return hi, (a - hi.astype(F32)).astype(BF16)


def _hgrn_intra(q, kk, f, lf2, sel_ref, g_ref, seg):
    c = q.shape[0]
    x = _dot(sel_ref[...], jnp.concatenate(_split_bf16(lf2), axis=0))
    g = x[0:c]

    row = lax.broadcasted_iota(jnp.int32, (c, 1), 0)
    rowi = lax.broadcasted_iota(jnp.int32, (c, c), 0)
    coli = lax.broadcasted_iota(jnp.int32, (c, c), 1)
    diff = rowi ^ coli
    lower = rowi > coli

    p0 = q * kk
    p1 = q * pltpu.roll(kk, 1, 0) * f
    amat = []
    for sl in _HEADS:
        a0 = jnp.sum(p0[:, sl], axis=-1, keepdims=True)
        a1 = jnp.sum(p1[:, sl], axis=-1, keepdims=True)
        amat.append(jnp.where(rowi == coli, a0, jnp.where(lower & (diff == 1), a1, 0.0)))

    sel_blocks = {b: x[off:off + c] for b, off in
                  zip(_selector_levels(seg), range(c + (c if seg < c else SUBLANES), x.shape[0], c))}
    if seg > SUBLANES:
        g_ref[...] = g
    for b in _levels(seg)[1:]:
        if b in sel_blocks:
            ex = sel_blocks[b]
        else:
            parts = []
            for pair in range(c // (2 * b)):
                anchor = g_ref[pl.ds(pair * 2 * b + b - 1, 1), :]
                parts.append(g[pair * 2 * b:(pair + 1) * 2 * b, :] - anchor)
            ex = -jnp.abs(parts[0] if len(parts) == 1 else jnp.concatenate(parts, axis=0))
        w = jnp.exp2(ex)
        right = ((row >> _log2(b)) & 1) == 1
        ql = jnp.where(right, q * w, 0.0).astype(BF16)
        kl = jnp.where(right, 0.0, kk * w).astype(BF16)
        mask = lower & ((diff >> _log2(b)) == 1)
        for h, sl in enumerate(_HEADS):
            amat[h] = jnp.where(mask, _dot_nt(ql[:, sl], kl[:, sl]), amat[h])
    return amat, q * jnp.exp2(g), x


def _gates(z_ref, rows, lb):
    f = lb + (1.0 - lb) * jax.nn.sigmoid(z_ref[rows, _OFF_F:_OFF_F + D_B])
    return f, 1.0 - f, jnp.log(f) * LOG2E


def _lower_bound(lbp_ref):
    lbp = lbp_ref[...]
    lbe = jnp.exp(lbp - jnp.max(lbp, axis=0, keepdims=True))
    return lbe[0:1, :] / jnp.sum(lbe, axis=0, keepdims=True)


def _mixer_b_out(z_ref, o_ref, y_ref, hn, rows):
    for h, sl in enumerate(_HEADS):
        on = _rms(o_ref[rows, sl], hn[:, sl])
        gate = z_ref[rows, _OFF_G + h * HEAD:_OFF_G + (h + 1) * HEAD]
        y_ref[rows, D_A + h * HEAD:D_A + (h + 1) * HEAD] = (on * _silu(gate)).astype(BF16)


def _cast_rider(w, layer, nt):
    _, r, c = w.shape
    rows = r // CAST_STEPS
    assert rows * CAST_STEPS == r and rows % (2 * SUBLANES) == 0

    def blk(n, t):
        return jnp.minimum(n * nt + t, CAST_STEPS - 1)

    return (pl.BlockSpec((None, rows, c), lambda n, t: (layer, blk(n, t), 0)),
            pl.BlockSpec((rows, c), lambda n, t: (blk(n, t), 0)),
            jax.ShapeDtypeStruct((r, c), BF16))


def _run_cast_riders(nt, pairs):
    @pl.when(pl.program_id(0) * nt + pl.program_id(1) < CAST_STEPS)
    def _():
        for src, dst in pairs:
            dst[...] = src[...].astype(BF16)


def _mixer0_prompt_kernel(x_ref, gn_ref, win_ref, cw_ref, lbp_ref, hn_ref, wout_ref, sel_ref,
                          c0_ref, c1_ref,
                          xo_ref, convo_ref, so_ref, c0o_ref, c1o_ref,
                          hb_ref, z_ref, o_ref, y_ref, ext_ref, st_ref, g_ref, *, tl, chunk, nsplit):
    t = pl.program_id(1)
    _run_cast_riders(pl.num_programs(1), [(c0_ref, c0o_ref), (c1_ref, c1o_ref)])

    @pl.when(t == 0)
    def _():
        ext_ref[0:SUBLANES, :] = jnp.zeros((SUBLANES, D_A), F32)
        st_ref[...] = jnp.zeros((N_HEADS, HEAD, HEAD), F32)

    hb_ref[...] = _rms(x_ref[...], gn_ref[...]).astype(BF16)
    lb = _lower_bound(lbp_ref)
    cw = cw_ref[...]
    hn = hn_ref[...]
    part = tl // nsplit

    for s in range(nsplit):
        r0 = s * part
        rows = slice(r0, r0 + part)
        hb = hb_ref[rows, :]
        z_ref[rows, _OFF_Q:_OFF_G] = _dot(hb, win_ref[:, _OFF_Q:_OFF_G])

        for c0 in range(r0, r0 + part, chunk):
            crows = slice(c0, c0 + chunk)
            f, kk, lf2 = _gates(z_ref, crows, lb)
            amat, qg, x = _hgrn_intra(z_ref[crows, _OFF_Q:_OFF_Q + D_B], kk, f, lf2, sel_ref,
                                      g_ref.at[c0 // chunk], chunk)
            g_last = x[chunk:chunk + 1]
            dec = jnp.exp2(g_last)
            kg = (kk * jnp.exp2(g_last - x[0:chunk])).astype(BF16)
            qg = qg.astype(BF16)
            vb = z_ref[crows, _OFF_I:_OFF_I + D_B].astype(BF16)
            for h, sl in enumerate(_HEADS):
                st = st_ref[h]
                lhs = jnp.concatenate([qg[:, sl], amat[h].astype(BF16)], axis=1)
                rhs = jnp.concatenate([st.T.astype(BF16), vb[:, sl]], axis=0)
                o_ref[crows, sl] = _dot(lhs, rhs)
                st_ref[h] = dec[:, sl] * st + _dot_tn(vb[:, sl], kg[:, sl])

        z_ref[rows, 0:_OFF_Q] = _dot(hb, win_ref[:, 0:_OFF_Q])
        z_ref[rows, _OFF_G:D_IN] = _dot(hb, win_ref[:, _OFF_G:D_IN])

        u = z_ref[rows, _OFF_AC:_OFF_AC + D_A] * z_ref[rows, _OFF_AV:_OFF_AV + D_A]
        ext_ref[SUBLANES + r0:SUBLANES + r0 + part, :] = u
        conv = (_shifted(ext_ref, SUBLANES, part, 2, row0=r0) * cw[0:1, :]
                + _shifted(ext_ref, SUBLANES, part, 1, row0=r0) * cw[1:2, :] + u * cw[2:3, :])
        y_ref[rows, 0:D_A] = (z_ref[rows, _OFF_AB:_OFF_AB + D_A] * conv).astype(BF16)
        _mixer_b_out(z_ref, o_ref, y_ref, hn, rows)
        xo_ref[rows, :] = x_ref[rows, :] + _dot(y_ref[rows, :], wout_ref[...])

    carry = ext_ref[tl:tl + SUBLANES, :]
    ext_ref[0:SUBLANES, :] = carry

    @pl.when(t == pl.num_programs(1) - 1)
    def _():
        convo_ref[0] = carry
        for h in range(N_HEADS):
            so_ref[0, h] = st_ref[h].T


def _mixer0_sample_kernel(x_ref, e1_ref, e2_ref, s0_ref, gn_ref, win_ref, cw_ref, lbp_ref, hn_ref,
                          wout_ref, sel_ref,
                          xo_ref, uo_ref, so_ref,
                          z_ref, o_ref, y_ref, lf_ref, qg_ref, kg_ref, *, tl, seq):
    x = x_ref[...]
    z_ref[...] = _dot(_rms(x, gn_ref[...]).astype(BF16), win_ref[...])

    u = z_ref[:, _OFF_AC:_OFF_AC + D_A] * z_ref[:, _OFF_AV:_OFF_AV + D_A]
    cw = cw_ref[...]
    tpos = lax.broadcasted_iota(jnp.int32, (tl, 1), 0) & (seq - 1)
    u1 = jnp.where(tpos >= 1, pltpu.roll(u, 1, 0), e1_ref[...])
    u2 = jnp.where(tpos >= 2, pltpu.roll(u, 2, 0), e2_ref[...])
    uo_ref[...] = u
    conv = u2 * cw[0:1, :] + u1 * cw[1:2, :] + u * cw[2:3, :]
    y_ref[:, 0:D_A] = (z_ref[:, _OFF_AB:_OFF_AB + D_A] * conv).astype(BF16)

    f, kk, lf2 = _gates(z_ref, slice(None), _lower_bound(lbp_ref))
    amat, qg, xsel = _hgrn_intra(z_ref[:, _OFF_Q:_OFF_Q + D_B], kk, f, lf2, sel_ref, None, seq)
    v_all = z_ref[:, _OFF_I:_OFF_I + D_B].astype(BF16)
    o_ref[...] = jnp.concatenate(
        [_dot(amat[h].astype(BF16), v_all[:, sl]) for h, sl in enumerate(_HEADS)], axis=1)
    qg_ref[...] = qg
    kg_ref[...] = kk * jnp.exp2(xsel[tl:2 * tl])
    lf_ref[...] = lf2
    ones = jnp.ones((seq, HEAD), BF16)

    def seq_body(i, carry_unused):
        rows = pl.ds(pl.multiple_of(i * seq, seq), seq)
        hi, mid = _split_bf16(lf_ref[rows, :])
        qg_i = qg_ref[rows, :].astype(BF16)
        kg_i = kg_ref[rows, :].astype(BF16)
        vb = z_ref[rows, _OFF_I:_OFF_I + D_B].astype(BF16)
        for h, sl in enumerate(_HEADS):
            s0 = s0_ref[i, h]
            o_ref[rows, sl] = o_ref[rows, sl] + _dot(qg_i[:, sl], s0.astype(BF16))
            g_last = _dot_tn(hi[:, sl], ones) + _dot_tn(mid[:, sl], ones)
            so_ref[i, h] = jnp.exp2(g_last) * s0 + _dot_tn(kg_i[:, sl], vb[:, sl])
        return carry_unused

    lax.fori_loop(0, tl // seq, seq_body, 0, unroll=4)

    _mixer_b_out(z_ref, o_ref, y_ref, hn_ref[...], slice(None))
    xo_ref[...] = x + _dot(y_ref[...], wout_ref[...])


def _const_spec(shape):
    nd = len(shape)
    return pl.BlockSpec(shape, lambda *_: (0,) * nd, pipeline_mode=pl.Buffered(1))


def _layer_spec(shape, layer):
    nd = len(shape) - 1
    return pl.BlockSpec((None,) + tuple(shape[1:]), lambda *_: (layer,) + (0,) * nd,
                        pipeline_mode=pl.Buffered(1))


def _mixer0_prompt(x2d, n_seq, seq_len, gn, win, cw, lbp, hn, wout, cast, *, tl, nsplit):
    nt = seq_len // tl
    assert n_seq * nt >= CAST_STEPS
    sel = jnp.asarray(_selector(HGRN_CHUNK, HGRN_CHUNK), BF16)
    kern = functools.partial(_mixer0_prompt_kernel, tl=tl, chunk=HGRN_CHUNK, nsplit=nsplit)
    row_spec = pl.BlockSpec((tl, D_MODEL), lambda n, t: (n * nt + t, 0))
    consts = (gn, win, cw, lbp, hn, wout, sel)
    riders = [_cast_rider(w, layer, nt) for w, layer in cast]
    return pl.pallas_call(
        kern,
        grid=(n_seq, nt),
        in_specs=[row_spec] + [_const_spec(a.shape) for a in consts] + [r[0] for r in riders],
        out_specs=[row_spec,
                   pl.BlockSpec((1, SUBLANES, D_A), lambda n, t: (n, 0, 0)),
                   pl.BlockSpec((1, N_HEADS, HEAD, HEAD), lambda n, t: (n, 0, 0, 0))]
                  + [r[1] for r in riders],
        out_shape=[jax.ShapeDtypeStruct(x2d.shape, F32),
                   jax.ShapeDtypeStruct((n_seq, SUBLANES, D_A), F32),
                   jax.ShapeDtypeStruct((n_seq, N_HEADS, HEAD, HEAD), F32)]
                  + [r[2] for r in riders],
        scratch_shapes=[pltpu.VMEM((tl, D_MODEL), BF16),
                        pltpu.VMEM((tl, D_IN), F32),
                        pltpu.VMEM((tl, D_B), F32),
                        pltpu.VMEM((tl, D_A + D_B), BF16),
                        pltpu.VMEM((SUBLANES + tl, D_A), F32),
                        pltpu.VMEM((N_HEADS, HEAD, HEAD), F32),
                        pltpu.VMEM((tl // HGRN_CHUNK, HGRN_CHUNK, D_B), F32)],
        compiler_params=pltpu.CompilerParams(
            dimension_semantics=("arbitrary", "arbitrary"), vmem_limit_bytes=VMEM_LIMIT),
        name="mixer0_prompt",
    )(x2d, *consts, *[w for w, _ in cast])


def _mixer0_sample(x2d, e1, e2, s0, gn, win, cw, lbp, hn, wout, *, seq_len, bn):
    n_seq = s0.shape[0]
    tl = bn * seq_len
    sel = jnp.asarray(_selector(tl, seq_len), BF16)
    kern = functools.partial(_mixer0_sample_kernel, tl=tl, seq=seq_len)
    row_spec = pl.BlockSpec((tl, D_MODEL), lambda i: (i, 0))
    conv_spec = pl.BlockSpec((tl, D_A), lambda i: (i, 0))
    st_spec = pl.BlockSpec((bn, N_HEADS, HEAD, HEAD), lambda i: (i, 0, 0, 0))
    consts = (gn, win, cw, lbp, hn, wout, sel)
    return pl.pallas_call(
        kern,
        grid=(n_seq // bn,),
        in_specs=[row_spec, conv_spec, conv_spec, st_spec] + [_const_spec(a.shape) for a in consts],
        out_specs=[row_spec, conv_spec, st_spec],
        out_shape=[jax.ShapeDtypeStruct(x2d.shape, F32),
                   jax.ShapeDtypeStruct((x2d.shape[0], D_A), F32),
                   jax.ShapeDtypeStruct(s0.shape, F32)],
        scratch_shapes=[pltpu.VMEM((tl, D_IN), F32),
                        pltpu.VMEM((tl, D_B), F32),
                        pltpu.VMEM((tl, D_A + D_B), BF16),
                        pltpu.VMEM((tl, D_B), F32),
                        pltpu.VMEM((tl, D_B), F32),
                        pltpu.VMEM((tl, D_B), F32)],
        compiler_params=pltpu.CompilerParams(
            dimension_semantics=("arbitrary",), vmem_limit_bytes=VMEM_LIMIT),
        name="mixer0_sample",
    )(x2d, e1, e2, s0, *consts)


def _pool_mix(x, gm, ext_ref, padr, row0, ns, pos0, wp_ref, sc):
    n = x.shape[0]
    h = _rms(x, gm)
    ext_ref[padr + row0:padr + row0 + n, :] = h
    pos = pos0 + (lax.broadcasted_iota(jnp.int32, (n, 1), 0) >> _log2(ns))
    ys = []
    for gi, w in enumerate(POOL_WINDOWS):
        cols = slice(gi * POOL_GROUP, (gi + 1) * POOL_GROUP)
        hg = h[:, cols]
        if ns == 1:
            hist = _round_up(POOL_BUF, SUBLANES)
            s = ext_ref[pl.ds(padr + row0 - hist, n + hist), cols]
            sh = 1
            while sh < w:
                s = s + pltpu.roll(s, sh, 0)
                sh *= 2
            acc = s[hist:hist + n, :]
        else:
            acc = hg
            for k in range(1, w):
                acc = acc + _shifted(ext_ref, padr, n, k * ns, cols, row0)
        cnt = jnp.minimum(w, pos + 1).astype(F32)
        ys.append(_dot((acc / cnt - hg).astype(BF16), wp_ref[gi]))
    return x + jnp.concatenate(ys, axis=1) * sc


def _ffn_kernel(*refs, tl, ns, padr, final, pool, nsplit, n_cast, lookahead):
    refs = list(refs)
    x_ref, st_ref = refs[:2]
    del refs[:2]
    if lookahead:
        xn_ref = refs.pop(0)
    if pool is not None:
        pst_ref, gm_ref, wp_ref, sc_ref = refs[:4]
        del refs[:4]
        ppad, start = pool
    gn_ref, wgu_ref, cw_ref, cb_ref, wd_ref, gf_ref = refs[:6]
    del refs[:6]
    cast_in = refs[:n_cast]
    del refs[:n_cast]
    xo_ref, sto_ref = refs[:2]
    del refs[:2]
    if pool is not None:
        psto_ref = refs.pop(0)
    cast_out = refs[:n_cast]
    del refs[:n_cast]
    ext_ref = refs.pop(0)
    if pool is not None:
        pext_ref = refs.pop(0)
    if lookahead:
        hb_buf = refs.pop(0)
        if pool is not None:
            xm_buf = refs.pop(0)
    assert not refs
    t = pl.program_id(1)
    last = pl.num_programs(1) - 1
    part = tl // nsplit
    if n_cast:
        _run_cast_riders(pl.num_programs(1), list(zip(cast_in, cast_out)))

    def front(src_ref, tile, r0, anchor=None):
        zero = 0.0 if anchor is None else _dep_zero(anchor)
        x = src_ref[r0:r0 + part, :]
        if pool is not None:
            pos0 = start + ((tile * tl + r0) >> _log2(ns))
            x = _pool_mix(x, gm_ref[...] + zero, pext_ref, ppad, r0, ns, pos0, wp_ref, sc_ref[...])
        return x, _rms(x, gn_ref[...] + zero).astype(BF16)

    def front_tile(src_ref, tile, slot, anchors=None):
        for s in range(nsplit):
            rows = slice(s * part, (s + 1) * part)
            x, hb = front(src_ref, tile, s * part, None if anchors is None else anchors[s])
            hb_buf[slot, rows, :] = hb
            if pool is not None:
                xm_buf[slot, rows, :] = x
        if pool is not None:
            pext_ref[0:ppad, :] = pext_ref[tl:tl + ppad, :]

    @pl.when(t == 0)
    def _():
        ext_ref[0:padr, :] = st_ref[0]
        if pool is not None:
            pext_ref[0:ppad, :] = pst_ref[0]
        if lookahead:
            front_tile(x_ref, 0, 0)

    cw = cw_ref[...]
    cb = cb_ref[...]
    slot = t % 2
    anchors = []
    for s in range(nsplit):
        r0 = s * part
        rows = slice(r0, r0 + part)
        if lookahead:
            hb = hb_buf[slot, rows, :]
            x = xm_buf[slot, rows, :] if pool is not None else x_ref[rows, :]
        else:
            x, hb = front(x_ref, t, r0)
        g = _dot(hb, wgu_ref[:, 0:D_FF])
        anchors.append(g[0:1, 0:D_MODEL])
        ext_ref[padr + r0:padr + r0 + part, :] = g
        gc = (_shifted(ext_ref, padr, part, 2 * ns, row0=r0) * cw[0:1, :]
              + _shifted(ext_ref, padr, part, ns, row0=r0) * cw[1:2, :]
              + g * cw[2:3, :] + cb)
        up = _dot(hb, wgu_ref[:, D_FF:2 * D_FF])
        act = (_silu(gc) * up).astype(BF16)
        xo = x + _dot(act, wd_ref[...])
        if final:
            xo = _rms(xo, gf_ref[...])
        xo_ref[rows, :] = xo

    if lookahead:
        front_tile(xn_ref, jnp.minimum(t + 1, last), 1 - slot, anchors)
    elif pool is not None:
        pext_ref[0:ppad, :] = pext_ref[tl:tl + ppad, :]
    carry = ext_ref[tl:tl + padr, :]
    ext_ref[0:padr, :] = carry

    @pl.when(t == last)
    def _():
        sto_ref[0] = carry
        if pool is not None:
            psto_ref[0] = pext_ref[0:ppad, :]


def _ffn(x2d, state, p, layer, wgu, wd, *, tl, ns, final, nsplit=1, pool_state=None, start=0, cast=()):
    n_groups, padr, _ = state.shape
    rows_per_group = x2d.shape[0] // n_groups
    nt = rows_per_group // tl
    lookahead = pool_state is not None and nt > 1
    assert padr % SUBLANES == 0 and padr >= CONV_HIST * ns and (nt == 1 or tl >= padr)
    assert not cast or n_groups * nt >= CAST_STEPS
    row_spec = pl.BlockSpec((tl, D_MODEL), lambda n, t: (n * nt + t, 0))
    st_spec = pl.BlockSpec((1, padr, D_FF), lambda n, t: (n, 0, 0))
    ins = [x2d, state]
    in_specs = [row_spec, st_spec]
    if lookahead:
        ins.append(x2d)
        in_specs.append(pl.BlockSpec((tl, D_MODEL), lambda n, t: (n * nt + jnp.minimum(t + 1, nt - 1), 0)))
        in_specs[0] = pl.BlockSpec((tl, D_MODEL), lambda n, t: (n * nt, 0))
    out_specs = [row_spec, st_spec]
    out_shape = [jax.ShapeDtypeStruct(x2d.shape, F32), jax.ShapeDtypeStruct(state.shape, F32)]
    scratch = [pltpu.VMEM((padr + tl, D_FF), F32)]
    pool = None
    if pool_state is not None:
        ppad = pool_state.shape[1]
        assert ppad % SUBLANES == 0 and ppad >= POOL_BUF * ns and (nt == 1 or tl >= ppad)
        pool = (ppad, start)
        pst_spec = pl.BlockSpec((1, ppad, D_MODEL), lambda n, t: (n, 0, 0))
        pool_consts = (p["norm_mix"][layer], p["pool_w"], p["pool_scale"])
        ins += [pool_state, *pool_consts]
        in_specs += [pst_spec] + [_const_spec(a.shape) for a in pool_consts]
        out_specs.append(pst_spec)
        out_shape.append(jax.ShapeDtypeStruct(pool_state.shape, F32))
        scratch.append(pltpu.VMEM((ppad + tl, D_MODEL), F32))
    ins += [p["norm_ffn"], wgu, p["ffn_cw"], p["ffn_cb"], wd, p["norm_final"]]
    in_specs += [_layer_spec(p["norm_ffn"].shape, layer), _const_spec(wgu.shape),
                 _layer_spec(p["ffn_cw"].shape, layer), _layer_spec(p["ffn_cb"].shape, layer),
                 _const_spec(wd.shape), _const_spec(p["norm_final"].shape)]
    riders = [_cast_rider(w, l, nt) for w, l in cast]
    ins += [w for w, _ in cast]
    in_specs += [r[0] for r in riders]
    out_specs += [r[1] for r in riders]
    out_shape += [r[2] for r in riders]
    if lookahead:
        scratch.append(pltpu.VMEM((2, tl, D_MODEL), BF16))
        if pool is not None:
            scratch.append(pltpu.VMEM((2, tl, D_MODEL), F32))
    kern = functools.partial(_ffn_kernel, tl=tl, ns=ns, padr=padr, final=final, pool=pool,
                             nsplit=nsplit, n_cast=len(cast), lookahead=lookahead)
    return pl.pallas_call(
        kern,
        grid=(n_groups, nt),
        in_specs=in_specs,
        out_specs=out_specs,
        out_shape=out_shape,
        scratch_shapes=scratch,
        compiler_params=pltpu.CompilerParams(
            dimension_semantics=("arbitrary", "arbitrary"), vmem_limit_bytes=VMEM_LIMIT),
        name=("pool_" if pool else "") + ("ffn_final" if final else "ffn"),
    )(*ins)


def _pool_kernel(x_ref, st_ref, gn_ref, wp_ref, sc_ref, xo_ref, sto_ref, ext_ref,
                 *, tl, ns, padr, start):
    t = pl.program_id(1)

    @pl.when(t == 0)
    def _():
        ext_ref[0:padr, :] = st_ref[0]

    pos0 = start + ((t * tl) >> _log2(ns))
    xo_ref[...] = _pool_mix(x_ref[...], gn_ref[...], ext_ref, padr, 0, ns, pos0, wp_ref, sc_ref[...])

    carry = ext_ref[tl:tl + padr, :]
    ext_ref[0:padr, :] = carry

    @pl.when(t == pl.num_programs(1) - 1)
    def _():
        sto_ref[0] = carry


def _pool(x2d, state, gn, wp, sc, *, tl, ns, start):
    n_groups, padr, _ = state.shape
    rows_per_group = x2d.shape[0] // n_groups
    nt = rows_per_group // tl
    assert padr % SUBLANES == 0 and padr >= POOL_BUF * ns and (nt == 1 or tl >= padr)
    kern = functools.partial(_pool_kernel, tl=tl, ns=ns, padr=padr, start=start)
    row_spec = pl.BlockSpec((tl, D_MODEL), lambda n, t: (n * nt + t, 0))
    st_spec = pl.BlockSpec((1, padr, D_MODEL), lambda n, t: (n, 0, 0))
    return pl.pallas_call(
        kern,
        grid=(n_groups, nt),
        in_specs=[row_spec, st_spec, _const_spec(gn.shape), _const_spec(wp.shape),
                  _const_spec(sc.shape)],
        out_specs=[row_spec, st_spec],
        out_shape=[jax.ShapeDtypeStruct(x2d.shape, F32),
                   jax.ShapeDtypeStruct(state.shape, F32)],
        scratch_shapes=[pltpu.VMEM((padr + tl, D_MODEL), F32)],
        compiler_params=pltpu.CompilerParams(
            dimension_semantics=("arbitrary", "arbitrary"), vmem_limit_bytes=VMEM_LIMIT),
        name="pool",
    )(x2d, state, gn, wp, sc)


def kernel(x_prompt, x_sample, state_conv_a, state_hgrn, state_pool, state_ffn, norm_mix, norm_ffn,
           norm_final, w_in, conv_a_w, hgrn_lower_bounds, hgrn_norm, w_out, pool_w, pool_scale,
           ffn_w_gu, ffn_conv_w, ffn_conv_b, ffn_w_down):
    batch, seq, _ = x_prompt.shape
    dec_batch, dec_seq, _ = x_sample.shape

    p = dict(
        norm_mix=norm_mix[:, None, :], norm_ffn=norm_ffn[:, None, :], norm_final=norm_final[None, :],
        ffn_cw=ffn_conv_w, ffn_cb=ffn_conv_b[:, None, :], pool_w=pool_w[0].astype(BF16),
        pool_scale=pool_scale[0][None, :])
    win = w_in[0].astype(BF16)
    wout = w_out[0].astype(BF16)
    mix_args = (p["norm_mix"][0], win, conv_a_w[0], hgrn_lower_bounds, hgrn_norm[0][None, :], wout)

    xp = x_prompt.reshape(batch * seq, D_MODEL)
    x1, conv_p, hgrn_p, wgu0, wd0 = _mixer0_prompt(
        xp, batch, seq, *mix_args, [(ffn_w_gu, 0), (ffn_w_down, 0)], tl=MIXER_TILE, nsplit=MIXER_PARTS)
    pad_ffn = _round_up(CONV_HIST, SUBLANES)
    pad_pool = _round_up(POOL_BUF, SUBLANES)
    zf = jnp.zeros((batch, pad_ffn, D_FF), F32)
    zp = jnp.zeros((batch, pad_pool, D_MODEL), F32)
    x2, ffn0_p, wgu1, wd1 = _ffn(x1, zf, p, 0, wgu0, wd0, tl=FFN_TILE, ns=1, final=False,
                                 nsplit=FFN_PARTS,
                                 cast=[(ffn_w_gu, 1), (ffn_w_down, 1)])
    yp, ffn1_p, pool_p = _ffn(x2, zf, p, 1, wgu1, wd1, tl=POOL_FFN_TILE, ns=1, final=True,
                              nsplit=POOL_FFN_PARTS, pool_state=zp, start=0)
    y_prompt = yp.reshape(batch, seq, D_MODEL)
    conv_prompt = conv_p[None, :, SUBLANES - CONV_HIST:, :]
    hgrn_prompt = hgrn_p[None]
    pool_prompt = pool_p[None, :, pad_pool - POOL_BUF:, :]
    ffn_prompt = jnp.stack([ffn0_p, ffn1_p])[:, :, pad_ffn - CONV_HIST:, :]

    xs = x_sample.reshape(dec_batch * dec_seq, D_MODEL)
    cs = state_conv_a[0]
    zeros_tail = jnp.zeros((dec_batch, dec_seq - 1, D_A), F32)
    e1 = jnp.concatenate([cs[:, 1:2], zeros_tail], axis=1).reshape(dec_batch * dec_seq, D_A)
    e2 = jnp.concatenate([cs, zeros_tail[:, 1:]], axis=1).reshape(dec_batch * dec_seq, D_A)
    x1s, us, hgrn_s = _mixer0_sample(xs, e1, e2, state_hgrn[0], *mix_args, seq_len=dec_seq, bn=SAMPLE_SEQS_PER_TILE)
    conv_sample = us.reshape(dec_batch, dec_seq, D_A)[None, :, dec_seq - CONV_HIST:, :]

    def to_tm(a):
        return a.transpose(1, 0, 2).reshape(1, a.shape[1] * dec_batch, a.shape[2])

    def from_tm(a, rows):
        return a.reshape(rows, dec_batch, a.shape[-1]).transpose(1, 0, 2)

    x1t = to_tm(x1s.reshape(dec_batch, dec_seq, D_MODEL))[0]
    x2t, ffn0_s = _ffn(x1t, to_tm(state_ffn[0]), p, 0, wgu0, wd0, tl=SAMPLE_FFN_TILE, ns=dec_batch,
                       final=False, nsplit=SAMPLE_FFN_PARTS)
    x3t, pool_s = _pool(x2t, to_tm(state_pool[0]), p["norm_mix"][1], p["pool_w"], p["pool_scale"],
                        tl=dec_batch * dec_seq, ns=dec_batch, start=PAST_LEN)
    ys, ffn1_s = _ffn(x3t, to_tm(state_ffn[1]), p, 1, wgu1, wd1, tl=SAMPLE_FFN_TILE, ns=dec_batch,
                      final=True, nsplit=SAMPLE_FFN_PARTS)
    y_sample = from_tm(ys, dec_seq)
    pool_sample = from_tm(pool_s[0], POOL_BUF)[None]
    ffn_sample = jnp.stack([from_tm(ffn0_s[0], CONV_HIST), from_tm(ffn1_s[0], CONV_HIST)])

    return (y_prompt, y_sample, conv_prompt, conv_sample, hgrn_prompt, hgrn_s[None],
            pool_prompt, pool_sample, ffn_prompt, ffn_sample)
```

```python
import functools

import jax
import jax.numpy as jnp
import numpy as np
from jax import lax
from jax.experimental import pallas as pl
from jax.experimental.pallas import tpu as pltpu

D_MODEL = 1024
D_A = 512
N_HEADS = 4
HEAD = 128
D_B = N_HEADS * HEAD
D_IN = 3 * D_A + 4 * D_B
D_FF = 2816
POOL_WINDOWS = (2, 4, 8, 16)
POOL_GROUP = D_MODEL // len(POOL_WINDOWS)
POOL_BUF = max(POOL_WINDOWS) - 1
CONV_HIST = 2
PAST_LEN = 16384
EPS = 1e-6
LOG2E = 1.4426950408889634

SUBLANES = 8
HGRN_CHUNK = 128
VMEM_LIMIT = 56 * 1024 * 1024
CAST_STEPS = 16
MIXER_TILE, MIXER_PARTS = 1024, 4
FFN_TILE, FFN_PARTS = 1024, 4
POOL_FFN_TILE, POOL_FFN_PARTS = 512, 2
SAMPLE_FFN_TILE, SAMPLE_FFN_PARTS = 512, 2
SAMPLE_SEQS_PER_TILE = 16

F32 = jnp.float32
BF16 = jnp.bfloat16

_OFF_AC, _OFF_AB, _OFF_AV = 0, D_A, 2 * D_A
_OFF_Q, _OFF_F, _OFF_I, _OFF_G = (3 * D_A, 3 * D_A + D_B, 3 * D_A + 2 * D_B, 3 * D_A + 3 * D_B)
_HEADS = [slice(h * HEAD, (h + 1) * HEAD) for h in range(N_HEADS)]


def _round_up(x, m):
    return (x + m - 1) // m * m


def _log2(x):
    assert x & (x - 1) == 0
    return x.bit_length() - 1


def _rms(x, g):
    return x * lax.rsqrt(jnp.mean(x * x, axis=-1, keepdims=True) + EPS) * g


def _silu(x):
    return x * jax.nn.sigmoid(x)


def _dot(a, b):
    return jnp.dot(a, b, preferred_element_type=F32)


def _dot_nt(a, b):
    return lax.dot_general(a, b, (((1,), (1,)), ((), ())), preferred_element_type=F32)


def _dot_tn(a, b):
    return lax.dot_general(a, b, (((0,), (0,)), ((), ())), preferred_element_type=F32)


def _dep_zero(a):
    u = lax.bitcast_convert_type(a, jnp.int32)
    return lax.shift_right_logical(lax.shift_right_logical(u, 16), 16).astype(F32)


def _shifted(ext_ref, padr, n, k_rows, cols=slice(None), row0=0):
    off = padr + row0 - k_rows
    assert off >= 0
    base = off // SUBLANES * SUBLANES
    rem = off - base
    if rem == 0:
        return ext_ref[pl.ds(off, n), cols]
    win = ext_ref[pl.ds(base, n + SUBLANES), cols]
    return pltpu.roll(win, SUBLANES - rem, 0)[SUBLANES:SUBLANES + n, :]


def _levels(seg):
    return [1 << i for i in range(_log2(seg))]


def _selector_levels(seg):
    return [b for b in _levels(seg) if 1 < b < SUBLANES]


def _selector(c, seg):
    t = np.arange(c)[:, None]
    u = np.arange(c)[None, :]
    same_seg = (t // seg) == (u // seg)
    blocks = [same_seg & (u <= t)]
    blocks.append(same_seg & (u > t) if seg < c else np.ones((SUBLANES, c), bool))
    for b in _selector_levels(seg):
        pair = (t // (2 * b)) == (u // (2 * b))
        anchor = (t // (2 * b)) * 2 * b + b - 1
        right = (t // b) % 2 == 1
        blocks.append(pair & np.where(right, (u > anchor) & (u <= t), (u > t) & (u <= anchor)))
    sel = np.concatenate(blocks, axis=0).astype(np.float32)
    return np.concatenate([sel, sel], axis=1)


def _split_bf16(a):
    hi = a.astype(BF16)
    return hi, (a - hi.astype(F32)).astype(BF16)


def _hgrn_intra(q, kk, f, lf2, sel_ref, g_ref, seg):
    c = q.shape[0]
    x = _dot(sel_ref[...], jnp.concatenate(_split_bf16(lf2), axis=0))
    g = x[0:c]

    row = lax.broadcasted_iota(jnp.int32, (c, 1), 0)
    rowi = lax.broadcasted_iota(jnp.int32, (c, c), 0)
    coli = lax.broadcasted_iota(jnp.int32, (c, c), 1)
    diff = rowi ^ coli
    lower = rowi > coli

    p0 = q * kk
    p1 = q * pltpu.roll(kk, 1, 0) * f
    amat = []
    for sl in _HEADS:
        a0 = jnp.sum(p0[:, sl], axis=-1, keepdims=True)
        a1 = jnp.sum(p1[:, sl], axis=-1, keepdims=True)
        amat.append(jnp.where(rowi == coli, a0, jnp.where(lower & (diff == 1), a1, 0.0)))

    sel_blocks = {b: x[off:off + c] for b, off in
                  zip(_selector_levels(seg), range(c + (c if seg < c else SUBLANES), x.shape[0], c))}
    if seg > SUBLANES:
        g_ref[...] = g
    for b in _levels(seg)[1:]:
        if b in sel_blocks:
            ex = sel_blocks[b]
        else:
            parts = []
            for pair in range(c // (2 * b)):
                anchor = g_ref[pl.ds(pair * 2 * b + b - 1, 1), :]
                parts.append(g[pair * 2 * b:(pair + 1) * 2 * b, :] - anchor)
            ex = -jnp.abs(parts[0] if len(parts) == 1 else jnp.concatenate(parts, axis=0))
        w = jnp.exp2(ex)
        right = ((row >> _log2(b)) & 1) == 1
        ql = jnp.where(right, q * w, 0.0).astype(BF16)
        kl = jnp.where(right, 0.0, kk * w).astype(BF16)
        mask = lower & ((diff >> _log2(b)) == 1)
        for h, sl in enumerate(_HEADS):
            amat[h] = jnp.where(mask, _dot_nt(ql[:, sl], kl[:, sl]), amat[h])
    return amat, q * jnp.exp2(g), x


def _gates(z_ref, rows, lb):
    f = lb + (1.0 - lb) * jax.nn.sigmoid(z_ref[rows, _OFF_F:_OFF_F + D_B])
    return f, 1.0 - f, jnp.log(f) * LOG2E


def _lower_bound(lbp_ref):
    lbp = lbp_ref[...]
    lbe = jnp.exp(lbp - jnp.max(lbp, axis=0, keepdims=True))
    return lbe[0:1, :] / jnp.sum(lbe, axis=0, keepdims=True)


def _mixer_b_out(z_ref, o_ref, y_ref, hn, rows):
    for h, sl in enumerate(_HEADS):
        on = _rms(o_ref[rows, sl], hn[:, sl])
        gate = z_ref[rows, _OFF_G + h * HEAD:_OFF_G + (h + 1) * HEAD]
        y_ref[rows, D_A + h * HEAD:D_A + (h + 1) * HEAD] = (on * _silu(gate)).astype(BF16)


def _cast_rider(w, layer, nt):
    _, r, c = w.shape
    rows = r // CAST_STEPS
    assert rows * CAST_STEPS == r and rows % (2 * SUBLANES) == 0

    def blk(n, t):
        return jnp.minimum(n * nt + t, CAST_STEPS - 1)

    return (pl.BlockSpec((None, rows, c), lambda n, t: (layer, blk(n, t), 0)),
            pl.BlockSpec((rows, c), lambda n, t: (blk(n, t), 0)),
            jax.ShapeDtypeStruct((r, c), BF16))


def _run_cast_riders(nt, pairs):
    @pl.when(pl.program_id(0) * nt + pl.program_id(1) < CAST_STEPS)
    def _():
        for src, dst in pairs:
            dst[...] = src[...].astype(BF16)


def _mixer0_prompt_kernel(x_ref, gn_ref, win_ref, cw_ref, lbp_ref, hn_ref, wout_ref, sel_ref,
                          c0_ref, c1_ref,
                          xo_ref, convo_ref, so_ref, c0o_ref, c1o_ref,
                          hb_ref, z_ref, o_ref, y_ref, ext_ref, st_ref, g_ref, *, tl, chunk, nsplit):
    t = pl.program_id(1)
    _run_cast_riders(pl.num_programs(1), [(c0_ref, c0o_ref), (c1_ref, c1o_ref)])

    @pl.when(t == 0)
    def _():
        ext_ref[0:SUBLANES, :] = jnp.zeros((SUBLANES, D_A), F32)
        st_ref[...] = jnp.zeros((N_HEADS, HEAD, HEAD), F32)

    hb_ref[...] = _rms(x_ref[...], gn_ref[...]).astype(BF16)
    lb = _lower_bound(lbp_ref)
    cw = cw_ref[...]
    hn = hn_ref[...]
    part = tl // nsplit

    for s in range(nsplit):
        r0 = s * part
        rows = slice(r0, r0 + part)
        hb = hb_ref[rows, :]
        z_ref[rows, _OFF_Q:_OFF_G] = _dot(hb, win_ref[:, _OFF_Q:_OFF_G])

        for c0 in range(r0, r0 + part, chunk):
            crows = slice(c0, c0 + chunk)
            f, kk, lf2 = _gates(z_ref, crows, lb)
            amat, qg, x = _hgrn_intra(z_ref[crows, _OFF_Q:_OFF_Q + D_B], kk, f, lf2, sel_ref,
                                      g_ref.at[c0 // chunk], chunk)
            g_last = x[chunk:chunk + 1]
            dec = jnp.exp2(g_last)
            kg = (kk * jnp.exp2(g_last - x[0:chunk])).astype(BF16)
            qg = qg.astype(BF16)
            vb = z_ref[crows, _OFF_I:_OFF_I + D_B].astype(BF16)
            for h, sl in enumerate(_HEADS):
                st = st_ref[h]
                lhs = jnp.concatenate([qg[:, sl], amat[h].astype(BF16)], axis=1)
                rhs = jnp.concatenate([st.T.astype(BF16), vb[:, sl]], axis=0)
                o_ref[crows, sl] = _dot(lhs, rhs)
                st_ref[h] = dec[:, sl] * st + _dot_tn(vb[:, sl], kg[:, sl])

        z_ref[rows, 0:_OFF_Q] = _dot(hb, win_ref[:, 0:_OFF_Q])
        z_ref[rows, _OFF_G:D_IN] = _dot(hb, win_ref[:, _OFF_G:D_IN])

        u = z_ref[rows, _OFF_AC:_OFF_AC + D_A] * z_ref[rows, _OFF_AV:_OFF_AV + D_A]
        ext_ref[SUBLANES + r0:SUBLANES + r0 + part, :] = u
        conv = (_shifted(ext_ref, SUBLANES, part, 2, row0=r0) * cw[0:1, :]
                + _shifted(ext_ref, SUBLANES, part, 1, row0=r0) * cw[1:2, :] + u * cw[2:3, :])
        y_ref[rows, 0:D_A] = (z_ref[rows, _OFF_AB:_OFF_AB + D_A] * conv).astype(BF16)
        _mixer_b_out(z_ref, o_ref, y_ref, hn, rows)
        xo_ref[rows, :] = x_ref[rows, :] + _dot(y_ref[rows, :], wout_ref[...])

    carry = ext_ref[tl:tl + SUBLANES, :]
    ext_ref[0:SUBLANES, :] = carry

    @pl.when(t == pl.num_programs(1) - 1)
    def _():
        convo_ref[0] = carry
        for h in range(N_HEADS):
            so_ref[0, h] = st_ref[h].T


def _mixer0_sample_kernel(x_ref, e1_ref, e2_ref, s0_ref, gn_ref, win_ref, cw_ref, lbp_ref, hn_ref,
                          wout_ref, sel_ref,
                          xo_ref, uo_ref, so_ref,
                          z_ref, o_ref, y_ref, dec_ref, qg_ref, kg_ref, *, tl, seq):
    x = x_ref[...]
    z_ref[...] = _dot(_rms(x, gn_ref[...]).astype(BF16), win_ref[...])

    u = z_ref[:, _OFF_AC:_OFF_AC + D_A] * z_ref[:, _OFF_AV:_OFF_AV + D_A]
    cw = cw_ref[...]
    tpos = lax.broadcasted_iota(jnp.int32, (tl, 1), 0) & (seq - 1)
    u1 = jnp.where(tpos >= 1, pltpu.roll(u, 1, 0), e1_ref[...])
    u2 = jnp.where(tpos >= 2, pltpu.roll(u, 2, 0), e2_ref[...])
    uo_ref[...] = u
    conv = u2 * cw[0:1, :] + u1 * cw[1:2, :] + u * cw[2:3, :]
    y_ref[:, 0:D_A] = (z_ref[:, _OFF_AB:_OFF_AB + D_A] * conv).astype(BF16)

    f, kk, lf2 = _gates(z_ref, slice(None), _lower_bound(lbp_ref))
    amat, qg, xsel = _hgrn_intra(z_ref[:, _OFF_Q:_OFF_Q + D_B], kk, f, lf2, sel_ref, None, seq)
    v_all = z_ref[:, _OFF_I:_OFF_I + D_B].astype(BF16)
    o_ref[...] = jnp.concatenate(
        [_dot(amat[h].astype(BF16), v_all[:, sl]) for h, sl in enumerate(_HEADS)], axis=1)
    qg_ref[...] = qg
    kg_ref[...] = kk * jnp.exp2(xsel[tl:2 * tl])
    hi, mid = _split_bf16(lf2)
    owner = (lax.broadcasted_iota(jnp.int32, (tl, HEAD), 0) >> _log2(seq)
             == lax.broadcasted_iota(jnp.int32, (tl, HEAD), 1)).astype(BF16)
    dec_ref[...] = jnp.exp2(_dot_tn(hi, owner) + _dot_tn(mid, owner))

    for i in range(tl // seq):
        rows = slice(i * seq, (i + 1) * seq)
        qg_i = qg_ref[rows, :].astype(BF16)
        kg_i = kg_ref[rows, :].astype(BF16)
        vb = z_ref[rows, _OFF_I:_OFF_I + D_B].astype(BF16)
        for h, sl in enumerate(_HEADS):
            s0 = s0_ref[i, h]
            o_ref[rows, sl] = o_ref[rows, sl] + _dot(qg_i[:, sl], s0.astype(BF16))
            so_ref[i, h] = dec_ref[sl, :][:, i:i + 1] * s0 + _dot_tn(kg_i[:, sl], vb[:, sl])

    _mixer_b_out(z_ref, o_ref, y_ref, hn_ref[...], slice(None))
    xo_ref[...] = x + _dot(y_ref[...], wout_ref[...])


def _const_spec(shape):
    nd = len(shape)
    return pl.BlockSpec(shape, lambda *_: (0,) * nd, pipeline_mode=pl.Buffered(1))


def _layer_spec(shape, layer):
    nd = len(shape) - 1
    return pl.BlockSpec((None,) + tuple(shape[1:]), lambda *_: (layer,) + (0,) * nd,
                        pipeline_mode=pl.Buffered(1))


def _mixer0_prompt(x2d, n_seq, seq_len, gn, win, cw, lbp, hn, wout, cast, *, tl, nsplit):
    nt = seq_len // tl
    assert n_seq * nt >= CAST_STEPS
    sel = jnp.asarray(_selector(HGRN_CHUNK, HGRN_CHUNK), BF16)
    kern = functools.partial(_mixer0_prompt_kernel, tl=tl, chunk=HGRN_CHUNK, nsplit=nsplit)
    row_spec = pl.BlockSpec((tl, D_MODEL), lambda n, t: (n * nt + t, 0))
    consts = (gn, win, cw, lbp, hn, wout, sel)
    riders = [_cast_rider(w, layer, nt) for w, layer in cast]
    return pl.pallas_call(
        kern,
        grid=(n_seq, nt),
        in_specs=[row_spec] + [_const_spec(a.shape) for a in consts] + [r[0] for r in riders],
        out_specs=[row_spec,
                   pl.BlockSpec((1, SUBLANES, D_A), lambda n, t: (n, 0, 0)),
                   pl.BlockSpec((1, N_HEADS, HEAD, HEAD), lambda n, t: (n, 0, 0, 0))]
                  + [r[1] for r in riders],
        out_shape=[jax.ShapeDtypeStruct(x2d.shape, F32),
                   jax.ShapeDtypeStruct((n_seq, SUBLANES, D_A), F32),
                   jax.ShapeDtypeStruct((n_seq, N_HEADS, HEAD, HEAD), F32)]
                  + [r[2] for r in riders],
        scratch_shapes=[pltpu.VMEM((tl, D_MODEL), BF16),
                        pltpu.VMEM((tl, D_IN), F32),
                        pltpu.VMEM((tl, D_B), F32),
                        pltpu.VMEM((tl, D_A + D_B), BF16),
                        pltpu.VMEM((SUBLANES + tl, D_A), F32),
                        pltpu.VMEM((N_HEADS, HEAD, HEAD), F32),
                        pltpu.VMEM((tl // HGRN_CHUNK, HGRN_CHUNK, D_B), F32)],
        compiler_params=pltpu.CompilerParams(
            dimension_semantics=("arbitrary", "arbitrary"), vmem_limit_bytes=VMEM_LIMIT),
        name="mixer0_prompt",
    )(x2d, *consts, *[w for w, _ in cast])


def _mixer0_sample(x2d, e1, e2, s0, gn, win, cw, lbp, hn, wout, *, seq_len, bn):
    n_seq = s0.shape[0]
    tl = bn * seq_len
    sel = jnp.asarray(_selector(tl, seq_len), BF16)
    kern = functools.partial(_mixer0_sample_kernel, tl=tl, seq=seq_len)
    row_spec = pl.BlockSpec((tl, D_MODEL), lambda i: (i, 0))
    conv_spec = pl.BlockSpec((tl, D_A), lambda i: (i, 0))
    st_spec = pl.BlockSpec((bn, N_HEADS, HEAD, HEAD), lambda i: (i, 0, 0, 0))
    consts = (gn, win, cw, lbp, hn, wout, sel)
    return pl.pallas_call(
        kern,
        grid=(n_seq // bn,),
        in_specs=[row_spec, conv_spec, conv_spec, st_spec] + [_const_spec(a.shape) for a in consts],
        out_specs=[row_spec, conv_spec, st_spec],
        out_shape=[jax.ShapeDtypeStruct(x2d.shape, F32),
                   jax.ShapeDtypeStruct((x2d.shape[0], D_A), F32),
                   jax.ShapeDtypeStruct(s0.shape, F32)],
        scratch_shapes=[pltpu.VMEM((tl, D_IN), F32),
                        pltpu.VMEM((tl, D_B), F32),
                        pltpu.VMEM((tl, D_A + D_B), BF16),
                        pltpu.VMEM((D_B, HEAD), F32),
                        pltpu.VMEM((tl, D_B), F32),
                        pltpu.VMEM((tl, D_B), F32)],
        compiler_params=pltpu.CompilerParams(
            dimension_semantics=("arbitrary",), vmem_limit_bytes=VMEM_LIMIT),
        name="mixer0_sample",
    )(x2d, e1, e2, s0, *consts)


def _pool_mix(x, gm, ext_ref, padr, row0, ns, pos0, wp_ref, sc):
    n = x.shape[0]
    h = _rms(x, gm)
    ext_ref[padr + row0:padr + row0 + n, :] = h
    pos = pos0 + (lax.broadcasted_iota(jnp.int32, (n, 1), 0) >> _log2(ns))
    ys = []
    for gi, w in enumerate(POOL_WINDOWS):
        cols = slice(gi * POOL_GROUP, (gi + 1) * POOL_GROUP)
        hg = h[:, cols]
        if ns == 1:
            hist = _round_up(POOL_BUF, SUBLANES)
            s = ext_ref[pl.ds(padr + row0 - hist, n + hist), cols]
            sh = 1
            while sh < w:
                s = s + pltpu.roll(s, sh, 0)
                sh *= 2
            acc = s[hist:hist + n, :]
        else:
            acc = hg
            for k in range(1, w):
                acc = acc + _shifted(ext_ref, padr, n, k * ns, cols, row0)
        cnt = jnp.minimum(w, pos + 1).astype(F32)
        ys.append(_dot((acc / cnt - hg).astype(BF16), wp_ref[gi]))
    return x + jnp.concatenate(ys, axis=1) * sc


def _ffn_kernel(*refs, tl, ns, padr, final, pool, nsplit, n_cast, lookahead):
    refs = list(refs)
    x_ref, st_ref = refs[:2]
    del refs[:2]
    if lookahead:
        xn_ref = refs.pop(0)
    if pool is not None:
        pst_ref, gm_ref, wp_ref, sc_ref = refs[:4]
        del refs[:4]
        ppad, start = pool
    gn_ref, wgu_ref, cw_ref, cb_ref, wd_ref, gf_ref = refs[:6]
    del refs[:6]
    cast_in = refs[:n_cast]
    del refs[:n_cast]
    xo_ref, sto_ref = refs[:2]
    del refs[:2]
    if pool is not None:
        psto_ref = refs.pop(0)
    cast_out = refs[:n_cast]
    del refs[:n_cast]
    ext_ref = refs.pop(0)
    if pool is not None:
        pext_ref = refs.pop(0)
    if lookahead:
        hb_buf = refs.pop(0)
        if pool is not None:
            xm_buf = refs.pop(0)
    assert not refs
    t = pl.program_id(1)
    last = pl.num_programs(1) - 1
    part = tl // nsplit
    if n_cast:
        _run_cast_riders(pl.num_programs(1), list(zip(cast_in, cast_out)))

    def front(src_ref, tile, r0, anchor=None):
        zero = 0.0 if anchor is None else _dep_zero(anchor)
        x = src_ref[r0:r0 + part, :]
        if pool is not None:
            pos0 = start + ((tile * tl + r0) >> _log2(ns))
            x = _pool_mix(x, gm_ref[...] + zero, pext_ref, ppad, r0, ns, pos0, wp_ref, sc_ref[...])
        return x, _rms(x, gn_ref[...] + zero).astype(BF16)

    def front_tile(src_ref, tile, slot, anchors=None):
        for s in range(nsplit):
            rows = slice(s * part, (s + 1) * part)
            x, hb = front(src_ref, tile, s * part, None if anchors is None else anchors[s])
            hb_buf[slot, rows, :] = hb
            if pool is not None:
                xm_buf[slot, rows, :] = x
        if pool is not None:
            pext_ref[0:ppad, :] = pext_ref[tl:tl + ppad, :]

    @pl.when(t == 0)
    def _():
        ext_ref[0:padr, :] = st_ref[0]
        if pool is not None:
            pext_ref[0:ppad, :] = pst_ref[0]
        if lookahead:
            front_tile(x_ref, 0, 0)

    cw = cw_ref[...]
    cb = cb_ref[...]
    slot = t % 2
    anchors = []
    for s in range(nsplit):
        r0 = s * part
        rows = slice(r0, r0 + part)
        if lookahead:
            hb = hb_buf[slot, rows, :]
            x = xm_buf[slot, rows, :] if pool is not None else x_ref[rows, :]
        else:
            x, hb = front(x_ref, t, r0)
        g = _dot(hb, wgu_ref[:, 0:D_FF])
        anchors.append(g[0:1, 0:D_MODEL])
        ext_ref[padr + r0:padr + r0 + part, :] = g
        gc = (_shifted(ext_ref, padr, part, 2 * ns, row0=r0) * cw[0:1, :]
              + _shifted(ext_ref, padr, part, ns, row0=r0) * cw[1:2, :]
              + g * cw[2:3, :] + cb)
        up = _dot(hb, wgu_ref[:, D_FF:2 * D_FF])
        act = (_silu(gc) * up).astype(BF16)
        xo = x + _dot(act, wd_ref[...])
        if final:
            xo = _rms(xo, gf_ref[...])
        xo_ref[rows, :] = xo

    if lookahead:
        front_tile(xn_ref, jnp.minimum(t + 1, last), 1 - slot, anchors)
    elif pool is not None:
        pext_ref[0:ppad, :] = pext_ref[tl:tl + ppad, :]
    carry = ext_ref[tl:tl + padr, :]
    ext_ref[0:padr, :] = carry

    @pl.when(t == last)
    def _():
        sto_ref[0] = carry
        if pool is not None:
            psto_ref[0] = pext_ref[0:ppad, :]


def _ffn(x2d, state, p, layer, wgu, wd, *, tl, ns, final, nsplit=1, pool_state=None, start=0, cast=()):
    n_groups, padr, _ = state.shape
    rows_per_group = x2d.shape[0] // n_groups
    nt = rows_per_group // tl
    lookahead = pool_state is not None and nt > 1
    assert padr % SUBLANES == 0 and padr >= CONV_HIST * ns and (nt == 1 or tl >= padr)
    assert not cast or n_groups * nt >= CAST_STEPS
    row_spec = pl.BlockSpec((tl, D_MODEL), lambda n, t: (n * nt + t, 0))
    st_spec = pl.BlockSpec((1, padr, D_FF), lambda n, t: (n, 0, 0))
    ins = [x2d, state]
    in_specs = [row_spec, st_spec]
    if lookahead:
        ins.append(x2d)
        in_specs.append(pl.BlockSpec((tl, D_MODEL), lambda n, t: (n * nt + jnp.minimum(t + 1, nt - 1), 0)))
        in_specs[0] = pl.BlockSpec((tl, D_MODEL), lambda n, t: (n * nt, 0))
    out_specs = [row_spec, st_spec]
    out_shape = [jax.ShapeDtypeStruct(x2d.shape, F32), jax.ShapeDtypeStruct(state.shape, F32)]
    scratch = [pltpu.VMEM((padr + tl, D_FF), F32)]
    pool = None
    if pool_state is not None:
        ppad = pool_state.shape[1]
        assert ppad % SUBLANES == 0 and ppad >= POOL_BUF * ns and (nt == 1 or tl >= ppad)
        pool = (ppad, start)
        pst_spec = pl.BlockSpec((1, ppad, D_MODEL), lambda n, t: (n, 0, 0))
        pool_consts = (p["norm_mix"][layer], p["pool_w"], p["pool_scale"])
        ins += [pool_state, *pool_consts]
        in_specs += [pst_spec] + [_const_spec(a.shape) for a in pool_consts]
        out_specs.append(pst_spec)
        out_shape.append(jax.ShapeDtypeStruct(pool_state.shape, F32))
        scratch.append(pltpu.VMEM((ppad + tl, D_MODEL), F32))
    ins += [p["norm_ffn"], wgu, p["ffn_cw"], p["ffn_cb"], wd, p["norm_final"]]
    in_specs += [_layer_spec(p["norm_ffn"].shape, layer), _const_spec(wgu.shape),
                 _layer_spec(p["ffn_cw"].shape, layer), _layer_spec(p["ffn_cb"].shape, layer),
                 _const_spec(wd.shape), _const_spec(p["norm_final"].shape)]
    riders = [_cast_rider(w, l, nt) for w, l in cast]
    ins += [w for w, _ in cast]
    in_specs += [r[0] for r in riders]
    out_specs += [r[1] for r in riders]
    out_shape += [r[2] for r in riders]
    if lookahead:
        scratch.append(pltpu.VMEM((2, tl, D_MODEL), BF16))
        if pool is not None:
            scratch.append(pltpu.VMEM((2, tl, D_MODEL), F32))
    kern = functools.partial(_ffn_kernel, tl=tl, ns=ns, padr=padr, final=final, pool=pool,
                             nsplit=nsplit, n_cast=len(cast), lookahead=lookahead)
    return pl.pallas_call(
        kern,
        grid=(n_groups, nt),
        in_specs=in_specs,
        out_specs=out_specs,
        out_shape=out_shape,
        scratch_shapes=scratch,
        compiler_params=pltpu.CompilerParams(
            dimension_semantics=("arbitrary", "arbitrary"), vmem_limit_bytes=VMEM_LIMIT),
        name=("pool_" if pool else "") + ("ffn_final" if final else "ffn"),
    )(*ins)


def _pool_kernel(x_ref, st_ref, gn_ref, wp_ref, sc_ref, xo_ref, sto_ref, ext_ref,
                 *, tl, ns, padr, start):
    t = pl.program_id(1)

    @pl.when(t == 0)
    def _():
        ext_ref[0:padr, :] = st_ref[0]

    pos0 = start + ((t * tl) >> _log2(ns))
    xo_ref[...] = _pool_mix(x_ref[...], gn_ref[...], ext_ref, padr, 0, ns, pos0, wp_ref, sc_ref[...])

    carry = ext_ref[tl:tl + padr, :]
    ext_ref[0:padr, :] = carry

    @pl.when(t == pl.num_programs(1) - 1)
    def _():
        sto_ref[0] = carry


def _pool(x2d, state, gn, wp, sc, *, tl, ns, start):
    n_groups, padr, _ = state.shape
    rows_per_group = x2d.shape[0] // n_groups
    nt = rows_per_group // tl
    assert padr % SUBLANES == 0 and padr >= POOL_BUF * ns and (nt == 1 or tl >= padr)
    kern = functools.partial(_pool_kernel, tl=tl, ns=ns, padr=padr, start=start)
    row_spec = pl.BlockSpec((tl, D_MODEL), lambda n, t: (n * nt + t, 0))
    st_spec = pl.BlockSpec((1, padr, D_MODEL), lambda n, t: (n, 0, 0))
    return pl.pallas_call(
        kern,
        grid=(n_groups, nt),
        in_specs=[row_spec, st_spec, _const_spec(gn.shape), _const_spec(wp.shape),
                  _const_spec(sc.shape)],
        out_specs=[row_spec, st_spec],
        out_shape=[jax.ShapeDtypeStruct(x2d.shape, F32),
                   jax.ShapeDtypeStruct(state.shape, F32)],
        scratch_shapes=[pltpu.VMEM((padr + tl, D_MODEL), F32)],
        compiler_params=pltpu.CompilerParams(
            dimension_semantics=("arbitrary", "arbitrary"), vmem_limit_bytes=VMEM_LIMIT),
        name="pool",
    )(x2d, state, gn, wp, sc)


def kernel(x_prompt, x_sample, state_conv_a, state_hgrn, state_pool, state_ffn, norm_mix, norm_ffn,
           norm_final, w_in, conv_a_w, hgrn_lower_bounds, hgrn_norm, w_out, pool_w, pool_scale,
           ffn_w_gu, ffn_conv_w, ffn_conv_b, ffn_w_down):
    batch, seq, _ = x_prompt.shape
    dec_batch, dec_seq, _ = x_sample.shape

    p = dict(
        norm_mix=norm_mix[:, None, :], norm_ffn=norm_ffn[:, None, :], norm_final=norm_final[None, :],
        ffn_cw=ffn_conv_w, ffn_cb=ffn_conv_b[:, None, :], pool_w=pool_w[0].astype(BF16),
        pool_scale=pool_scale[0][None, :])
    win = w_in[0].astype(BF16)
    wout = w_out[0].astype(BF16)
    mix_args = (p["norm_mix"][0], win, conv_a_w[0], hgrn_lower_bounds, hgrn_norm[0][None, :], wout)

    xp = x_prompt.reshape(batch * seq, D_MODEL)
    x1, conv_p, hgrn_p, wgu0, wd0 = _mixer0_prompt(
        xp, batch, seq, *mix_args, [(ffn_w_gu, 0), (ffn_w_down, 0)], tl=MIXER_TILE, nsplit=MIXER_PARTS)
    pad_ffn = _round_up(CONV_HIST, SUBLANES)
    pad_pool = _round_up(POOL_BUF, SUBLANES)
    zf = jnp.zeros((batch, pad_ffn, D_FF), F32)
    zp = jnp.zeros((batch, pad_pool, D_MODEL), F32)
    x2, ffn0_p, wgu1, wd1 = _ffn(x1, zf, p, 0, wgu0, wd0, tl=FFN_TILE, ns=1, final=False,
                                 nsplit=FFN_PARTS,
                                 cast=[(ffn_w_gu, 1), (ffn_w_down, 1)])
    yp, ffn1_p, pool_p = _ffn(x2, zf, p, 1, wgu1, wd1, tl=POOL_FFN_TILE, ns=1, final=True,
                              nsplit=POOL_FFN_PARTS, pool_state=zp, start=0)
    y_prompt = yp.reshape(batch, seq, D_MODEL)
    conv_prompt = conv_p[None, :, SUBLANES - CONV_HIST:, :]
    hgrn_prompt = hgrn_p[None]
    pool_prompt = pool_p[None, :, pad_pool - POOL_BUF:, :]
    ffn_prompt = jnp.stack([ffn0_p, ffn1_p])[:, :, pad_ffn - CONV_HIST:, :]

    xs = x_sample.reshape(dec_batch * dec_seq, D_MODEL)
    cs = state_conv_a[0]
    zeros_tail = jnp.zeros((dec_batch, dec_seq - 1, D_A), F32)
    e1 = jnp.concatenate([cs[:, 1:2], zeros_tail], axis=1).reshape(dec_batch * dec_seq, D_A)
    e2 = jnp.concatenate([cs, zeros_tail[:, 1:]], axis=1).reshape(dec_batch * dec_seq, D_A)
    x1s, us, hgrn_s = _mixer0_sample(xs, e1, e2, state_hgrn[0], *mix_args, seq_len=dec_seq, bn=SAMPLE_SEQS_PER_TILE)
    conv_sample = us.reshape(dec_batch, dec_seq, D_A)[None, :, dec_seq - CONV_HIST:, :]

    def to_tm(a):
        return a.transpose(1, 0, 2).reshape(1, a.shape[1] * dec_batch, a.shape[2])

    def from_tm(a, rows):
        return a.reshape(rows, dec_batch, a.shape[-1]).transpose(1, 0, 2)

    x1t = to_tm(x1s.reshape(dec_batch, dec_seq, D_MODEL))[0]
    x2t, ffn0_s = _ffn(x1t, to_tm(state_ffn[0]), p, 0, wgu0, wd0, tl=SAMPLE_FFN_TILE, ns=dec_batch,
                       final=False, nsplit=SAMPLE_FFN_PARTS)
    x3t, pool_s = _pool(x2t, to_tm(state_pool[0]), p["norm_mix"][1], p["pool_w"], p["pool_scale"],
                        tl=dec_batch * dec_seq, ns=dec_batch, start=PAST_LEN)
    ys, ffn1_s = _ffn(x3t, to_tm(state_ffn[1]), p, 1, wgu1, wd1, tl=SAMPLE_FFN_TILE, ns=dec_batch,
                      final=True, nsplit=SAMPLE_FFN_PARTS)
    y_sample = from_tm(ys, dec_seq)
    pool_sample = from_tm(pool_s[0], POOL_BUF)[None]
    ffn_sample = jnp.stack([from_tm(ffn0_s[0], CONV_HIST), from_tm(ffn1_s[0], CONV_HIST)])

    return (y_prompt, y_sample, conv_prompt, conv_sample, hgrn_prompt, hgrn_s[None],
            pool_prompt, pool_sample, ffn_prompt, ffn_sample)
```

```python
import functools

import jax
import jax.numpy as jnp
import numpy as np
from jax import lax
from jax.experimental import pallas as pl
from jax.experimental.pallas import tpu as pltpu

D_MODEL = 1024
D_A = 512
N_HEADS = 4
HEAD = 128
D_B = N_HEADS * HEAD
D_IN = 3 * D_A + 4 * D_B
D_FF = 2816
POOL_WINDOWS = (2, 4, 8, 16)
POOL_GROUP = D_MODEL // len(POOL_WINDOWS)
POOL_BUF = max(POOL_WINDOWS) - 1
CONV_HIST = 2
PAST_LEN = 16384
EPS = 1e-6
LOG2E = 1.4426950408889634

SUBLANES = 8
HGRN_CHUNK = 128
VMEM_LIMIT = 56 * 1024 * 1024
CAST_STEPS = 16
MIXER_TILE, MIXER_PARTS = 1024, 4
FFN_TILE, FFN_PARTS = 1024, 4
POOL_FFN_TILE, POOL_FFN_PARTS = 512, 2
SAMPLE_FFN_TILE, SAMPLE_FFN_PARTS = 512, 2
SAMPLE_SEQS_PER_TILE = 32
SAMPLE_INTRA_ROWS = 128

F32 = jnp.float32
BF16 = jnp.bfloat16

_OFF_AC, _OFF_AB, _OFF_AV = 0, D_A, 2 * D_A
_OFF_Q, _OFF_F, _OFF_I, _OFF_G = (3 * D_A, 3 * D_A + D_B, 3 * D_A + 2 * D_B, 3 * D_A + 3 * D_B)
_HEADS = [slice(h * HEAD, (h + 1) * HEAD) for h in range(N_HEADS)]


def _round_up(x, m):
    return (x + m - 1) // m * m


def _log2(x):
    assert x & (x - 1) == 0
    return x.bit_length() - 1


def _rms(x, g):
    return x * lax.rsqrt(jnp.mean(x * x, axis=-1, keepdims=True) + EPS) * g


def _silu(x):
    return x * jax.nn.sigmoid(x)


def _dot(a, b):
    return jnp.dot(a, b, preferred_element_type=F32)


def _dot_nt(a, b):
    return lax.dot_general(a, b, (((1,), (1,)), ((), ())), preferred_element_type=F32)


def _dot_tn(a, b):
    return lax.dot_general(a, b, (((0,), (0,)), ((), ())), preferred_element_type=F32)


def _dep_zero(a):
    u = lax.bitcast_convert_type(a, jnp.int32)
    return lax.shift_right_logical(lax.shift_right_logical(u, 16), 16).astype(F32)


def _shifted(ext_ref, padr, n, k_rows, cols=slice(None), row0=0):
    off = padr + row0 - k_rows
    assert off >= 0
    base = off // SUBLANES * SUBLANES
    rem = off - base
    if rem == 0:
        return ext_ref[pl.ds(off, n), cols]
    win = ext_ref[pl.ds(base, n + SUBLANES), cols]
    return pltpu.roll(win, SUBLANES - rem, 0)[SUBLANES:SUBLANES + n, :]


def _levels(seg):
    return [1 << i for i in range(_log2(seg))]


def _selector_levels(seg):
    return [b for b in _levels(seg) if 1 < b < SUBLANES]


def _selector(c, seg):
    t = np.arange(c)[:, None]
    u = np.arange(c)[None, :]
    same_seg = (t // seg) == (u // seg)
    blocks = [same_seg & (u <= t)]
    blocks.append(same_seg & (u > t) if seg < c else np.ones((SUBLANES, c), bool))
    for b in _selector_levels(seg):
        pair = (t // (2 * b)) == (u // (2 * b))
        anchor = (t // (2 * b)) * 2 * b + b - 1
        right = (t // b) % 2 == 1
        blocks.append(pair & np.where(right, (u > anchor) & (u <= t), (u > t) & (u <= anchor)))
    sel = np.concatenate(blocks, axis=0).astype(np.float32)
    return np.concatenate([sel, sel], axis=1)


def _split_bf16(a):
    hi = a.astype(BF16)
    return hi, (a - hi.astype(F32)).astype(BF16)


def _hgrn_intra(q, kk, f, lf2, sel_ref, g_ref, seg):
    c = q.shape[0]
    x = _dot(sel_ref[...], jnp.concatenate(_split_bf16(lf2), axis=0))
    g = x[0:c]

    row = lax.broadcasted_iota(jnp.int32, (c, 1), 0)
    rowi = lax.broadcasted_iota(jnp.int32, (c, c), 0)
    coli = lax.broadcasted_iota(jnp.int32, (c, c), 1)
    diff = rowi ^ coli
    lower = rowi > coli

    p0 = q * kk
    p1 = q * pltpu.roll(kk, 1, 0) * f
    amat = []
    for sl in _HEADS:
        a0 = jnp.sum(p0[:, sl], axis=-1, keepdims=True)
        a1 = jnp.sum(p1[:, sl], axis=-1, keepdims=True)
        amat.append(jnp.where(rowi == coli, a0, jnp.where(lower & (diff == 1), a1, 0.0)))

    sel_blocks = {b: x[off:off + c] for b, off in
                  zip(_selector_levels(seg), range(c + (c if seg < c else SUBLANES), x.shape[0], c))}
    if seg > SUBLANES:
        g_ref[...] = g
    for b in _levels(seg)[1:]:
        if b in sel_blocks:
            ex = sel_blocks[b]
        else:
            parts = []
            for pair in range(c // (2 * b)):
                anchor = g_ref[pl.ds(pair * 2 * b + b - 1, 1), :]
                parts.append(g[pair * 2 * b:(pair + 1) * 2 * b, :] - anchor)
            ex = -jnp.abs(parts[0] if len(parts) == 1 else jnp.concatenate(parts, axis=0))
        w = jnp.exp2(ex)
        right = ((row >> _log2(b)) & 1) == 1
        ql = jnp.where(right, q * w, 0.0).astype(BF16)
        kl = jnp.where(right, 0.0, kk * w).astype(BF16)
        mask = lower & ((diff >> _log2(b)) == 1)
        for h, sl in enumerate(_HEADS):
            amat[h] = jnp.where(mask, _dot_nt(ql[:, sl], kl[:, sl]), amat[h])
    return amat, q * jnp.exp2(g), x


def _gates(z_ref, rows, lb):
    f = lb + (1.0 - lb) * jax.nn.sigmoid(z_ref[rows, _OFF_F:_OFF_F + D_B])
    return f, 1.0 - f, jnp.log(f) * LOG2E


def _lower_bound(lbp_ref):
    lbp = lbp_ref[...]
    lbe = jnp.exp(lbp - jnp.max(lbp, axis=0, keepdims=True))
    return lbe[0:1, :] / jnp.sum(lbe, axis=0, keepdims=True)


def _mixer_b_out(z_ref, o_ref, y_ref, hn, rows):
    for h, sl in enumerate(_HEADS):
        on = _rms(o_ref[rows, sl], hn[:, sl])
        gate = z_ref[rows, _OFF_G + h * HEAD:_OFF_G + (h + 1) * HEAD]
        y_ref[rows, D_A + h * HEAD:D_A + (h + 1) * HEAD] = (on * _silu(gate)).astype(BF16)


def _cast_rider(w, layer, nt):
    _, r, c = w.shape
    rows = r // CAST_STEPS
    assert rows * CAST_STEPS == r and rows % (2 * SUBLANES) == 0

    def blk(n, t):
        return jnp.minimum(n * nt + t, CAST_STEPS - 1)

    return (pl.BlockSpec((None, rows, c), lambda n, t: (layer, blk(n, t), 0)),
            pl.BlockSpec((rows, c), lambda n, t: (blk(n, t), 0)),
            jax.ShapeDtypeStruct((r, c), BF16))


def _run_cast_riders(nt, pairs):
    @pl.when(pl.program_id(0) * nt + pl.program_id(1) < CAST_STEPS)
    def _():
        for src, dst in pairs:
            dst[...] = src[...].astype(BF16)


def _mixer0_prompt_kernel(x_ref, gn_ref, win_ref, cw_ref, lbp_ref, hn_ref, wout_ref, sel_ref,
                          c0_ref, c1_ref,
                          xo_ref, convo_ref, so_ref, c0o_ref, c1o_ref,
                          hb_ref, z_ref, o_ref, y_ref, ext_ref, st_ref, g_ref, *, tl, chunk, nsplit):
    t = pl.program_id(1)
    _run_cast_riders(pl.num_programs(1), [(c0_ref, c0o_ref), (c1_ref, c1o_ref)])

    @pl.when(t == 0)
    def _():
        ext_ref[0:SUBLANES, :] = jnp.zeros((SUBLANES, D_A), F32)
        st_ref[...] = jnp.zeros((N_HEADS, HEAD, HEAD), F32)

    hb_ref[...] = _rms(x_ref[...], gn_ref[...]).astype(BF16)
    lb = _lower_bound(lbp_ref)
    cw = cw_ref[...]
    hn = hn_ref[...]
    part = tl // nsplit

    for s in range(nsplit):
        r0 = s * part
        rows = slice(r0, r0 + part)
        hb = hb_ref[rows, :]
        z_ref[rows, _OFF_Q:_OFF_G] = _dot(hb, win_ref[:, _OFF_Q:_OFF_G])

        for c0 in range(r0, r0 + part, chunk):
            crows = slice(c0, c0 + chunk)
            f, kk, lf2 = _gates(z_ref, crows, lb)
            amat, qg, x = _hgrn_intra(z_ref[crows, _OFF_Q:_OFF_Q + D_B], kk, f, lf2, sel_ref,
                                      g_ref.at[c0 // chunk], chunk)
            g_last = x[chunk:chunk + 1]
            dec = jnp.exp2(g_last)
            kg = (kk * jnp.exp2(g_last - x[0:chunk])).astype(BF16)
            qg = qg.astype(BF16)
            vb = z_ref[crows, _OFF_I:_OFF_I + D_B].astype(BF16)
            for h, sl in enumerate(_HEADS):
                st = st_ref[h]
                lhs = jnp.concatenate([qg[:, sl], amat[h].astype(BF16)], axis=1)
                rhs = jnp.concatenate([st.T.astype(BF16), vb[:, sl]], axis=0)
                o_ref[crows, sl] = _dot(lhs, rhs)
                st_ref[h] = dec[:, sl] * st + _dot_tn(vb[:, sl], kg[:, sl])

        z_ref[rows, 0:_OFF_Q] = _dot(hb, win_ref[:, 0:_OFF_Q])
        z_ref[rows, _OFF_G:D_IN] = _dot(hb, win_ref[:, _OFF_G:D_IN])

        u = z_ref[rows, _OFF_AC:_OFF_AC + D_A] * z_ref[rows, _OFF_AV:_OFF_AV + D_A]
        ext_ref[SUBLANES + r0:SUBLANES + r0 + part, :] = u
        conv = (_shifted(ext_ref, SUBLANES, part, 2, row0=r0) * cw[0:1, :]
                + _shifted(ext_ref, SUBLANES, part, 1, row0=r0) * cw[1:2, :] + u * cw[2:3, :])
        y_ref[rows, 0:D_A] = (z_ref[rows, _OFF_AB:_OFF_AB + D_A] * conv).astype(BF16)
        _mixer_b_out(z_ref, o_ref, y_ref, hn, rows)
        xo_ref[rows, :] = x_ref[rows, :] + _dot(y_ref[rows, :], wout_ref[...])

    carry = ext_ref[tl:tl + SUBLANES, :]
    ext_ref[0:SUBLANES, :] = carry

    @pl.when(t == pl.num_programs(1) - 1)
    def _():
        convo_ref[0] = carry
        for h in range(N_HEADS):
            so_ref[0, h] = st_ref[h].T


def _mixer0_sample_kernel(x_ref, e2_ref, s0_ref, gn_ref, win_ref, cw_ref, lbp_ref, hn_ref,
                          wout_ref, sel_ref,
                          xo_ref, uo_ref, so_ref,
                          z_ref, o_ref, y_ref, dec_ref, qg_ref, kg_ref, *, tl, seq, chunk):
    x = x_ref[...]
    z_ref[...] = _dot(_rms(x, gn_ref[...]).astype(BF16), win_ref[...])

    u = z_ref[:, _OFF_AC:_OFF_AC + D_A] * z_ref[:, _OFF_AV:_OFF_AV + D_A]
    cw = cw_ref[...]
    e2 = e2_ref[...]
    tpos = lax.broadcasted_iota(jnp.int32, (tl, 1), 0) & (seq - 1)
    u1 = jnp.where(tpos >= 1, pltpu.roll(u, 1, 0), pltpu.roll(e2, tl - 1, 0))
    u2 = jnp.where(tpos >= 2, pltpu.roll(u, 2, 0), e2)
    uo_ref[...] = u
    conv = u2 * cw[0:1, :] + u1 * cw[1:2, :] + u * cw[2:3, :]
    y_ref[:, 0:D_A] = (z_ref[:, _OFF_AB:_OFF_AB + D_A] * conv).astype(BF16)

    lb = _lower_bound(lbp_ref)
    seqs_per_chunk = chunk // seq
    owner = (lax.broadcasted_iota(jnp.int32, (chunk, HEAD), 0) >> _log2(seq)
             == lax.broadcasted_iota(jnp.int32, (chunk, HEAD), 1)).astype(BF16)
    for ci in range(tl // chunk):
        crows = slice(ci * chunk, (ci + 1) * chunk)
        f, kk, lf2 = _gates(z_ref, crows, lb)
        amat, qg, xsel = _hgrn_intra(z_ref[crows, _OFF_Q:_OFF_Q + D_B], kk, f, lf2, sel_ref, None, seq)
        v_c = z_ref[crows, _OFF_I:_OFF_I + D_B].astype(BF16)
        o_ref[crows, :] = jnp.concatenate(
            [_dot(amat[h].astype(BF16), v_c[:, sl]) for h, sl in enumerate(_HEADS)], axis=1)
        qg_ref[crows, :] = qg
        kg_ref[crows, :] = kk * jnp.exp2(xsel[chunk:2 * chunk])
        hi, mid = _split_bf16(lf2)
        dec_ref[ci] = jnp.exp2(_dot_tn(hi, owner) + _dot_tn(mid, owner))

    for i in range(tl // seq):
        rows = slice(i * seq, (i + 1) * seq)
        ci, col = divmod(i, seqs_per_chunk)
        qg_i = qg_ref[rows, :].astype(BF16)
        kg_i = kg_ref[rows, :].astype(BF16)
        vb = z_ref[rows, _OFF_I:_OFF_I + D_B].astype(BF16)
        for h, sl in enumerate(_HEADS):
            s0 = s0_ref[i, h]
            o_ref[rows, sl] = o_ref[rows, sl] + _dot(qg_i[:, sl], s0.astype(BF16))
            so_ref[i, h] = dec_ref[ci, sl, :][:, col:col + 1] * s0 + _dot_tn(kg_i[:, sl], vb[:, sl])

    _mixer_b_out(z_ref, o_ref, y_ref, hn_ref[...], slice(None))
    xo_ref[...] = x + _dot(y_ref[...], wout_ref[...])


def _const_spec(shape):
    nd = len(shape)
    return pl.BlockSpec(shape, lambda *_: (0,) * nd, pipeline_mode=pl.Buffered(1))


def _layer_spec(shape, layer):
    nd = len(shape) - 1
    return pl.BlockSpec((None,) + tuple(shape[1:]), lambda *_: (layer,) + (0,) * nd,
                        pipeline_mode=pl.Buffered(1))


def _mixer0_prompt(x2d, n_seq, seq_len, gn, win, cw, lbp, hn, wout, cast, *, tl, nsplit):
    nt = seq_len // tl
    assert n_seq * nt >= CAST_STEPS
    sel = jnp.asarray(_selector(HGRN_CHUNK, HGRN_CHUNK), BF16)
    kern = functools.partial(_mixer0_prompt_kernel, tl=tl, chunk=HGRN_CHUNK, nsplit=nsplit)
    row_spec = pl.BlockSpec((tl, D_MODEL), lambda n, t: (n * nt + t, 0))
    consts = (gn, win, cw, lbp, hn, wout, sel)
    riders = [_cast_rider(w, layer, nt) for w, layer in cast]
    return pl.pallas_call(
        kern,
        grid=(n_seq, nt),
        in_specs=[row_spec] + [_const_spec(a.shape) for a in consts] + [r[0] for r in riders],
        out_specs=[row_spec,
                   pl.BlockSpec((1, SUBLANES, D_A), lambda n, t: (n, 0, 0)),
                   pl.BlockSpec((1, N_HEADS, HEAD, HEAD), lambda n, t: (n, 0, 0, 0))]
                  + [r[1] for r in riders],
        out_shape=[jax.ShapeDtypeStruct(x2d.shape, F32),
                   jax.ShapeDtypeStruct((n_seq, SUBLANES, D_A), F32),
                   jax.ShapeDtypeStruct((n_seq, N_HEADS, HEAD, HEAD), F32)]
                  + [r[2] for r in riders],
        scratch_shapes=[pltpu.VMEM((tl, D_MODEL), BF16),
                        pltpu.VMEM((tl, D_IN), F32),
                        pltpu.VMEM((tl, D_B), F32),
                        pltpu.VMEM((tl, D_A + D_B), BF16),
                        pltpu.VMEM((SUBLANES + tl, D_A), F32),
                        pltpu.VMEM((N_HEADS, HEAD, HEAD), F32),
                        pltpu.VMEM((tl // HGRN_CHUNK, HGRN_CHUNK, D_B), F32)],
        compiler_params=pltpu.CompilerParams(
            dimension_semantics=("arbitrary", "arbitrary"), vmem_limit_bytes=VMEM_LIMIT),
        name="mixer0_prompt",
    )(x2d, *consts, *[w for w, _ in cast])


def _mixer0_sample(x2d, e2, s0, gn, win, cw, lbp, hn, wout, *, seq_len, bn):
    n_seq = s0.shape[0]
    tl = bn * seq_len
    chunk = min(tl, SAMPLE_INTRA_ROWS)
    sel = jnp.asarray(_selector(chunk, seq_len), BF16)
    kern = functools.partial(_mixer0_sample_kernel, tl=tl, seq=seq_len, chunk=chunk)
    row_spec = pl.BlockSpec((tl, D_MODEL), lambda i: (i, 0))
    conv_spec = pl.BlockSpec((tl, D_A), lambda i: (i, 0))
    st_spec = pl.BlockSpec((bn, N_HEADS, HEAD, HEAD), lambda i: (i, 0, 0, 0))
    consts = (gn, win, cw, lbp, hn, wout, sel)
    return pl.pallas_call(
        kern,
        grid=(n_seq // bn,),
        in_specs=[row_spec, conv_spec, st_spec] + [_const_spec(a.shape) for a in consts],
        out_specs=[row_spec, conv_spec, st_spec],
        out_shape=[jax.ShapeDtypeStruct(x2d.shape, F32),
                   jax.ShapeDtypeStruct((x2d.shape[0], D_A), F32),
                   jax.ShapeDtypeStruct(s0.shape, F32)],
        scratch_shapes=[pltpu.VMEM((tl, D_IN), F32),
                        pltpu.VMEM((tl, D_B), F32),
                        pltpu.VMEM((tl, D_A + D_B), BF16),
                        pltpu.VMEM((tl // chunk, D_B, HEAD), F32),
                        pltpu.VMEM((tl, D_B), F32),
                        pltpu.VMEM((tl, D_B), F32)],
        compiler_params=pltpu.CompilerParams(
            dimension_semantics=("arbitrary",), vmem_limit_bytes=VMEM_LIMIT),
        name="mixer0_sample",
    )(x2d, e2, s0, *consts)


def _pool_mix(x, gm, ext_ref, padr, row0, ns, pos0, wp_ref, sc):
    n = x.shape[0]
    h = _rms(x, gm)
    ext_ref[padr + row0:padr + row0 + n, :] = h
    pos = pos0 + (lax.broadcasted_iota(jnp.int32, (n, 1), 0) >> _log2(ns))
    ys = []
    for gi, w in enumerate(POOL_WINDOWS):
        cols = slice(gi * POOL_GROUP, (gi + 1) * POOL_GROUP)
        hg = h[:, cols]
        if ns == 1:
            hist = _round_up(POOL_BUF, SUBLANES)
            s = ext_ref[pl.ds(padr + row0 - hist, n + hist), cols]
            sh = 1
            while sh < w:
                s = s + pltpu.roll(s, sh, 0)
                sh *= 2
            acc = s[hist:hist + n, :]
        else:
            acc = hg
            for k in range(1, w):
                acc = acc + _shifted(ext_ref, padr, n, k * ns, cols, row0)
        cnt = jnp.minimum(w, pos + 1).astype(F32)
        ys.append(_dot((acc / cnt - hg).astype(BF16), wp_ref[gi]))
    return x + jnp.concatenate(ys, axis=1) * sc


def _ffn_kernel(*refs, tl, ns, padr, final, pool, nsplit, n_cast, lookahead):
    refs = list(refs)
    x_ref, st_ref = refs[:2]
    del refs[:2]
    if lookahead:
        xn_ref = refs.pop(0)
    if pool is not None:
        pst_ref, gm_ref, wp_ref, sc_ref = refs[:4]
        del refs[:4]
        ppad, start = pool
    gn_ref, wgu_ref, cw_ref, cb_ref, wd_ref, gf_ref = refs[:6]
    del refs[:6]
    cast_in = refs[:n_cast]
    del refs[:n_cast]
    xo_ref, sto_ref = refs[:2]
    del refs[:2]
    if pool is not None:
        psto_ref = refs.pop(0)
    cast_out = refs[:n_cast]
    del refs[:n_cast]
    ext_ref = refs.pop(0)
    if pool is not None:
        pext_ref = refs.pop(0)
    if lookahead:
        hb_buf = refs.pop(0)
        if pool is not None:
            xm_buf = refs.pop(0)
    assert not refs
    t = pl.program_id(1)
    last = pl.num_programs(1) - 1
    part = tl // nsplit
    if n_cast:
        _run_cast_riders(pl.num_programs(1), list(zip(cast_in, cast_out)))

    def front(src_ref, tile, r0, anchor=None):
        zero = 0.0 if anchor is None else _dep_zero(anchor)
        x = src_ref[r0:r0 + part, :]
        if pool is not None:
            pos0 = start + ((tile * tl + r0) >> _log2(ns))
            x = _pool_mix(x, gm_ref[...] + zero, pext_ref, ppad, r0, ns, pos0, wp_ref, sc_ref[...])
        return x, _rms(x, gn_ref[...] + zero).astype(BF16)

    def front_tile(src_ref, tile, slot, anchors=None):
        for s in range(nsplit):
            rows = slice(s * part, (s + 1) * part)
            x, hb = front(src_ref, tile, s * part, None if anchors is None else anchors[s])
            hb_buf[slot, rows, :] = hb
            if pool is not None:
                xm_buf[slot, rows, :] = x
        if pool is not None:
            pext_ref[0:ppad, :] = pext_ref[tl:tl + ppad, :]

    @pl.when(t == 0)
    def _():
        ext_ref[0:padr, :] = st_ref[0]
        if pool is not None:
            pext_ref[0:ppad, :] = pst_ref[0]
        if lookahead:
            front_tile(x_ref, 0, 0)

    cw = cw_ref[...]
    cb = cb_ref[...]
    slot = t % 2
    anchors = []
    for s in range(nsplit):
        r0 = s * part
        rows = slice(r0, r0 + part)
        if lookahead:
            hb = hb_buf[slot, rows, :]
            x = xm_buf[slot, rows, :] if pool is not None else x_ref[rows, :]
        else:
            x, hb = front(x_ref, t, r0)
        g = _dot(hb, wgu_ref[:, 0:D_FF])
        anchors.append(g[0:1, 0:D_MODEL])
        ext_ref[padr + r0:padr + r0 + part, :] = g
        gc = (_shifted(ext_ref, padr, part, 2 * ns, row0=r0) * cw[0:1, :]
              + _shifted(ext_ref, padr, part, ns, row0=r0) * cw[1:2, :]
              + g * cw[2:3, :] + cb)
        up = _dot(hb, wgu_ref[:, D_FF:2 * D_FF])
        act = (_silu(gc) * up).astype(BF16)
        xo = x + _dot(act, wd_ref[...])
        if final:
            xo = _rms(xo, gf_ref[...])
        xo_ref[rows, :] = xo

    if lookahead:
        front_tile(xn_ref, jnp.minimum(t + 1, last), 1 - slot, anchors)
    elif pool is not None:
        pext_ref[0:ppad, :] = pext_ref[tl:tl + ppad, :]
    carry = ext_ref[tl:tl + padr, :]
    ext_ref[0:padr, :] = carry

    @pl.when(t == last)
    def _():
        sto_ref[0] = carry
        if pool is not None:
            psto_ref[0] = pext_ref[0:ppad, :]


def _ffn(x2d, state, p, layer, wgu, wd, *, tl, ns, final, nsplit=1, pool_state=None, start=0, cast=()):
    n_groups, padr, _ = state.shape
    rows_per_group = x2d.shape[0] // n_groups
    nt = rows_per_group // tl
    lookahead = pool_state is not None and nt > 1
    assert padr % SUBLANES == 0 and padr >= CONV_HIST * ns and (nt == 1 or tl >= padr)
    assert not cast or n_groups * nt >= CAST_STEPS
    row_spec = pl.BlockSpec((tl, D_MODEL), lambda n, t: (n * nt + t, 0))
    st_spec = pl.BlockSpec((1, padr, D_FF), lambda n, t: (n, 0, 0))
    ins = [x2d, state]
    in_specs = [row_spec, st_spec]
    if lookahead:
        ins.append(x2d)
        in_specs.append(pl.BlockSpec((tl, D_MODEL), lambda n, t: (n * nt + jnp.minimum(t + 1, nt - 1), 0)))
        in_specs[0] = pl.BlockSpec((tl, D_MODEL), lambda n, t: (n * nt, 0))
    out_specs = [row_spec, st_spec]
    out_shape = [jax.ShapeDtypeStruct(x2d.shape, F32), jax.ShapeDtypeStruct(state.shape, F32)]
    scratch = [pltpu.VMEM((padr + tl, D_FF), F32)]
    pool = None
    if pool_state is not None:
        ppad = pool_state.shape[1]
        assert ppad % SUBLANES == 0 and ppad >= POOL_BUF * ns and (nt == 1 or tl >= ppad)
        pool = (ppad, start)
        pst_spec = pl.BlockSpec((1, ppad, D_MODEL), lambda n, t: (n, 0, 0))
        pool_consts = (p["norm_mix"][layer], p["pool_w"], p["pool_scale"])
        ins += [pool_state, *pool_consts]
        in_specs += [pst_spec] + [_const_spec(a.shape) for a in pool_consts]
        out_specs.append(pst_spec)
        out_shape.append(jax.ShapeDtypeStruct(pool_state.shape, F32))
        scratch.append(pltpu.VMEM((ppad + tl, D_MODEL), F32))
    ins += [p["norm_ffn"], wgu, p["ffn_cw"], p["ffn_cb"], wd, p["norm_final"]]
    in_specs += [_layer_spec(p["norm_ffn"].shape, layer), _const_spec(wgu.shape),
                 _layer_spec(p["ffn_cw"].shape, layer), _layer_spec(p["ffn_cb"].shape, layer),
                 _const_spec(wd.shape), _const_spec(p["norm_final"].shape)]
    riders = [_cast_rider(w, l, nt) for w, l in cast]
    ins += [w for w, _ in cast]
    in_specs += [r[0] for r in riders]
    out_specs += [r[1] for r in riders]
    out_shape += [r[2] for r in riders]
    if lookahead:
        scratch.append(pltpu.VMEM((2, tl, D_MODEL), BF16))
        if pool is not None:
            scratch.append(pltpu.VMEM((2, tl, D_MODEL), F32))
    kern = functools.partial(_ffn_kernel, tl=tl, ns=ns, padr=padr, final=final, pool=pool,
                             nsplit=nsplit, n_cast=len(cast), lookahead=lookahead)
    return pl.pallas_call(
        kern,
        grid=(n_groups, nt),
        in_specs=in_specs,
        out_specs=out_specs,
        out_shape=out_shape,
        scratch_shapes=scratch,
        compiler_params=pltpu.CompilerParams(
            dimension_semantics=("arbitrary", "arbitrary"), vmem_limit_bytes=VMEM_LIMIT),
        name=("pool_" if pool else "") + ("ffn_final" if final else "ffn"),
    )(*ins)


def _pool_kernel(x_ref, st_ref, gn_ref, wp_ref, sc_ref, xo_ref, sto_ref, ext_ref,
                 *, tl, ns, padr, start):
    t = pl.program_id(1)

    @pl.when(t == 0)
    def _():
        ext_ref[0:padr, :] = st_ref[0]

    pos0 = start + ((t * tl) >> _log2(ns))
    xo_ref[...] = _pool_mix(x_ref[...], gn_ref[...], ext_ref, padr, 0, ns, pos0, wp_ref, sc_ref[...])

    carry = ext_ref[tl:tl + padr, :]
    ext_ref[0:padr, :] = carry

    @pl.when(t == pl.num_programs(1) - 1)
    def _():
        sto_ref[0] = carry


def _pool(x2d, state, gn, wp, sc, *, tl, ns, start):
    n_groups, padr, _ = state.shape
    rows_per_group = x2d.shape[0] // n_groups
    nt = rows_per_group // tl
    assert padr % SUBLANES == 0 and padr >= POOL_BUF * ns and (nt == 1 or tl >= padr)
    kern = functools.partial(_pool_kernel, tl=tl, ns=ns, padr=padr, start=start)
    row_spec = pl.BlockSpec((tl, D_MODEL), lambda n, t: (n * nt + t, 0))
    st_spec = pl.BlockSpec((1, padr, D_MODEL), lambda n, t: (n, 0, 0))
    return pl.pallas_call(
        kern,
        grid=(n_groups, nt),
        in_specs=[row_spec, st_spec, _const_spec(gn.shape), _const_spec(wp.shape),
                  _const_spec(sc.shape)],
        out_specs=[row_spec, st_spec],
        out_shape=[jax.ShapeDtypeStruct(x2d.shape, F32),
                   jax.ShapeDtypeStruct(state.shape, F32)],
        scratch_shapes=[pltpu.VMEM((padr + tl, D_MODEL), F32)],
        compiler_params=pltpu.CompilerParams(
            dimension_semantics=("arbitrary", "arbitrary"), vmem_limit_bytes=VMEM_LIMIT),
        name="pool",
    )(x2d, state, gn, wp, sc)


def kernel(x_prompt, x_sample, state_conv_a, state_hgrn, state_pool, state_ffn, norm_mix, norm_ffn,
           norm_final, w_in, conv_a_w, hgrn_lower_bounds, hgrn_norm, w_out, pool_w, pool_scale,
           ffn_w_gu, ffn_conv_w, ffn_conv_b, ffn_w_down):
    batch, seq, _ = x_prompt.shape
    dec_batch, dec_seq, _ = x_sample.shape

    p = dict(
        norm_mix=norm_mix[:, None, :], norm_ffn=norm_ffn[:, None, :], norm_final=norm_final[None, :],
        ffn_cw=ffn_conv_w, ffn_cb=ffn_conv_b[:, None, :], pool_w=pool_w[0].astype(BF16),
        pool_scale=pool_scale[0][None, :])
    win = w_in[0].astype(BF16)
    wout = w_out[0].astype(BF16)
    mix_args = (p["norm_mix"][0], win, conv_a_w[0], hgrn_lower_bounds, hgrn_norm[0][None, :], wout)

    xp = x_prompt.reshape(batch * seq, D_MODEL)
    x1, conv_p, hgrn_p, wgu0, wd0 = _mixer0_prompt(
        xp, batch, seq, *mix_args, [(ffn_w_gu, 0), (ffn_w_down, 0)], tl=MIXER_TILE, nsplit=MIXER_PARTS)
    pad_ffn = _round_up(CONV_HIST, SUBLANES)
    pad_pool = _round_up(POOL_BUF, SUBLANES)
    zf = jnp.zeros((batch, pad_ffn, D_FF), F32)
    zp = jnp.zeros((batch, pad_pool, D_MODEL), F32)
    x2, ffn0_p, wgu1, wd1 = _ffn(x1, zf, p, 0, wgu0, wd0, tl=FFN_TILE, ns=1, final=False,
                                 nsplit=FFN_PARTS,
                                 cast=[(ffn_w_gu, 1), (ffn_w_down, 1)])
    yp, ffn1_p, pool_p = _ffn(x2, zf, p, 1, wgu1, wd1, tl=POOL_FFN_TILE, ns=1, final=True,
                              nsplit=POOL_FFN_PARTS, pool_state=zp, start=0)
    y_prompt = yp.reshape(batch, seq, D_MODEL)
    conv_prompt = conv_p[None, :, SUBLANES - CONV_HIST:, :]
    hgrn_prompt = hgrn_p[None]
    pool_prompt = pool_p[None, :, pad_pool - POOL_BUF:, :]
    ffn_prompt = jnp.stack([ffn0_p, ffn1_p])[:, :, pad_ffn - CONV_HIST:, :]

    xs = x_sample.reshape(dec_batch * dec_seq, D_MODEL)
    cs = state_conv_a[0]
    e2 = jnp.pad(cs, ((0, 0), (0, dec_seq - CONV_HIST), (0, 0))).reshape(dec_batch * dec_seq, D_A)
    x1s, us, hgrn_s = _mixer0_sample(xs, e2, state_hgrn[0], *mix_args, seq_len=dec_seq,
                                     bn=SAMPLE_SEQS_PER_TILE)
    conv_sample = us.reshape(dec_batch, dec_seq, D_A)[None, :, dec_seq - CONV_HIST:, :]

    def to_tm(a):
        return a.transpose(1, 0, 2).reshape(1, a.shape[1] * dec_batch, a.shape[2])

    def from_tm(a, rows):
        return a.reshape(rows, dec_batch, a.shape[-1]).transpose(1, 0, 2)

    x1t = to_tm(x1s.reshape(dec_batch, dec_seq, D_MODEL))[0]
    x2t, ffn0_s = _ffn(x1t, to_tm(state_ffn[0]), p, 0, wgu0, wd0, tl=SAMPLE_FFN_TILE, ns=dec_batch,
                       final=False, nsplit=SAMPLE_FFN_PARTS)
    x3t, pool_s = _pool(x2t, to_tm(state_pool[0]), p["norm_mix"][1], p["pool_w"], p["pool_scale"],
                        tl=dec_batch * dec_seq, ns=dec_batch, start=PAST_LEN)
    ys, ffn1_s = _ffn(x3t, to_tm(state_ffn[1]), p, 1, wgu1, wd1, tl=SAMPLE_FFN_TILE, ns=dec_batch,
                      final=True, nsplit=SAMPLE_FFN_PARTS)
    y_sample = from_tm(ys, dec_seq)
    pool_sample = from_tm(pool_s[0], POOL_BUF)[None]
    ffn_sample = jnp.stack([from_tm(ffn0_s[0], CONV_HIST), from_tm(ffn1_s[0], CONV_HIST)])

    return (y_prompt, y_sample, conv_prompt, conv_sample, hgrn_prompt, hgrn_s[None],
            pool_prompt, pool_sample, ffn_prompt, ffn_sample)
```

```python
import functools

import jax
import jax.numpy as jnp
import numpy as np
from jax import lax
from jax.experimental import pallas as pl
from jax.experimental.pallas import tpu as pltpu

D_MODEL = 1024
D_A = 512
N_HEADS = 4
HEAD = 128
D_B = N_HEADS * HEAD
D_IN = 3 * D_A + 4 * D_B
D_FF = 2816
POOL_WINDOWS = (2, 4, 8, 16)
POOL_GROUP = D_MODEL // len(POOL_WINDOWS)
POOL_BUF = max(POOL_WINDOWS) - 1
CONV_HIST = 2
PAST_LEN = 16384
EPS = 1e-6
LOG2E = 1.4426950408889634

SUBLANES = 8
HGRN_CHUNK = 128
VMEM_LIMIT = 56 * 1024 * 1024
CAST_STEPS = 16
MIXER_TILE, MIXER_PARTS = 1024, 4
FFN_TILE, FFN_PARTS = 1024, 4
POOL_FFN_TILE, POOL_FFN_PARTS = 512, 1
SAMPLE_FFN_TILE, SAMPLE_FFN_PARTS = 512, 1
SAMPLE_SEQS_PER_TILE = 32
SAMPLE_INTRA_ROWS = 128

F32 = jnp.float32
BF16 = jnp.bfloat16

_OFF_AC, _OFF_AB, _OFF_AV = 0, D_A, 2 * D_A
_OFF_Q, _OFF_F, _OFF_I, _OFF_G = (3 * D_A, 3 * D_A + D_B, 3 * D_A + 2 * D_B, 3 * D_A + 3 * D_B)
_HEADS = [slice(h * HEAD, (h + 1) * HEAD) for h in range(N_HEADS)]


def _round_up(x, m):
    return (x + m - 1) // m * m


def _log2(x):
    assert x & (x - 1) == 0
    return x.bit_length() - 1


def _rms(x, g):
    return x * lax.rsqrt(jnp.mean(x * x, axis=-1, keepdims=True) + EPS) * g


def _silu(x):
    return x * jax.nn.sigmoid(x)


def _dot(a, b):
    return jnp.dot(a, b, preferred_element_type=F32)


def _dot_nt(a, b):
    return lax.dot_general(a, b, (((1,), (1,)), ((), ())), preferred_element_type=F32)


def _dot_tn(a, b):
    return lax.dot_general(a, b, (((0,), (0,)), ((), ())), preferred_element_type=F32)


def _dep_zero(a):
    u = lax.bitcast_convert_type(a, jnp.int32)
    return lax.shift_right_logical(lax.shift_right_logical(u, 16), 16).astype(F32)


def _shifted(ext_ref, padr, n, k_rows, cols=slice(None), row0=0):
    off = padr + row0 - k_rows
    assert off >= 0
    base = off // SUBLANES * SUBLANES
    rem = off - base
    if rem == 0:
        return ext_ref[pl.ds(off, n), cols]
    win = ext_ref[pl.ds(base, n + SUBLANES), cols]
    return pltpu.roll(win, SUBLANES - rem, 0)[SUBLANES:SUBLANES + n, :]


def _levels(seg):
    return [1 << i for i in range(_log2(seg))]


def _selector_levels(seg):
    return [b for b in _levels(seg) if 1 < b < SUBLANES]


def _selector(c, seg):
    t = np.arange(c)[:, None]
    u = np.arange(c)[None, :]
    same_seg = (t // seg) == (u // seg)
    blocks = [same_seg & (u <= t)]
    blocks.append(same_seg & (u > t) if seg < c else np.ones((SUBLANES, c), bool))
    for b in _selector_levels(seg):
        pair = (t // (2 * b)) == (u // (2 * b))
        anchor = (t // (2 * b)) * 2 * b + b - 1
        right = (t // b) % 2 == 1
        blocks.append(pair & np.where(right, (u > anchor) & (u <= t), (u > t) & (u <= anchor)))
    sel = np.concatenate(blocks, axis=0).astype(np.float32)
    return np.concatenate([sel, sel], axis=1)


def _split_bf16(a):
    hi = a.astype(BF16)
    return hi, (a - hi.astype(F32)).astype(BF16)


def _hgrn_intra(q, kk, f, lf2, sel_ref, g_ref, seg):
    c = q.shape[0]
    x = _dot(sel_ref[...], jnp.concatenate(_split_bf16(lf2), axis=0))
    g = x[0:c]

    row = lax.broadcasted_iota(jnp.int32, (c, 1), 0)
    rowi = lax.broadcasted_iota(jnp.int32, (c, c), 0)
    coli = lax.broadcasted_iota(jnp.int32, (c, c), 1)
    diff = rowi ^ coli
    lower = rowi > coli

    p0 = q * kk
    p1 = q * pltpu.roll(kk, 1, 0) * f
    amat = []
    for sl in _HEADS:
        a0 = jnp.sum(p0[:, sl], axis=-1, keepdims=True)
        a1 = jnp.sum(p1[:, sl], axis=-1, keepdims=True)
        amat.append(jnp.where(rowi == coli, a0, jnp.where(lower & (diff == 1), a1, 0.0)))

    sel_blocks = {b: x[off:off + c] for b, off in
                  zip(_selector_levels(seg), range(c + (c if seg < c else SUBLANES), x.shape[0], c))}
    if seg > SUBLANES:
        g_ref[...] = g
    for b in _levels(seg)[1:]:
        if b in sel_blocks:
            ex = sel_blocks[b]
        else:
            parts = []
            for pair in range(c // (2 * b)):
                anchor = g_ref[pl.ds(pair * 2 * b + b - 1, 1), :]
                parts.append(g[pair * 2 * b:(pair + 1) * 2 * b, :] - anchor)
            ex = -jnp.abs(parts[0] if len(parts) == 1 else jnp.concatenate(parts, axis=0))
        w = jnp.exp2(ex)
        right = ((row >> _log2(b)) & 1) == 1
        ql = jnp.where(right, q * w, 0.0).astype(BF16)
        kl = jnp.where(right, 0.0, kk * w).astype(BF16)
        mask = lower & ((diff >> _log2(b)) == 1)
        for h, sl in enumerate(_HEADS):
            amat[h] = jnp.where(mask, _dot_nt(ql[:, sl], kl[:, sl]), amat[h])
    return amat, q * jnp.exp2(g), x


def _gates(z_ref, rows, lb):
    f = lb + (1.0 - lb) * jax.nn.sigmoid(z_ref[rows, _OFF_F:_OFF_F + D_B])
    return f, 1.0 - f, jnp.log(f) * LOG2E


def _lower_bound(lbp_ref):
    lbp = lbp_ref[...]
    lbe = jnp.exp(lbp - jnp.max(lbp, axis=0, keepdims=True))
    return lbe[0:1, :] / jnp.sum(lbe, axis=0, keepdims=True)


def _mixer_b_out(z_ref, o_ref, y_ref, hn, rows):
    for h, sl in enumerate(_HEADS):
        on = _rms(o_ref[rows, sl], hn[:, sl])
        gate = z_ref[rows, _OFF_G + h * HEAD:_OFF_G + (h + 1) * HEAD]
        y_ref[rows, D_A + h * HEAD:D_A + (h + 1) * HEAD] = (on * _silu(gate)).astype(BF16)


def _cast_rider(w, layer, nt):
    _, r, c = w.shape
    rows = r // CAST_STEPS
    assert rows * CAST_STEPS == r and rows % (2 * SUBLANES) == 0

    def blk(n, t):
        return jnp.minimum(n * nt + t, CAST_STEPS - 1)

    return (pl.BlockSpec((None, rows, c), lambda n, t: (layer, blk(n, t), 0)),
            pl.BlockSpec((rows, c), lambda n, t: (blk(n, t), 0)),
            jax.ShapeDtypeStruct((r, c), BF16))


def _run_cast_riders(nt, pairs):
    @pl.when(pl.program_id(0) * nt + pl.program_id(1) < CAST_STEPS)
    def _():
        for src, dst in pairs:
            dst[...] = src[...].astype(BF16)


def _mixer0_prompt_kernel(x_ref, gn_ref, win_ref, cw_ref, lbp_ref, hn_ref, wout_ref, sel_ref,
                          c0_ref, c1_ref,
                          xo_ref, convo_ref, so_ref, c0o_ref, c1o_ref,
                          hb_ref, z_ref, o_ref, y_ref, ext_ref, st_ref, g_ref, *, tl, chunk, nsplit):
    t = pl.program_id(1)
    _run_cast_riders(pl.num_programs(1), [(c0_ref, c0o_ref), (c1_ref, c1o_ref)])

    @pl.when(t == 0)
    def _():
        ext_ref[0:SUBLANES, :] = jnp.zeros((SUBLANES, D_A), F32)
        st_ref[...] = jnp.zeros((N_HEADS, HEAD, HEAD), F32)

    hb_ref[...] = _rms(x_ref[...], gn_ref[...]).astype(BF16)
    lb = _lower_bound(lbp_ref)
    cw = cw_ref[...]
    hn = hn_ref[...]
    part = tl // nsplit

    for s in range(nsplit):
        r0 = s * part
        rows = slice(r0, r0 + part)
        hb = hb_ref[rows, :]
        z_ref[rows, _OFF_Q:_OFF_G] = _dot(hb, win_ref[:, _OFF_Q:_OFF_G])

        for c0 in range(r0, r0 + part, chunk):
            crows = slice(c0, c0 + chunk)
            f, kk, lf2 = _gates(z_ref, crows, lb)
            amat, qg, x = _hgrn_intra(z_ref[crows, _OFF_Q:_OFF_Q + D_B], kk, f, lf2, sel_ref,
                                      g_ref.at[c0 // chunk], chunk)
            g_last = x[chunk:chunk + 1]
            dec = jnp.exp2(g_last)
            kg = (kk * jnp.exp2(g_last - x[0:chunk])).astype(BF16)
            qg = qg.astype(BF16)
            vb = z_ref[crows, _OFF_I:_OFF_I + D_B].astype(BF16)
            for h, sl in enumerate(_HEADS):
                st = st_ref[h]
                lhs = jnp.concatenate([qg[:, sl], amat[h].astype(BF16)], axis=1)
                rhs = jnp.concatenate([st.T.astype(BF16), vb[:, sl]], axis=0)
                o_ref[crows, sl] = _dot(lhs, rhs)
                st_ref[h] = dec[:, sl] * st + _dot_tn(vb[:, sl], kg[:, sl])

        z_ref[rows, 0:_OFF_Q] = _dot(hb, win_ref[:, 0:_OFF_Q])
        z_ref[rows, _OFF_G:D_IN] = _dot(hb, win_ref[:, _OFF_G:D_IN])

        u = z_ref[rows, _OFF_AC:_OFF_AC + D_A] * z_ref[rows, _OFF_AV:_OFF_AV + D_A]
        ext_ref[SUBLANES + r0:SUBLANES + r0 + part, :] = u
        conv = (_shifted(ext_ref, SUBLANES, part, 2, row0=r0) * cw[0:1, :]
                + _shifted(ext_ref, SUBLANES, part, 1, row0=r0) * cw[1:2, :] + u * cw[2:3, :])
        y_ref[rows, 0:D_A] = (z_ref[rows, _OFF_AB:_OFF_AB + D_A] * conv).astype(BF16)
        _mixer_b_out(z_ref, o_ref, y_ref, hn, rows)
        xo_ref[rows, :] = x_ref[rows, :] + _dot(y_ref[rows, :], wout_ref[...])

    carry = ext_ref[tl:tl + SUBLANES, :]
    ext_ref[0:SUBLANES, :] = carry

    @pl.when(t == pl.num_programs(1) - 1)
    def _():
        convo_ref[0] = carry
        for h in range(N_HEADS):
            so_ref[0, h] = st_ref[h].T


def _mixer0_sample_kernel(x_ref, e2_ref, s0_ref, gn_ref, win_ref, cw_ref, lbp_ref, hn_ref,
                          wout_ref, sel_ref,
                          xo_ref, uo_ref, so_ref,
                          z_ref, o_ref, y_ref, dec_ref, qg_ref, kg_ref, *, tl, seq, chunk):
    x = x_ref[...]
    z_ref[...] = _dot(_rms(x, gn_ref[...]).astype(BF16), win_ref[...])

    u = z_ref[:, _OFF_AC:_OFF_AC + D_A] * z_ref[:, _OFF_AV:_OFF_AV + D_A]
    cw = cw_ref[...]
    e2 = e2_ref[...]
    tpos = lax.broadcasted_iota(jnp.int32, (tl, 1), 0) & (seq - 1)
    u1 = jnp.where(tpos >= 1, pltpu.roll(u, 1, 0), pltpu.roll(e2, tl - 1, 0))
    u2 = jnp.where(tpos >= 2, pltpu.roll(u, 2, 0), e2)
    uo_ref[...] = u
    conv = u2 * cw[0:1, :] + u1 * cw[1:2, :] + u * cw[2:3, :]
    y_ref[:, 0:D_A] = (z_ref[:, _OFF_AB:_OFF_AB + D_A] * conv).astype(BF16)

    lb = _lower_bound(lbp_ref)
    seqs_per_chunk = chunk // seq
    owner = (lax.broadcasted_iota(jnp.int32, (chunk, HEAD), 0) >> _log2(seq)
             == lax.broadcasted_iota(jnp.int32, (chunk, HEAD), 1)).astype(BF16)
    for ci in range(tl // chunk):
        crows = slice(ci * chunk, (ci + 1) * chunk)
        f, kk, lf2 = _gates(z_ref, crows, lb)
        amat, qg, xsel = _hgrn_intra(z_ref[crows, _OFF_Q:_OFF_Q + D_B], kk, f, lf2, sel_ref, None, seq)
        v_c = z_ref[crows, _OFF_I:_OFF_I + D_B].astype(BF16)
        o_ref[crows, :] = jnp.concatenate(
            [_dot(amat[h].astype(BF16), v_c[:, sl]) for h, sl in enumerate(_HEADS)], axis=1)
        qg_ref[crows, :] = qg
        kg_ref[crows, :] = kk * jnp.exp2(xsel[chunk:2 * chunk])
        hi, mid = _split_bf16(lf2)
        dec_ref[ci] = jnp.exp2(_dot_tn(hi, owner) + _dot_tn(mid, owner))

    for i in range(tl // seq):
        rows = slice(i * seq, (i + 1) * seq)
        ci, col = divmod(i, seqs_per_chunk)
        qg_i = qg_ref[rows, :].astype(BF16)
        kg_i = kg_ref[rows, :].astype(BF16)
        vb = z_ref[rows, _OFF_I:_OFF_I + D_B].astype(BF16)
        for h, sl in enumerate(_HEADS):
            s0 = s0_ref[i, h]
            o_ref[rows, sl] = o_ref[rows, sl] + _dot(qg_i[:, sl], s0.astype(BF16))
            so_ref[i, h] = dec_ref[ci, sl, :][:, col:col + 1] * s0 + _dot_tn(kg_i[:, sl], vb[:, sl])

    _mixer_b_out(z_ref, o_ref, y_ref, hn_ref[...], slice(None))
    xo_ref[...] = x + _dot(y_ref[...], wout_ref[...])


def _const_spec(shape):
    nd = len(shape)
    return pl.BlockSpec(shape, lambda *_: (0,) * nd, pipeline_mode=pl.Buffered(1))


def _layer_spec(shape, layer):
    nd = len(shape) - 1
    return pl.BlockSpec((None,) + tuple(shape[1:]), lambda *_: (layer,) + (0,) * nd,
                        pipeline_mode=pl.Buffered(1))


def _mixer0_prompt(x2d, n_seq, seq_len, gn, win, cw, lbp, hn, wout, cast, *, tl, nsplit):
    nt = seq_len // tl
    assert n_seq * nt >= CAST_STEPS
    sel = jnp.asarray(_selector(HGRN_CHUNK, HGRN_CHUNK), BF16)
    kern = functools.partial(_mixer0_prompt_kernel, tl=tl, chunk=HGRN_CHUNK, nsplit=nsplit)
    row_spec = pl.BlockSpec((tl, D_MODEL), lambda n, t: (n * nt + t, 0))
    consts = (gn, win, cw, lbp, hn, wout, sel)
    riders = [_cast_rider(w, layer, nt) for w, layer in cast]
    return pl.pallas_call(
        kern,
        grid=(n_seq, nt),
        in_specs=[row_spec] + [_const_spec(a.shape) for a in consts] + [r[0] for r in riders],
        out_specs=[row_spec,
                   pl.BlockSpec((1, SUBLANES, D_A), lambda n, t: (n, 0, 0)),
                   pl.BlockSpec((1, N_HEADS, HEAD, HEAD), lambda n, t: (n, 0, 0, 0))]
                  + [r[1] for r in riders],
        out_shape=[jax.ShapeDtypeStruct(x2d.shape, F32),
                   jax.ShapeDtypeStruct((n_seq, SUBLANES, D_A), F32),
                   jax.ShapeDtypeStruct((n_seq, N_HEADS, HEAD, HEAD), F32)]
                  + [r[2] for r in riders],
        scratch_shapes=[pltpu.VMEM((tl, D_MODEL), BF16),
                        pltpu.VMEM((tl, D_IN), F32),
                        pltpu.VMEM((tl, D_B), F32),
                        pltpu.VMEM((tl, D_A + D_B), BF16),
                        pltpu.VMEM((SUBLANES + tl, D_A), F32),
                        pltpu.VMEM((N_HEADS, HEAD, HEAD), F32),
                        pltpu.VMEM((tl // HGRN_CHUNK, HGRN_CHUNK, D_B), F32)],
        compiler_params=pltpu.CompilerParams(
            dimension_semantics=("arbitrary", "arbitrary"), vmem_limit_bytes=VMEM_LIMIT),
        name="mixer0_prompt",
    )(x2d, *consts, *[w for w, _ in cast])


def _mixer0_sample(x2d, e2, s0, gn, win, cw, lbp, hn, wout, *, seq_len, bn):
    n_seq = s0.shape[0]
    tl = bn * seq_len
    chunk = min(tl, SAMPLE_INTRA_ROWS)
    sel = jnp.asarray(_selector(chunk, seq_len), BF16)
    kern = functools.partial(_mixer0_sample_kernel, tl=tl, seq=seq_len, chunk=chunk)
    row_spec = pl.BlockSpec((tl, D_MODEL), lambda i: (i, 0))
    conv_spec = pl.BlockSpec((tl, D_A), lambda i: (i, 0))
    st_spec = pl.BlockSpec((bn, N_HEADS, HEAD, HEAD), lambda i: (i, 0, 0, 0))
    consts = (gn, win, cw, lbp, hn, wout, sel)
    return pl.pallas_call(
        kern,
        grid=(n_seq // bn,),
        in_specs=[row_spec, conv_spec, st_spec] + [_const_spec(a.shape) for a in consts],
        out_specs=[row_spec, conv_spec, st_spec],
        out_shape=[jax.ShapeDtypeStruct(x2d.shape, F32),
                   jax.ShapeDtypeStruct((x2d.shape[0], D_A), F32),
                   jax.ShapeDtypeStruct(s0.shape, F32)],
        scratch_shapes=[pltpu.VMEM((tl, D_IN), F32),
                        pltpu.VMEM((tl, D_B), F32),
                        pltpu.VMEM((tl, D_A + D_B), BF16),
                        pltpu.VMEM((tl // chunk, D_B, HEAD), F32),
                        pltpu.VMEM((tl, D_B), F32),
                        pltpu.VMEM((tl, D_B), F32)],
        compiler_params=pltpu.CompilerParams(
            dimension_semantics=("arbitrary",), vmem_limit_bytes=VMEM_LIMIT),
        name="mixer0_sample",
    )(x2d, e2, s0, *consts)


def _pool_mix(x, gm, ext_ref, padr, row0, ns, pos0, wp_ref, sc):
    n = x.shape[0]
    h = _rms(x, gm)
    ext_ref[padr + row0:padr + row0 + n, :] = h
    pos = pos0 + (lax.broadcasted_iota(jnp.int32, (n, 1), 0) >> _log2(ns))
    ys = []
    for gi, w in enumerate(POOL_WINDOWS):
        cols = slice(gi * POOL_GROUP, (gi + 1) * POOL_GROUP)
        hg = h[:, cols]
        if ns == 1:
            hist = _round_up(POOL_BUF, SUBLANES)
            s = ext_ref[pl.ds(padr + row0 - hist, n + hist), cols]
            sh = 1
            while sh < w:
                s = s + pltpu.roll(s, sh, 0)
                sh *= 2
            acc = s[hist:hist + n, :]
        else:
            acc = hg
            for k in range(1, w):
                acc = acc + _shifted(ext_ref, padr, n, k * ns, cols, row0)
        cnt = jnp.minimum(w, pos + 1).astype(F32)
        ys.append(_dot((acc / cnt - hg).astype(BF16), wp_ref[gi]))
    return x + jnp.concatenate(ys, axis=1) * sc


def _ffn_kernel(*refs, tl, ns, padr, final, pool, nsplit, n_cast, lookahead):
    refs = list(refs)
    x_ref, st_ref = refs[:2]
    del refs[:2]
    if lookahead:
        xn_ref = refs.pop(0)
    if pool is not None:
        pst_ref, gm_ref, wp_ref, sc_ref = refs[:4]
        del refs[:4]
        ppad, start = pool
    gn_ref, wgu_ref, cw_ref, cb_ref, wd_ref, gf_ref = refs[:6]
    del refs[:6]
    cast_in = refs[:n_cast]
    del refs[:n_cast]
    xo_ref, sto_ref = refs[:2]
    del refs[:2]
    if pool is not None:
        psto_ref = refs.pop(0)
    cast_out = refs[:n_cast]
    del refs[:n_cast]
    ext_ref = refs.pop(0)
    if pool is not None:
        pext_ref = refs.pop(0)
    if lookahead:
        hb_buf = refs.pop(0)
        if pool is not None:
            xm_buf = refs.pop(0)
    assert not refs
    t = pl.program_id(1)
    last = pl.num_programs(1) - 1
    part = tl // nsplit
    if n_cast:
        _run_cast_riders(pl.num_programs(1), list(zip(cast_in, cast_out)))

    def front(src_ref, tile, r0, anchor=None):
        zero = 0.0 if anchor is None else _dep_zero(anchor)
        x = src_ref[r0:r0 + part, :]
        if pool is not None:
            pos0 = start + ((tile * tl + r0) >> _log2(ns))
            x = _pool_mix(x, gm_ref[...] + zero, pext_ref, ppad, r0, ns, pos0, wp_ref, sc_ref[...])
        return x, _rms(x, gn_ref[...] + zero).astype(BF16)

    def front_tile(src_ref, tile, slot, anchors=None):
        for s in range(nsplit):
            rows = slice(s * part, (s + 1) * part)
            x, hb = front(src_ref, tile, s * part, None if anchors is None else anchors[s])
            hb_buf[slot, rows, :] = hb
            if pool is not None:
                xm_buf[slot, rows, :] = x
        if pool is not None:
            pext_ref[0:ppad, :] = pext_ref[tl:tl + ppad, :]

    @pl.when(t == 0)
    def _():
        ext_ref[0:padr, :] = st_ref[0]
        if pool is not None:
            pext_ref[0:ppad, :] = pst_ref[0]
        if lookahead:
            front_tile(x_ref, 0, 0)

    cw = cw_ref[...]
    cb = cb_ref[...]
    slot = t % 2
    anchors = []
    for s in range(nsplit):
        r0 = s * part
        rows = slice(r0, r0 + part)
        if lookahead:
            hb = hb_buf[slot, rows, :]
            x = xm_buf[slot, rows, :] if pool is not None else x_ref[rows, :]
        else:
            x, hb = front(x_ref, t, r0)
        g = _dot(hb, wgu_ref[:, 0:D_FF])
        anchors.append(g[0:1, 0:D_MODEL])
        ext_ref[padr + r0:padr + r0 + part, :] = g
        gc = (_shifted(ext_ref, padr, part, 2 * ns, row0=r0) * cw[0:1, :]
              + _shifted(ext_ref, padr, part, ns, row0=r0) * cw[1:2, :]
              + g * cw[2:3, :] + cb)
        up = _dot(hb, wgu_ref[:, D_FF:2 * D_FF])
        act = (_silu(gc) * up).astype(BF16)
        xo = x + _dot(act, wd_ref[...])
        if final:
            xo = _rms(xo, gf_ref[...])
        xo_ref[rows, :] = xo

    if lookahead:
        front_tile(xn_ref, jnp.minimum(t + 1, last), 1 - slot, anchors)
    elif pool is not None:
        pext_ref[0:ppad, :] = pext_ref[tl:tl + ppad, :]
    carry = ext_ref[tl:tl + padr, :]
    ext_ref[0:padr, :] = carry

    @pl.when(t == last)
    def _():
        sto_ref[0] = carry
        if pool is not None:
            psto_ref[0] = pext_ref[0:ppad, :]


def _ffn(x2d, state, p, layer, wgu, wd, *, tl, ns, final, nsplit=1, pool_state=None, start=0, cast=()):
    n_groups, padr, _ = state.shape
    rows_per_group = x2d.shape[0] // n_groups
    nt = rows_per_group // tl
    lookahead = pool_state is not None and nt > 1
    assert padr % SUBLANES == 0 and padr >= CONV_HIST * ns and (nt == 1 or tl >= padr)
    assert not cast or n_groups * nt >= CAST_STEPS
    row_spec = pl.BlockSpec((tl, D_MODEL), lambda n, t: (n * nt + t, 0))
    st_spec = pl.BlockSpec((1, padr, D_FF), lambda n, t: (n, 0, 0))
    ins = [x2d, state]
    in_specs = [row_spec, st_spec]
    if lookahead:
        ins.append(x2d)
        in_specs.append(pl.BlockSpec((tl, D_MODEL), lambda n, t: (n * nt + jnp.minimum(t + 1, nt - 1), 0)))
        in_specs[0] = pl.BlockSpec((tl, D_MODEL), lambda n, t: (n * nt, 0))
    out_specs = [row_spec, st_spec]
    out_shape = [jax.ShapeDtypeStruct(x2d.shape, F32), jax.ShapeDtypeStruct(state.shape, F32)]
    scratch = [pltpu.VMEM((padr + tl, D_FF), F32)]
    pool = None
    if pool_state is not None:
        ppad = pool_state.shape[1]
        assert ppad % SUBLANES == 0 and ppad >= POOL_BUF * ns and (nt == 1 or tl >= ppad)
        pool = (ppad, start)
        pst_spec = pl.BlockSpec((1, ppad, D_MODEL), lambda n, t: (n, 0, 0))
        pool_consts = (p["norm_mix"][layer], p["pool_w"], p["pool_scale"])
        ins += [pool_state, *pool_consts]
        in_specs += [pst_spec] + [_const_spec(a.shape) for a in pool_consts]
        out_specs.append(pst_spec)
        out_shape.append(jax.ShapeDtypeStruct(pool_state.shape, F32))
        scratch.append(pltpu.VMEM((ppad + tl, D_MODEL), F32))
    ins += [p["norm_ffn"], wgu, p["ffn_cw"], p["ffn_cb"], wd, p["norm_final"]]
    in_specs += [_layer_spec(p["norm_ffn"].shape, layer), _const_spec(wgu.shape),
                 _layer_spec(p["ffn_cw"].shape, layer), _layer_spec(p["ffn_cb"].shape, layer),
                 _const_spec(wd.shape), _const_spec(p["norm_final"].shape)]
    riders = [_cast_rider(w, l, nt) for w, l in cast]
    ins += [w for w, _ in cast]
    in_specs += [r[0] for r in riders]
    out_specs += [r[1] for r in riders]
    out_shape += [r[2] for r in riders]
    if lookahead:
        scratch.append(pltpu.VMEM((2, tl, D_MODEL), BF16))
        if pool is not None:
            scratch.append(pltpu.VMEM((2, tl, D_MODEL), F32))
    kern = functools.partial(_ffn_kernel, tl=tl, ns=ns, padr=padr, final=final, pool=pool,
                             nsplit=nsplit, n_cast=len(cast), lookahead=lookahead)
    return pl.pallas_call(
        kern,
        grid=(n_groups, nt),
        in_specs=in_specs,
        out_specs=out_specs,
        out_shape=out_shape,
        scratch_shapes=scratch,
        compiler_params=pltpu.CompilerParams(
            dimension_semantics=("arbitrary", "arbitrary"), vmem_limit_bytes=VMEM_LIMIT),
        name=("pool_" if pool else "") + ("ffn_final" if final else "ffn"),
    )(*ins)


def _pool_kernel(x_ref, st_ref, gn_ref, wp_ref, sc_ref, xo_ref, sto_ref, ext_ref,
                 *, tl, ns, padr, start):
    t = pl.program_id(1)

    @pl.when(t == 0)
    def _():
        ext_ref[0:padr, :] = st_ref[0]

    pos0 = start + ((t * tl) >> _log2(ns))
    xo_ref[...] = _pool_mix(x_ref[...], gn_ref[...], ext_ref, padr, 0, ns, pos0, wp_ref, sc_ref[...])

    carry = ext_ref[tl:tl + padr, :]
    ext_ref[0:padr, :] = carry

    @pl.when(t == pl.num_programs(1) - 1)
    def _():
        sto_ref[0] = carry


def _pool(x2d, state, gn, wp, sc, *, tl, ns, start):
    n_groups, padr, _ = state.shape
    rows_per_group = x2d.shape[0] // n_groups
    nt = rows_per_group // tl
    assert padr % SUBLANES == 0 and padr >= POOL_BUF * ns and (nt == 1 or tl >= padr)
    kern = functools.partial(_pool_kernel, tl=tl, ns=ns, padr=padr, start=start)
    row_spec = pl.BlockSpec((tl, D_MODEL), lambda n, t: (n * nt + t, 0))
    st_spec = pl.BlockSpec((1, padr, D_MODEL), lambda n, t: (n, 0, 0))
    return pl.pallas_call(
        kern,
        grid=(n_groups, nt),
        in_specs=[row_spec, st_spec, _const_spec(gn.shape), _const_spec(wp.shape),
                  _const_spec(sc.shape)],
        out_specs=[row_spec, st_spec],
        out_shape=[jax.ShapeDtypeStruct(x2d.shape, F32),
                   jax.ShapeDtypeStruct(state.shape, F32)],
        scratch_shapes=[pltpu.VMEM((padr + tl, D_MODEL), F32)],
        compiler_params=pltpu.CompilerParams(
            dimension_semantics=("arbitrary", "arbitrary"), vmem_limit_bytes=VMEM_LIMIT),
        name="pool",
    )(x2d, state, gn, wp, sc)


def kernel(x_prompt, x_sample, state_conv_a, state_hgrn, state_pool, state_ffn, norm_mix, norm_ffn,
           norm_final, w_in, conv_a_w, hgrn_lower_bounds, hgrn_norm, w_out, pool_w, pool_scale,
           ffn_w_gu, ffn_conv_w, ffn_conv_b, ffn_w_down):
    batch, seq, _ = x_prompt.shape
    dec_batch, dec_seq, _ = x_sample.shape

    p = dict(
        norm_mix=norm_mix[:, None, :], norm_ffn=norm_ffn[:, None, :], norm_final=norm_final[None, :],
        ffn_cw=ffn_conv_w, ffn_cb=ffn_conv_b[:, None, :], pool_w=pool_w[0].astype(BF16),
        pool_scale=pool_scale[0][None, :])
    win = w_in[0].astype(BF16)
    wout = w_out[0].astype(BF16)
    mix_args = (p["norm_mix"][0], win, conv_a_w[0], hgrn_lower_bounds, hgrn_norm[0][None, :], wout)

    xp = x_prompt.reshape(batch * seq, D_MODEL)
    x1, conv_p, hgrn_p, wgu0, wd0 = _mixer0_prompt(
        xp, batch, seq, *mix_args, [(ffn_w_gu, 0), (ffn_w_down, 0)], tl=MIXER_TILE, nsplit=MIXER_PARTS)
    pad_ffn = _round_up(CONV_HIST, SUBLANES)
    pad_pool = _round_up(POOL_BUF, SUBLANES)
    zf = jnp.zeros((batch, pad_ffn, D_FF), F32)
    zp = jnp.zeros((batch, pad_pool, D_MODEL), F32)
    x2, ffn0_p, wgu1, wd1 = _ffn(x1, zf, p, 0, wgu0, wd0, tl=FFN_TILE, ns=1, final=False,
                                 nsplit=FFN_PARTS,
                                 cast=[(ffn_w_gu, 1), (ffn_w_down, 1)])
    yp, ffn1_p, pool_p = _ffn(x2, zf, p, 1, wgu1, wd1, tl=POOL_FFN_TILE, ns=1, final=True,
                              nsplit=POOL_FFN_PARTS, pool_state=zp, start=0)
    y_prompt = yp.reshape(batch, seq, D_MODEL)
    conv_prompt = conv_p[None, :, SUBLANES - CONV_HIST:, :]
    hgrn_prompt = hgrn_p[None]
    pool_prompt = pool_p[None, :, pad_pool - POOL_BUF:, :]
    ffn_prompt = jnp.stack([ffn0_p, ffn1_p])[:, :, pad_ffn - CONV_HIST:, :]

    xs = x_sample.reshape(dec_batch * dec_seq, D_MODEL)
    cs = state_conv_a[0]
    e2 = jnp.pad(cs, ((0, 0), (0, dec_seq - CONV_HIST), (0, 0))).reshape(dec_batch * dec_seq, D_A)
    x1s, us, hgrn_s = _mixer0_sample(xs, e2, state_hgrn[0], *mix_args, seq_len=dec_seq,
                                     bn=SAMPLE_SEQS_PER_TILE)
    conv_sample = us.reshape(dec_batch, dec_seq, D_A)[None, :, dec_seq - CONV_HIST:, :]

    def to_tm(a):
        return a.transpose(1, 0, 2).reshape(1, a.shape[1] * dec_batch, a.shape[2])

    def from_tm(a, rows):
        return a.reshape(rows, dec_batch, a.shape[-1]).transpose(1, 0, 2)

    x1t = to_tm(x1s.reshape(dec_batch, dec_seq, D_MODEL))[0]
    x2t, ffn0_s = _ffn(x1t, to_tm(state_ffn[0]), p, 0, wgu0, wd0, tl=SAMPLE_FFN_TILE, ns=dec_batch,
                       final=False, nsplit=SAMPLE_FFN_PARTS)
    x3t, pool_s = _pool(x2t, to_tm(state_pool[0]), p["norm_mix"][1], p["pool_w"], p["pool_scale"],
                        tl=dec_batch * dec_seq, ns=dec_batch, start=PAST_LEN)
    ys, ffn1_s = _ffn(x3t, to_tm(state_ffn[1]), p, 1, wgu1, wd1, tl=SAMPLE_FFN_TILE, ns=dec_batch,
                      final=True, nsplit=SAMPLE_FFN_PARTS)
    y_sample = from_tm(ys, dec_seq)
    pool_sample = from_tm(pool_s[0], POOL_BUF)[None]
    ffn_sample = jnp.stack([from_tm(ffn0_s[0], CONV_HIST), from_tm(ffn1_s[0], CONV_HIST)])

    return (y_prompt, y_sample, conv_prompt, conv_sample, hgrn_prompt, hgrn_s[None],
            pool_prompt, pool_sample, ffn_prompt, ffn_sample)
```

```python
import functools

import jax
import jax.numpy as jnp
import numpy as np
from jax import lax
from jax.experimental import pallas as pl
from jax.experimental.pallas import tpu as pltpu

D_MODEL = 1024
D_A = 512
N_HEADS = 4
HEAD = 128
D_B = N_HEADS * HEAD
D_IN = 3 * D_A + 4 * D_B
D_FF = 2816
POOL_WINDOWS = (2, 4, 8, 16)
POOL_GROUP = D_MODEL // len(POOL_WINDOWS)
POOL_BUF = max(POOL_WINDOWS) - 1
CONV_HIST = 2
PAST_LEN = 16384
EPS = 1e-6
LOG2E = 1.4426950408889634

SUBLANES = 8
HGRN_CHUNK = 128
VMEM_LIMIT = 56 * 1024 * 1024
CAST_STEPS = 16
MIXER_TILE, MIXER_PARTS = 1024, 4
FFN_TILE, FFN_PARTS = 1024, 4
POOL_FFN_TILE, POOL_FFN_PARTS = 512, 1
SAMPLE_FFN_TILE, SAMPLE_FFN_PARTS = 512, 1
SAMPLE_SEQS_PER_TILE = 32
SAMPLE_INTRA_ROWS = 128

F32 = jnp.float32
BF16 = jnp.bfloat16

_OFF_AC, _OFF_AB, _OFF_AV = 0, D_A, 2 * D_A
_OFF_Q, _OFF_F, _OFF_I, _OFF_G = (3 * D_A, 3 * D_A + D_B, 3 * D_A + 2 * D_B, 3 * D_A + 3 * D_B)
_HEADS = [slice(h * HEAD, (h + 1) * HEAD) for h in range(N_HEADS)]


def _round_up(x, m):
    return (x + m - 1) // m * m


def _log2(x):
    assert x & (x - 1) == 0
    return x.bit_length() - 1


def _rms(x, g):
    return x * lax.rsqrt(jnp.mean(x * x, axis=-1, keepdims=True) + EPS) * g


def _silu(x):
    return x * jax.nn.sigmoid(x)


def _dot(a, b):
    return jnp.dot(a, b, preferred_element_type=F32)


def _dot_nt(a, b):
    return lax.dot_general(a, b, (((1,), (1,)), ((), ())), preferred_element_type=F32)


def _dot_tn(a, b):
    return lax.dot_general(a, b, (((0,), (0,)), ((), ())), preferred_element_type=F32)


def _dep_zero(a):
    u = lax.bitcast_convert_type(a, jnp.int32)
    return lax.shift_right_logical(lax.shift_right_logical(u, 16), 16).astype(F32)


def _shifted(ext_ref, padr, n, k_rows, cols=slice(None), row0=0):
    off = padr + row0 - k_rows
    assert off >= 0
    base = off // SUBLANES * SUBLANES
    rem = off - base
    if rem == 0:
        return ext_ref[pl.ds(off, n), cols]
    win = ext_ref[pl.ds(base, n + SUBLANES), cols]
    return pltpu.roll(win, SUBLANES - rem, 0)[SUBLANES:SUBLANES + n, :]


def _levels(seg):
    return [1 << i for i in range(_log2(seg))]


def _selector_levels(seg):
    return [b for b in _levels(seg) if 1 < b < SUBLANES]


def _selector(c, seg):
    t = np.arange(c)[:, None]
    u = np.arange(c)[None, :]
    same_seg = (t // seg) == (u // seg)
    blocks = [same_seg & (u <= t)]
    blocks.append(same_seg & (u > t) if seg < c else np.ones((SUBLANES, c), bool))
    for b in _selector_levels(seg):
        pair = (t // (2 * b)) == (u // (2 * b))
        anchor = (t // (2 * b)) * 2 * b + b - 1
        right = (t // b) % 2 == 1
        blocks.append(pair & np.where(right, (u > anchor) & (u <= t), (u > t) & (u <= anchor)))
    sel = np.concatenate(blocks, axis=0).astype(np.float32)
    return np.concatenate([sel, sel], axis=1)


def _split_bf16(a):
    hi = a.astype(BF16)
    return hi, (a - hi.astype(F32)).astype(BF16)


def _hgrn_intra(q, kk, f, lf2, sel_ref, g_ref, seg):
    c = q.shape[0]
    x = _dot(sel_ref[...], jnp.concatenate(_split_bf16(lf2), axis=0))
    g = x[0:c]

    row = lax.broadcasted_iota(jnp.int32, (c, 1), 0)
    rowi = lax.broadcasted_iota(jnp.int32, (c, c), 0)
    coli = lax.broadcasted_iota(jnp.int32, (c, c), 1)
    diff = rowi ^ coli
    lower = rowi > coli

    p0 = q * kk
    p1 = q * pltpu.roll(kk, 1, 0) * f
    amat = []
    for sl in _HEADS:
        a0 = jnp.sum(p0[:, sl], axis=-1, keepdims=True)
        a1 = jnp.sum(p1[:, sl], axis=-1, keepdims=True)
        amat.append(jnp.where(rowi == coli, a0, jnp.where(lower & (diff == 1), a1, 0.0)))

    sel_blocks = {b: x[off:off + c] for b, off in
                  zip(_selector_levels(seg), range(c + (c if seg < c else SUBLANES), x.shape[0], c))}
    if seg > SUBLANES:
        g_ref[...] = g
    for b in _levels(seg)[1:]:
        if b in sel_blocks:
            ex = sel_blocks[b]
        else:
            parts = []
            for pair in range(c // (2 * b)):
                anchor = g_ref[pl.ds(pair * 2 * b + b - 1, 1), :]
                parts.append(g[pair * 2 * b:(pair + 1) * 2 * b, :] - anchor)
            ex = -jnp.abs(parts[0] if len(parts) == 1 else jnp.concatenate(parts, axis=0))
        w = jnp.exp2(ex)
        right = ((row >> _log2(b)) & 1) == 1
        ql = jnp.where(right, q * w, 0.0).astype(BF16)
        kl = jnp.where(right, 0.0, kk * w).astype(BF16)
        mask = lower & ((diff >> _log2(b)) == 1)
        for h, sl in enumerate(_HEADS):
            amat[h] = jnp.where(mask, _dot_nt(ql[:, sl], kl[:, sl]), amat[h])
    return amat, q * jnp.exp2(g), x


def _gates(z_ref, rows, lb):
    f = lb + (1.0 - lb) * jax.nn.sigmoid(z_ref[rows, _OFF_F:_OFF_F + D_B])
    return f, 1.0 - f, jnp.log(f) * LOG2E


def _lower_bound(lbp_ref):
    lbp = lbp_ref[...]
    lbe = jnp.exp(lbp - jnp.max(lbp, axis=0, keepdims=True))
    return lbe[0:1, :] / jnp.sum(lbe, axis=0, keepdims=True)


def _mixer_b_out(z_ref, o_ref, y_ref, hn, rows):
    for h, sl in enumerate(_HEADS):
        on = _rms(o_ref[rows, sl], hn[:, sl])
        gate = z_ref[rows, _OFF_G + h * HEAD:_OFF_G + (h + 1) * HEAD]
        y_ref[rows, D_A + h * HEAD:D_A + (h + 1) * HEAD] = (on * _silu(gate)).astype(BF16)


def _cast_rider(w, layer, nt):
    _, r, c = w.shape
    rows = r // CAST_STEPS
    assert rows * CAST_STEPS == r and rows % (2 * SUBLANES) == 0

    def blk(n, t):
        return jnp.minimum(n * nt + t, CAST_STEPS - 1)

    return (pl.BlockSpec((None, rows, c), lambda n, t: (layer, blk(n, t), 0)),
            pl.BlockSpec((rows, c), lambda n, t: (blk(n, t), 0)),
            jax.ShapeDtypeStruct((r, c), BF16))


def _run_cast_riders(nt, pairs):
    @pl.when(pl.program_id(0) * nt + pl.program_id(1) < CAST_STEPS)
    def _():
        for src, dst in pairs:
            dst[...] = src[...].astype(BF16)


def _mixer0_prompt_kernel(x_ref, gn_ref, win_ref, cw_ref, lbp_ref, hn_ref, wout_ref, sel_ref,
                          c0_ref, c1_ref,
                          xo_ref, convo_ref, so_ref, c0o_ref, c1o_ref,
                          hb_ref, z_ref, o_ref, y_ref, ext_ref, st_ref, g_ref, *, tl, chunk, nsplit):
    t = pl.program_id(1)
    _run_cast_riders(pl.num_programs(1), [(c0_ref, c0o_ref), (c1_ref, c1o_ref)])

    @pl.when(t == 0)
    def _():
        ext_ref[0:SUBLANES, :] = jnp.zeros((SUBLANES, D_A), F32)
        st_ref[...] = jnp.zeros((N_HEADS, HEAD, HEAD), F32)

    hb_ref[...] = _rms(x_ref[...], gn_ref[...]).astype(BF16)
    lb = _lower_bound(lbp_ref)
    cw = cw_ref[...]
    hn = hn_ref[...]
    part = tl // nsplit

    for s in range(nsplit):
        r0 = s * part
        rows = slice(r0, r0 + part)
        hb = hb_ref[rows, :]
        z_ref[rows, _OFF_Q:_OFF_G] = _dot(hb, win_ref[:, _OFF_Q:_OFF_G])

        for c0 in range(r0, r0 + part, chunk):
            crows = slice(c0, c0 + chunk)
            f, kk, lf2 = _gates(z_ref, crows, lb)
            amat, qg, x = _hgrn_intra(z_ref[crows, _OFF_Q:_OFF_Q + D_B], kk, f, lf2, sel_ref,
                                      g_ref.at[c0 // chunk], chunk)
            g_last = x[chunk:chunk + 1]
            dec = jnp.exp2(g_last)
            kg = (kk * jnp.exp2(g_last - x[0:chunk])).astype(BF16)
            qg = qg.astype(BF16)
            vb = z_ref[crows, _OFF_I:_OFF_I + D_B].astype(BF16)
            for h, sl in enumerate(_HEADS):
                st = st_ref[h]
                lhs = jnp.concatenate([qg[:, sl], amat[h].astype(BF16)], axis=1)
                rhs = jnp.concatenate([st.T.astype(BF16), vb[:, sl]], axis=0)
                o_ref[crows, sl] = _dot(lhs, rhs)
                st_ref[h] = dec[:, sl] * st + _dot_tn(vb[:, sl], kg[:, sl])

        z_ref[rows, 0:_OFF_Q] = _dot(hb, win_ref[:, 0:_OFF_Q])
        z_ref[rows, _OFF_G:D_IN] = _dot(hb, win_ref[:, _OFF_G:D_IN])

        u = z_ref[rows, _OFF_AC:_OFF_AC + D_A] * z_ref[rows, _OFF_AV:_OFF_AV + D_A]
        ext_ref[SUBLANES + r0:SUBLANES + r0 + part, :] = u
        conv = (_shifted(ext_ref, SUBLANES, part, 2, row0=r0) * cw[0:1, :]
                + _shifted(ext_ref, SUBLANES, part, 1, row0=r0) * cw[1:2, :] + u * cw[2:3, :])
        y_ref[rows, 0:D_A] = (z_ref[rows, _OFF_AB:_OFF_AB + D_A] * conv).astype(BF16)
        _mixer_b_out(z_ref, o_ref, y_ref, hn, rows)
        xo_ref[rows, :] = x_ref[rows, :] + _dot(y_ref[rows, :], wout_ref[...])

    carry = ext_ref[tl:tl + SUBLANES, :]
    ext_ref[0:SUBLANES, :] = carry

    @pl.when(t == pl.num_programs(1) - 1)
    def _():
        convo_ref[0] = carry
        for h in range(N_HEADS):
            so_ref[0, h] = st_ref[h].T


def _mixer0_sample_kernel(x_ref, e2_ref, s0_ref, gn_ref, win_ref, cw_ref, lbp_ref, hn_ref,
                          wout_ref, sel_ref,
                          xo_ref, uo_ref, so_ref,
                          z_ref, o_ref, y_ref, dec_ref, qg_ref, kg_ref, *, tl, seq, chunk):
    x = x_ref[...]
    z_ref[...] = _dot(_rms(x, gn_ref[...]).astype(BF16), win_ref[...])

    u = z_ref[:, _OFF_AC:_OFF_AC + D_A] * z_ref[:, _OFF_AV:_OFF_AV + D_A]
    cw = cw_ref[...]
    e2 = e2_ref[...]
    tpos = lax.broadcasted_iota(jnp.int32, (tl, 1), 0) & (seq - 1)
    u1 = jnp.where(tpos >= 1, pltpu.roll(u, 1, 0), pltpu.roll(e2, tl - 1, 0))
    u2 = jnp.where(tpos >= 2, pltpu.roll(u, 2, 0), e2)
    uo_ref[...] = u
    conv = u2 * cw[0:1, :] + u1 * cw[1:2, :] + u * cw[2:3, :]
    y_ref[:, 0:D_A] = (z_ref[:, _OFF_AB:_OFF_AB + D_A] * conv).astype(BF16)

    lb = _lower_bound(lbp_ref)
    seqs_per_chunk = chunk // seq
    owner = (lax.broadcasted_iota(jnp.int32, (chunk, HEAD), 0) >> _log2(seq)
             == lax.broadcasted_iota(jnp.int32, (chunk, HEAD), 1)).astype(BF16)
    for ci in range(tl // chunk):
        crows = slice(ci * chunk, (ci + 1) * chunk)
        f, kk, lf2 = _gates(z_ref, crows, lb)
        amat, qg, xsel = _hgrn_intra(z_ref[crows, _OFF_Q:_OFF_Q + D_B], kk, f, lf2, sel_ref, None, seq)
        v_c = z_ref[crows, _OFF_I:_OFF_I + D_B].astype(BF16)
        o_ref[crows, :] = jnp.concatenate(
            [_dot(amat[h].astype(BF16), v_c[:, sl]) for h, sl in enumerate(_HEADS)], axis=1)
        qg_ref[crows, :] = qg
        kg_ref[crows, :] = kk * jnp.exp2(xsel[chunk:2 * chunk])
        hi, mid = _split_bf16(lf2)
        dec_ref[ci] = jnp.exp2(_dot_tn(hi, owner) + _dot_tn(mid, owner))

    for i in range(tl // seq):
        rows = slice(i * seq, (i + 1) * seq)
        ci, col = divmod(i, seqs_per_chunk)
        qg_i = qg_ref[rows, :].astype(BF16)
        kg_i = kg_ref[rows, :].astype(BF16)
        vb = z_ref[rows, _OFF_I:_OFF_I + D_B].astype(BF16)
        for h, sl in enumerate(_HEADS):
            s0 = s0_ref[i, h]
            o_ref[rows, sl] = o_ref[rows, sl] + _dot(qg_i[:, sl], s0.astype(BF16))
            so_ref[i, h] = dec_ref[ci, sl, :][:, col:col + 1] * s0 + _dot_tn(kg_i[:, sl], vb[:, sl])

    _mixer_b_out(z_ref, o_ref, y_ref, hn_ref[...], slice(None))
    xo_ref[...] = x + _dot(y_ref[...], wout_ref[...])


def _const_spec(shape):
    nd = len(shape)
    return pl.BlockSpec(shape, lambda *_: (0,) * nd, pipeline_mode=pl.Buffered(1))


def _layer_spec(shape, layer):
    nd = len(shape) - 1
    return pl.BlockSpec((None,) + tuple(shape[1:]), lambda *_: (layer,) + (0,) * nd,
                        pipeline_mode=pl.Buffered(1))


def _mixer0_prompt(x2d, n_seq, seq_len, gn, win, cw, lbp, hn, wout, cast, *, tl, nsplit):
    nt = seq_len // tl
    assert n_seq * nt >= CAST_STEPS
    sel = jnp.asarray(_selector(HGRN_CHUNK, HGRN_CHUNK), BF16)
    kern = functools.partial(_mixer0_prompt_kernel, tl=tl, chunk=HGRN_CHUNK, nsplit=nsplit)
    row_spec = pl.BlockSpec((tl, D_MODEL), lambda n, t: (n * nt + t, 0))
    consts = (gn, win, cw, lbp, hn, wout, sel)
    riders = [_cast_rider(w, layer, nt) for w, layer in cast]
    return pl.pallas_call(
        kern,
        grid=(n_seq, nt),
        in_specs=[row_spec] + [_const_spec(a.shape) for a in consts] + [r[0] for r in riders],
        out_specs=[row_spec,
                   pl.BlockSpec((1, SUBLANES, D_A), lambda n, t: (n, 0, 0)),
                   pl.BlockSpec((1, N_HEADS, HEAD, HEAD), lambda n, t: (n, 0, 0, 0))]
                  + [r[1] for r in riders],
        out_shape=[jax.ShapeDtypeStruct(x2d.shape, F32),
                   jax.ShapeDtypeStruct((n_seq, SUBLANES, D_A), F32),
                   jax.ShapeDtypeStruct((n_seq, N_HEADS, HEAD, HEAD), F32)]
                  + [r[2] for r in riders],
        scratch_shapes=[pltpu.VMEM((tl, D_MODEL), BF16),
                        pltpu.VMEM((tl, D_IN), F32),
                        pltpu.VMEM((tl, D_B), F32),
                        pltpu.VMEM((tl, D_A + D_B), BF16),
                        pltpu.VMEM((SUBLANES + tl, D_A), F32),
                        pltpu.VMEM((N_HEADS, HEAD, HEAD), F32),
                        pltpu.VMEM((tl // HGRN_CHUNK, HGRN_CHUNK, D_B), F32)],
        compiler_params=pltpu.CompilerParams(
            dimension_semantics=("arbitrary", "arbitrary"), vmem_limit_bytes=VMEM_LIMIT),
        name="mixer0_prompt",
    )(x2d, *consts, *[w for w, _ in cast])


def _mixer0_sample(x2d, e2, s0, gn, win, cw, lbp, hn, wout, *, seq_len, bn):
    n_seq = s0.shape[0]
    tl = bn * seq_len
    chunk = min(tl, SAMPLE_INTRA_ROWS)
    sel = jnp.asarray(_selector(chunk, seq_len), BF16)
    kern = functools.partial(_mixer0_sample_kernel, tl=tl, seq=seq_len, chunk=chunk)
    row_spec = pl.BlockSpec((tl, D_MODEL), lambda i: (i, 0))
    conv_spec = pl.BlockSpec((tl, D_A), lambda i: (i, 0))
    st_spec = pl.BlockSpec((bn, N_HEADS, HEAD, HEAD), lambda i: (i, 0, 0, 0))
    consts = (gn, win, cw, lbp, hn, wout, sel)
    return pl.pallas_call(
        kern,
        grid=(n_seq // bn,),
        in_specs=[row_spec, conv_spec, st_spec] + [_const_spec(a.shape) for a in consts],
        out_specs=[row_spec, conv_spec, st_spec],
        out_shape=[jax.ShapeDtypeStruct(x2d.shape, F32),
                   jax.ShapeDtypeStruct((x2d.shape[0], D_A), F32),
                   jax.ShapeDtypeStruct(s0.shape, F32)],
        scratch_shapes=[pltpu.VMEM((tl, D_IN), F32),
                        pltpu.VMEM((tl, D_B), F32),
                        pltpu.VMEM((tl, D_A + D_B), BF16),
                        pltpu.VMEM((tl // chunk, D_B, HEAD), F32),
                        pltpu.VMEM((tl, D_B), F32),
                        pltpu.VMEM((tl, D_B), F32)],
        compiler_params=pltpu.CompilerParams(
            dimension_semantics=("arbitrary",), vmem_limit_bytes=VMEM_LIMIT),
        name="mixer0_sample",
    )(x2d, e2, s0, *consts)


def _pool_mix(x, gm, ext_ref, padr, row0, ns, pos0, wp_ref, sc):
    n = x.shape[0]
    h = _rms(x, gm)
    ext_ref[padr + row0:padr + row0 + n, :] = h
    pos = pos0 + (lax.broadcasted_iota(jnp.int32, (n, 1), 0) >> _log2(ns))
    ys = []
    for gi, w in enumerate(POOL_WINDOWS):
        cols = slice(gi * POOL_GROUP, (gi + 1) * POOL_GROUP)
        hg = h[:, cols]
        if ns == 1:
            hist = _round_up(POOL_BUF, SUBLANES)
            s = ext_ref[pl.ds(padr + row0 - hist, n + hist), cols]
            sh = 1
            while sh < w:
                s = s + pltpu.roll(s, sh, 0)
                sh *= 2
            acc = s[hist:hist + n, :]
        else:
            acc = hg
            for k in range(1, w):
                acc = acc + _shifted(ext_ref, padr, n, k * ns, cols, row0)
        cnt = jnp.minimum(w, pos + 1).astype(F32)
        ys.append(_dot((acc / cnt - hg).astype(BF16), wp_ref[gi]))
    return x + jnp.concatenate(ys, axis=1) * sc


def _ffn_kernel(*refs, tl, ns, padr, final, pool, nsplit, n_cast, lookahead):
    refs = list(refs)
    x_ref, st_ref = refs[:2]
    del refs[:2]
    if lookahead:
        xn_ref = refs.pop(0)
    if pool is not None:
        pst_ref = refs.pop(0)
        if lookahead:
            pstn_ref = refs.pop(0)
        gm_ref, wp_ref, sc_ref = refs[:3]
        del refs[:3]
        ppad, start = pool
    gn_ref, wgu_ref, cw_ref, cb_ref, wd_ref, gf_ref = refs[:6]
    del refs[:6]
    cast_in = refs[:n_cast]
    del refs[:n_cast]
    xo_ref, sto_ref = refs[:2]
    del refs[:2]
    if pool is not None:
        psto_ref = refs.pop(0)
    cast_out = refs[:n_cast]
    del refs[:n_cast]
    ext_ref = refs.pop(0)
    if pool is not None:
        pext_ref = refs.pop(0)
    if lookahead:
        hb_buf = refs.pop(0)
        if pool is not None:
            xm_buf = refs.pop(0)
    assert not refs
    t = pl.program_id(1)
    last = pl.num_programs(1) - 1
    part = tl // nsplit
    if n_cast:
        _run_cast_riders(pl.num_programs(1), list(zip(cast_in, cast_out)))

    def front(src_ref, tile, r0, anchor=None):
        zero = 0.0 if anchor is None else _dep_zero(anchor)
        x = src_ref[r0:r0 + part, :]
        if pool is not None:
            pos0 = start + ((tile * tl + r0) >> _log2(ns))
            x = _pool_mix(x, gm_ref[...] + zero, pext_ref, ppad, r0, ns, pos0, wp_ref, sc_ref[...])
        return x, _rms(x, gn_ref[...] + zero).astype(BF16)

    def front_tile(src_ref, tile, slot, anchors=None):
        for s in range(nsplit):
            rows = slice(s * part, (s + 1) * part)
            x, hb = front(src_ref, tile, s * part, None if anchors is None else anchors[s])
            hb_buf[slot, rows, :] = hb
            if pool is not None:
                xm_buf[slot, rows, :] = x
        if pool is not None:
            pext_ref[0:ppad, :] = pext_ref[tl:tl + ppad, :]

    @pl.when(t == 0)
    def _():
        ext_ref[0:padr, :] = st_ref[0]
        if pool is not None and not lookahead:
            pext_ref[0:ppad, :] = pst_ref[0]

    if lookahead:
        @pl.when((t == 0) & (pl.program_id(0) == 0))
        def _():
            pext_ref[0:ppad, :] = pst_ref[0]
            front_tile(x_ref, 0, 0)

    cw = cw_ref[...]
    cb = cb_ref[...]
    slot = t % 2
    anchors = []
    for s in range(nsplit):
        r0 = s * part
        rows = slice(r0, r0 + part)
        if lookahead:
            hb = hb_buf[slot, rows, :]
            x = xm_buf[slot, rows, :] if pool is not None else x_ref[rows, :]
        else:
            x, hb = front(x_ref, t, r0)
        g = _dot(hb, wgu_ref[:, 0:D_FF])
        anchors.append(g[0:1, 0:D_MODEL])
        ext_ref[padr + r0:padr + r0 + part, :] = g
        gc = (_shifted(ext_ref, padr, part, 2 * ns, row0=r0) * cw[0:1, :]
              + _shifted(ext_ref, padr, part, ns, row0=r0) * cw[1:2, :]
              + g * cw[2:3, :] + cb)
        up = _dot(hb, wgu_ref[:, D_FF:2 * D_FF])
        act = (_silu(gc) * up).astype(BF16)
        xo = x + _dot(act, wd_ref[...])
        if final:
            xo = _rms(xo, gf_ref[...])
        xo_ref[rows, :] = xo

    if lookahead:
        pcarry = pext_ref[0:ppad, :]
        pext_ref[0:ppad, :] = jnp.where(t == last, pstn_ref[0], pcarry)
        front_tile(xn_ref, jnp.where(t == last, 0, t + 1), 1 - slot, anchors)
    elif pool is not None:
        pcarry = pext_ref[tl:tl + ppad, :]
        pext_ref[0:ppad, :] = pcarry
    carry = ext_ref[tl:tl + padr, :]
    ext_ref[0:padr, :] = carry

    @pl.when(t == last)
    def _():
        sto_ref[0] = carry
        if pool is not None:
            psto_ref[0] = pcarry


def _ffn(x2d, state, p, layer, wgu, wd, *, tl, ns, final, nsplit=1, pool_state=None, start=0, cast=()):
    n_groups, padr, _ = state.shape
    rows_per_group = x2d.shape[0] // n_groups
    nt = rows_per_group // tl
    lookahead = pool_state is not None and nt > 1
    assert padr % SUBLANES == 0 and padr >= CONV_HIST * ns and (nt == 1 or tl >= padr)
    assert not cast or n_groups * nt >= CAST_STEPS
    row_spec = pl.BlockSpec((tl, D_MODEL), lambda n, t: (n * nt + t, 0))
    st_spec = pl.BlockSpec((1, padr, D_FF), lambda n, t: (n, 0, 0))
    ins = [x2d, state]
    in_specs = [row_spec, st_spec]
    if lookahead:
        assert nt % 2 == 0
        last_tile = n_groups * nt - 1
        ins.append(x2d)
        in_specs.append(pl.BlockSpec((tl, D_MODEL), lambda n, t: (jnp.minimum(n * nt + t + 1, last_tile), 0)))
        in_specs[0] = pl.BlockSpec((tl, D_MODEL), lambda n, t: (0, 0))
    out_specs = [row_spec, st_spec]
    out_shape = [jax.ShapeDtypeStruct(x2d.shape, F32), jax.ShapeDtypeStruct(state.shape, F32)]
    scratch = [pltpu.VMEM((padr + tl, D_FF), F32)]
    pool = None
    if pool_state is not None:
        ppad = pool_state.shape[1]
        assert ppad % SUBLANES == 0 and ppad >= POOL_BUF * ns and (nt == 1 or tl >= ppad)
        pool = (ppad, start)
        pst_spec = pl.BlockSpec((1, ppad, D_MODEL), lambda n, t: (n, 0, 0))
        pool_consts = (p["norm_mix"][layer], p["pool_w"], p["pool_scale"])
        ins.append(pool_state)
        in_specs.append(pst_spec)
        if lookahead:
            ins.append(pool_state)
            in_specs.append(pl.BlockSpec((1, ppad, D_MODEL),
                                         lambda n, t: (jnp.minimum(n + 1, n_groups - 1), 0, 0)))
        ins += pool_consts
        in_specs += [_const_spec(a.shape) for a in pool_consts]
        out_specs.append(pst_spec)
        out_shape.append(jax.ShapeDtypeStruct(pool_state.shape, F32))
        scratch.append(pltpu.VMEM((ppad + tl, D_MODEL), F32))
    ins += [p["norm_ffn"], wgu, p["ffn_cw"], p["ffn_cb"], wd, p["norm_final"]]
    in_specs += [_layer_spec(p["norm_ffn"].shape, layer), _const_spec(wgu.shape),
                 _layer_spec(p["ffn_cw"].shape, layer), _layer_spec(p["ffn_cb"].shape, layer),
                 _const_spec(wd.shape), _const_spec(p["norm_final"].shape)]
    riders = [_cast_rider(w, l, nt) for w, l in cast]
    ins += [w for w, _ in cast]
    in_specs += [r[0] for r in riders]
    out_specs += [r[1] for r in riders]
    out_shape += [r[2] for r in riders]
    if lookahead:
        scratch.append(pltpu.VMEM((2, tl, D_MODEL), BF16))
        if pool is not None:
            scratch.append(pltpu.VMEM((2, tl, D_MODEL), F32))
    kern = functools.partial(_ffn_kernel, tl=tl, ns=ns, padr=padr, final=final, pool=pool,
                             nsplit=nsplit, n_cast=len(cast), lookahead=lookahead)
    return pl.pallas_call(
        kern,
        grid=(n_groups, nt),
        in_specs=in_specs,
        out_specs=out_specs,
        out_shape=out_shape,
        scratch_shapes=scratch,
        compiler_params=pltpu.CompilerParams(
            dimension_semantics=("arbitrary", "arbitrary"), vmem_limit_bytes=VMEM_LIMIT),
        name=("pool_" if pool else "") + ("ffn_final" if final else "ffn"),
    )(*ins)


def _pool_kernel(x_ref, st_ref, gn_ref, wp_ref, sc_ref, xo_ref, sto_ref, ext_ref,
                 *, tl, ns, padr, start):
    t = pl.program_id(1)

    @pl.when(t == 0)
    def _():
        ext_ref[0:padr, :] = st_ref[0]

    pos0 = start + ((t * tl) >> _log2(ns))
    xo_ref[...] = _pool_mix(x_ref[...], gn_ref[...], ext_ref, padr, 0, ns, pos0, wp_ref, sc_ref[...])

    carry = ext_ref[tl:tl + padr, :]
    ext_ref[0:padr, :] = carry

    @pl.when(t == pl.num_programs(1) - 1)
    def _():
        sto_ref[0] = carry


def _pool(x2d, state, gn, wp, sc, *, tl, ns, start):
    n_groups, padr, _ = state.shape
    rows_per_group = x2d.shape[0] // n_groups
    nt = rows_per_group // tl
    assert padr % SUBLANES == 0 and padr >= POOL_BUF * ns and (nt == 1 or tl >= padr)
    kern = functools.partial(_pool_kernel, tl=tl, ns=ns, padr=padr, start=start)
    row_spec = pl.BlockSpec((tl, D_MODEL), lambda n, t: (n * nt + t, 0))
    st_spec = pl.BlockSpec((1, padr, D_MODEL), lambda n, t: (n, 0, 0))
    return pl.pallas_call(
        kern,
        grid=(n_groups, nt),
        in_specs=[row_spec, st_spec, _const_spec(gn.shape), _const_spec(wp.shape),
                  _const_spec(sc.shape)],
        out_specs=[row_spec, st_spec],
        out_shape=[jax.ShapeDtypeStruct(x2d.shape, F32),
                   jax.ShapeDtypeStruct(state.shape, F32)],
        scratch_shapes=[pltpu.VMEM((padr + tl, D_MODEL), F32)],
        compiler_params=pltpu.CompilerParams(
            dimension_semantics=("arbitrary", "arbitrary"), vmem_limit_bytes=VMEM_LIMIT),
        name="pool",
    )(x2d, state, gn, wp, sc)


def kernel(x_prompt, x_sample, state_conv_a, state_hgrn, state_pool, state_ffn, norm_mix, norm_ffn,
           norm_final, w_in, conv_a_w, hgrn_lower_bounds, hgrn_norm, w_out, pool_w, pool_scale,
           ffn_w_gu, ffn_conv_w, ffn_conv_b, ffn_w_down):
    batch, seq, _ = x_prompt.shape
    dec_batch, dec_seq, _ = x_sample.shape

    p = dict(
        norm_mix=norm_mix[:, None, :], norm_ffn=norm_ffn[:, None, :], norm_final=norm_final[None, :],
        ffn_cw=ffn_conv_w, ffn_cb=ffn_conv_b[:, None, :], pool_w=pool_w[0].astype(BF16),
        pool_scale=pool_scale[0][None, :])
    win = w_in[0].astype(BF16)
    wout = w_out[0].astype(BF16)
    mix_args = (p["norm_mix"][0], win, conv_a_w[0], hgrn_lower_bounds, hgrn_norm[0][None, :], wout)

    xp = x_prompt.reshape(batch * seq, D_MODEL)
    x1, conv_p, hgrn_p, wgu0, wd0 = _mixer0_prompt(
        xp, batch, seq, *mix_args, [(ffn_w_gu, 0), (ffn_w_down, 0)], tl=MIXER_TILE, nsplit=MIXER_PARTS)
    pad_ffn = _round_up(CONV_HIST, SUBLANES)
    pad_pool = _round_up(POOL_BUF, SUBLANES)
    zf = jnp.zeros((batch, pad_ffn, D_FF), F32)
    zp = jnp.zeros((batch, pad_pool, D_MODEL), F32)
    x2, ffn0_p, wgu1, wd1 = _ffn(x1, zf, p, 0, wgu0, wd0, tl=FFN_TILE, ns=1, final=False,
                                 nsplit=FFN_PARTS,
                                 cast=[(ffn_w_gu, 1), (ffn_w_down, 1)])
    yp, ffn1_p, pool_p = _ffn(x2, zf, p, 1, wgu1, wd1, tl=POOL_FFN_TILE, ns=1, final=True,
                              nsplit=POOL_FFN_PARTS, pool_state=zp, start=0)
    y_prompt = yp.reshape(batch, seq, D_MODEL)
    conv_prompt = conv_p[None, :, SUBLANES - CONV_HIST:, :]
    hgrn_prompt = hgrn_p[None]
    pool_prompt = pool_p[None, :, pad_pool - POOL_BUF:, :]
    ffn_prompt = jnp.stack([ffn0_p, ffn1_p])[:, :, pad_ffn - CONV_HIST:, :]

    xs = x_sample.reshape(dec_batch * dec_seq, D_MODEL)
    cs = state_conv_a[0]
    e2 = jnp.pad(cs, ((0, 0), (0, dec_seq - CONV_HIST), (0, 0))).reshape(dec_batch * dec_seq, D_A)
    x1s, us, hgrn_s = _mixer0_sample(xs, e2, state_hgrn[0], *mix_args, seq_len=dec_seq,
                                     bn=SAMPLE_SEQS_PER_TILE)
    conv_sample = us.reshape(dec_batch, dec_seq, D_A)[None, :, dec_seq - CONV_HIST:, :]

    def to_tm(a):
        return a.transpose(1, 0, 2).reshape(1, a.shape[1] * dec_batch, a.shape[2])

    def from_tm(a, rows):
        return a.reshape(rows, dec_batch, a.shape[-1]).transpose(1, 0, 2)

    x1t = to_tm(x1s.reshape(dec_batch, dec_seq, D_MODEL))[0]
    x2t, ffn0_s = _ffn(x1t, to_tm(state_ffn[0]), p, 0, wgu0, wd0, tl=SAMPLE_FFN_TILE, ns=dec_batch,
                       final=False, nsplit=SAMPLE_FFN_PARTS)
    x3t, pool_s = _pool(x2t, to_tm(state_pool[0]), p["norm_mix"][1], p["pool_w"], p["pool_scale"],
                        tl=dec_batch * dec_seq, ns=dec_batch, start=PAST_LEN)
    ys, ffn1_s = _ffn(x3t, to_tm(state_ffn[1]), p, 1, wgu1, wd1, tl=SAMPLE_FFN_TILE, ns=dec_batch,
                      final=True, nsplit=SAMPLE_FFN_PARTS)
    y_sample = from_tm(ys, dec_seq)
    pool_sample = from_tm(pool_s[0], POOL_BUF)[None]
    ffn_sample = jnp.stack([from_tm(ffn0_s[0], CONV_HIST), from_tm(ffn1_s[0], CONV_HIST)])

    return (y_prompt, y_sample, conv_prompt, conv_sample, hgrn_prompt, hgrn_s[None],
            pool_prompt, pool_sample, ffn_prompt, ffn_sample)
```
